```python
import math
import jax, jax.numpy as jnp
from jax import lax
import numpy as np

D_MODEL = 1024
BATCH = 16
SEQ = 2048
DEPTH = 2
DEC_BATCH = 128
DEC_SEQ = 4
PAST_LEN = 8192
PAGE_SIZE = 128

N_A_LAYERS = DEPTH // 2
N_B_LAYERS = DEPTH - N_A_LAYERS
N_DENSE = (DEPTH + 1) // 2
N_MOE = DEPTH // 2

A_HEADS = 8
A_DQK = D_MODEL // (2 * A_HEADS)
A_DV = D_MODEL // A_HEADS
A_CHUNK = 64
GATE_CAP = 15.0
A_PROJ = A_HEADS * (2 * A_DQK + 2 * A_DV + 2)

B_HEADS = 8
Q_LORA = 384
KV_LORA = 256
NOPE_DIM = 128
ROPE_DIM = 64
B_DV = 128
ROPE_THETA = 10000.0
Q_BLOCK = 128
SOFTMAX_SCALE = (NOPE_DIM + ROPE_DIM) ** -0.5

D_FF = 2752
N_EXPERTS = 8
TOP_K = 2
D_FF_EXPERT = 3584
EPS = 1e-6

kernel_name = "yoco_mlstm_mla_moe_step"


def rms_norm(x, g):
    xf = x.astype(jnp.float32)
    y = xf * lax.rsqrt(jnp.mean(xf * xf, axis=-1, keepdims=True) + EPS)
    return (y * g.astype(jnp.float32)).astype(x.dtype)


def rope(x, pos):
    half = ROPE_DIM // 2
    inv = ROPE_THETA ** (-jnp.arange(half, dtype=jnp.float32) / half)
    ang = pos.astype(jnp.float32)[:, None] * inv[None, :]
    shape = (ang.shape[0],) + (1,) * (x.ndim - 3) + (half,)
    cos, sin = jnp.cos(ang).reshape(shape), jnp.sin(ang).reshape(shape)
    x1 = x[..., :half].astype(jnp.float32)
    x2 = x[..., half:].astype(jnp.float32)
    return jnp.concatenate([x1 * cos - x2 * sin, x2 * cos + x1 * sin], axis=-1).astype(x.dtype)


def _mlstm_chunk(carry, inp):
    C, n, m = carry
    q, k, v, li, lf = inp
    L = q.shape[1]
    a = jnp.swapaxes(jnp.cumsum(lf, axis=1), 1, 2)
    li = jnp.swapaxes(li, 1, 2)
    causal = jnp.tril(jnp.ones((L, L), dtype=bool))
    d = jnp.where(causal, a[..., :, None] - a[..., None, :] + li[..., None, :], -jnp.inf)
    inter = a + m[..., None]
    m_t = jnp.maximum(inter, jnp.max(d, axis=-1))
    w = jnp.exp(d - m_t[..., None]) * jnp.einsum('bthk,bshk->bhts', q, k)
    e_inter = jnp.exp(inter - m_t)
    num = jnp.einsum('bhts,bshv->bthv', w, v) + jnp.einsum('bht,bthk,bhkv->bthv', e_inter, q, C)
    den = jnp.sum(w, axis=-1) + e_inter * jnp.einsum('bthk,bhk->bht', q, n)
    h = num / jnp.swapaxes(jnp.maximum(jnp.abs(den), jnp.exp(-m_t)), 1, 2)[..., None]
    m_new = m_t[..., -1]
    e_end = jnp.exp(a[..., -1:] - a + li - m_new[..., None])
    e_carry = jnp.exp(inter[..., -1] - m_new)
    C_new = e_carry[..., None, None] * C + jnp.einsum('bhs,bshk,bshv->bhkv', e_end, k, v)
    n_new = e_carry[..., None] * n + jnp.einsum('bhs,bshk->bhk', e_end, k)
    return (C_new, n_new, m_new), h


def mlstm_recurrence(q, k, v, logi, logf, C0, n0, m0):
    B, T = q.shape[:2]
    L = math.gcd(T, A_CHUNK)
    nc = T // L

    def chunks(x):
        return jnp.moveaxis(x.astype(jnp.float32).reshape((B, nc, L) + x.shape[2:]), 1, 0)

    q = q.astype(jnp.float32) * (A_DQK ** -0.5)
    carry0 = (C0.astype(jnp.float32), n0.astype(jnp.float32), m0.astype(jnp.float32))
    (C, n, m), h = lax.scan(_mlstm_chunk, carry0, (chunks(q), chunks(k), chunks(v), chunks(logi), chunks(logf)))
    h = jnp.moveaxis(h, 0, 1).reshape(B, T, A_HEADS, A_DV)
    return h, C, n, m


def mlstm_layer(x, C0, n0, m0, g_norm, w_in, b_gate, g_head, w_out):
    B, T, _ = x.shape
    xn = rms_norm(x, g_norm)
    proj = xn @ w_in
    hq, hv = A_HEADS * A_DQK, A_HEADS * A_DV
    q, k, v, o, gates = jnp.split(proj, [hq, 2 * hq, 2 * hq + hv, 2 * hq + 2 * hv], axis=-1)
    q = q.reshape(B, T, A_HEADS, A_DQK)
    k = k.reshape(B, T, A_HEADS, A_DQK)
    v = v.reshape(B, T, A_HEADS, A_DV)
    gates = gates.astype(jnp.float32) + b_gate.astype(jnp.float32)
    gates = GATE_CAP * jnp.tanh(gates / GATE_CAP)
    logi = gates[..., :A_HEADS]
    logf = jax.nn.log_sigmoid(gates[..., A_HEADS:])
    h, C, n, m = mlstm_recurrence(q, k, v, logi, logf, C0, n0, m0)
    h = rms_norm(h.astype(x.dtype), g_head).reshape(B, T, hv)
    y = (jax.nn.sigmoid(o) * h) @ w_out
    return y, C, n, m


def shared_latent_kv(h, pos, g_kv, w_dkv, g_ckv):
    hn = rms_norm(h, g_kv)
    ckv, kr = jnp.split(hn @ w_dkv, [KV_LORA], axis=-1)
    return rms_norm(ckv, g_ckv), rope(kr, pos)


def mla_attend(q_lat, q_rope, ckv, krope, q_pos, k_pos):
    B, T = q_lat.shape[:2]
    qb = math.gcd(T, Q_BLOCK)
    nb = T // qb

    def blocks(a):
        return jnp.moveaxis(a.reshape((B, nb, qb) + a.shape[2:]), 1, 0)

    def one_block(args):
        ql, qr, qp = args
        s = (jnp.einsum('bthc,bsc->bhts', ql, ckv) + jnp.einsum('bthr,bsr->bhts', qr, krope)).astype(jnp.float32) * SOFTMAX_SCALE
        s = jnp.where(k_pos[None, :] <= qp[:, None], s, -jnp.inf)
        p = jax.nn.softmax(s, axis=-1).astype(ckv.dtype)
        return jnp.einsum('bhts,bsc->bthc', p, ckv)

    out = lax.map(one_block, (blocks(q_lat), blocks(q_rope), q_pos.reshape(nb, qb)))
    return jnp.moveaxis(out, 0, 1).reshape(B, T, B_HEADS, KV_LORA)


def mla_layer(x, pos, keys_ckv, keys_krope, k_pos, g_norm, w_dq, g_q, w_uq, w_uk, w_uv, w_o):
    B, T, _ = x.shape
    xn = rms_norm(x, g_norm)
    cq = rms_norm(xn @ w_dq, g_q)
    q = (cq @ w_uq).reshape(B, T, B_HEADS, NOPE_DIM + ROPE_DIM)
    q_nope, q_rope = q[..., :NOPE_DIM], rope(q[..., NOPE_DIM:], pos)
    q_lat = jnp.einsum('bthn,chn->bthc', q_nope, w_uk)
    o_lat = mla_attend(q_lat, q_rope, keys_ckv, keys_krope, pos, k_pos)
    o = jnp.einsum('bthc,chv->bthv', o_lat, w_uv).reshape(B, T, B_HEADS * B_DV)
    return o @ w_o


def dense_swiglu(x, g_norm, w_gate, w_up, w_down):
    xn = rms_norm(x, g_norm)
    return (jax.nn.silu(xn @ w_gate) * (xn @ w_up)) @ w_down


def moe_swiglu(x, g_norm, w_router, w_gate, w_up, w_down):
    xn = rms_norm(x, g_norm)
    logits = (xn @ w_router).astype(jnp.float32)
    top_v, top_i = lax.top_k(logits, TOP_K)
    top_w = jax.nn.softmax(top_v, axis=-1)
    gate = jnp.sum(jax.nn.one_hot(top_i, N_EXPERTS, dtype=jnp.float32) * top_w[..., None], axis=-2).astype(x.dtype)
    y = jnp.zeros_like(x)
    for e in range(N_EXPERTS):
        he = jax.nn.silu(xn @ w_gate[e]) * (xn @ w_up[e])
        y = y + gate[..., e:e + 1] * (he @ w_down[e])
    return y


def run_trunk(x, pos, C_in, n_in, m_in, past_ckv, past_krope, p):
    new_C, new_n, new_m = [], [], []
    ckv_new = krope_new = keys_ckv = keys_krope = k_pos = None
    i_dense = i_moe = 0
    for layer in range(DEPTH):
        if layer < N_A_LAYERS:
            a = layer
            y, C, n, m = mlstm_layer(x, C_in[a], n_in[a], m_in[a], p['g_norm_a'][a], p['w_in_a'][a],
                                     p['b_gate_a'][a], p['g_head_a'][a], p['w_out_a'][a])
            new_C.append(C); new_n.append(n); new_m.append(m)
        else:
            if layer == N_A_LAYERS:
                ckv_new, krope_new = shared_latent_kv(x, pos, p['g_kv'], p['w_dkv'], p['g_ckv'])
                if past_ckv is None:
                    keys_ckv, keys_krope, k_pos = ckv_new, krope_new, pos
                else:
                    keys_ckv = jnp.concatenate([past_ckv, ckv_new], axis=1)
                    keys_krope = jnp.concatenate([past_krope, krope_new], axis=1)
                    k_pos = jnp.concatenate([jnp.arange(past_ckv.shape[1], dtype=jnp.int32), pos])
            b = layer - N_A_LAYERS
            y = mla_layer(x, pos, keys_ckv, keys_krope, k_pos, p['g_norm_b'][b], p['w_dq'][b], p['g_q'][b],
                          p['w_uq'][b], p['w_uk'], p['w_uv'], p['w_o_b'][b])
        x = x + y
        if layer % 2 == 0:
            x = x + dense_swiglu(x, p['g_ffn_d'][i_dense], p['w_gate_d'][i_dense], p['w_up_d'][i_dense], p['w_down_d'][i_dense])
            i_dense += 1
        else:
            x = x + moe_swiglu(x, p['g_ffn_m'][i_moe], p['w_router'][i_moe], p['w_gate_m'][i_moe],
                               p['w_up_m'][i_moe], p['w_down_m'][i_moe])
            i_moe += 1
    return rms_norm(x, p['g_final']), jnp.stack(new_C), jnp.stack(new_n), jnp.stack(new_m), ckv_new, krope_new


def setup_inputs(seed: int = 0) -> dict:
    key = jax.random.key(seed)
    ks = iter(jax.random.split(key, 48))

    def nrm(shape, scale):
        return scale * jax.random.normal(next(ks), shape, jnp.float32)

    def gain(shape):
        return 1.0 + nrm(shape, 0.02)

    n_pages = PAST_LEN // PAGE_SIZE
    n_used = DEC_BATCH * n_pages
    n_phys = n_used + n_used // 4
    perm = jax.random.permutation(next(ks), n_phys)
    page_table = perm[:n_used].reshape(DEC_BATCH, n_pages).astype(jnp.int32)
    forget_bias = jnp.broadcast_to(jnp.linspace(3.0, 6.0, A_HEADS, dtype=jnp.float32), (N_A_LAYERS, A_HEADS))
    b_gate_a = jnp.concatenate([nrm((N_A_LAYERS, A_HEADS), 0.1), forget_bias + nrm((N_A_LAYERS, A_HEADS), 0.1)], axis=-1)
    return {
        "x_prompt": nrm((BATCH, SEQ, D_MODEL), 1.0),
        "x_sample": nrm((DEC_BATCH, DEC_SEQ, D_MODEL), 1.0),
        "state_C": nrm((N_A_LAYERS, DEC_BATCH, A_HEADS, A_DQK, A_DV), 0.5),
        "state_n": nrm((N_A_LAYERS, DEC_BATCH, A_HEADS, A_DQK), 0.5),
        "state_m": nrm((N_A_LAYERS, DEC_BATCH, A_HEADS), 1.0),
        "cache_ckv": nrm((n_phys, PAGE_SIZE, KV_LORA), 1.0),
        "cache_krope": nrm((n_phys, PAGE_SIZE, ROPE_DIM), 1.0),
        "page_table": page_table,
        "g_norm_a": gain((N_A_LAYERS, D_MODEL)),
        "w_in_a": nrm((N_A_LAYERS, D_MODEL, A_PROJ), D_MODEL ** -0.5),
        "b_gate_a": b_gate_a,
        "g_head_a": gain((N_A_LAYERS, A_HEADS, A_DV)),
        "w_out_a": nrm((N_A_LAYERS, A_HEADS * A_DV, D_MODEL), (A_HEADS * A_DV) ** -0.5),
        "g_kv": gain((D_MODEL,)),
        "w_dkv": nrm((D_MODEL, KV_LORA + ROPE_DIM), D_MODEL ** -0.5),
        "g_ckv": gain((KV_LORA,)),
        "w_uk": nrm((KV_LORA, B_HEADS, NOPE_DIM), KV_LORA ** -0.5),
        "w_uv": nrm((KV_LORA, B_HEADS, B_DV), KV_LORA ** -0.5),
        "g_norm_b": gain((N_B_LAYERS, D_MODEL)),
        "w_dq": nrm((N_B_LAYERS, D_MODEL, Q_LORA), D_MODEL ** -0.5),
        "g_q": gain((N_B_LAYERS, Q_LORA)),
        "w_uq": nrm((N_B_LAYERS, Q_LORA, B_HEADS * (NOPE_DIM + ROPE_DIM)), Q_LORA ** -0.5),
        "w_o_b": nrm((N_B_LAYERS, B_HEADS * B_DV, D_MODEL), (B_HEADS * B_DV) ** -0.5),
        "g_ffn_d": gain((N_DENSE, D_MODEL)),
        "w_gate_d": nrm((N_DENSE, D_MODEL, D_FF), D_MODEL ** -0.5),
        "w_up_d": nrm((N_DENSE, D_MODEL, D_FF), D_MODEL ** -0.5),
        "w_down_d": nrm((N_DENSE, D_FF, D_MODEL), D_FF ** -0.5),
        "g_ffn_m": gain((N_MOE, D_MODEL)),
        "w_router": nrm((N_MOE, D_MODEL, N_EXPERTS), D_MODEL ** -0.5),
        "w_gate_m": nrm((N_MOE, N_EXPERTS, D_MODEL, D_FF_EXPERT), D_MODEL ** -0.5),
        "w_up_m": nrm((N_MOE, N_EXPERTS, D_MODEL, D_FF_EXPERT), D_MODEL ** -0.5),
        "w_down_m": nrm((N_MOE, N_EXPERTS, D_FF_EXPERT, D_MODEL), D_FF_EXPERT ** -0.5),
        "g_final": gain((D_MODEL,)),
    }


def reference(x_prompt, x_sample, state_C, state_n, state_m, cache_ckv, cache_krope, page_table,
              g_norm_a, w_in_a, b_gate_a, g_head_a, w_out_a,
              g_kv, w_dkv, g_ckv, w_uk, w_uv,
              g_norm_b, w_dq, g_q, w_uq, w_o_b,
              g_ffn_d, w_gate_d, w_up_d, w_down_d,
              g_ffn_m, w_router, w_gate_m, w_up_m, w_down_m,
              g_final):
    p = dict(g_norm_a=g_norm_a, w_in_a=w_in_a, b_gate_a=b_gate_a, g_head_a=g_head_a, w_out_a=w_out_a,
             g_kv=g_kv, w_dkv=w_dkv, g_ckv=g_ckv, w_uk=w_uk, w_uv=w_uv,
             g_norm_b=g_norm_b, w_dq=w_dq, g_q=g_q, w_uq=w_uq, w_o_b=w_o_b,
             g_ffn_d=g_ffn_d, w_gate_d=w_gate_d, w_up_d=w_up_d, w_down_d=w_down_d,
             g_ffn_m=g_ffn_m, w_router=w_router, w_gate_m=w_gate_m, w_up_m=w_up_m, w_down_m=w_down_m,
             g_final=g_final)

    bp, tp = x_prompt.shape[:2]
    pos_p = jnp.arange(tp, dtype=jnp.int32)
    C0 = jnp.zeros((N_A_LAYERS, bp, A_HEADS, A_DQK, A_DV), jnp.float32)
    n0 = jnp.zeros((N_A_LAYERS, bp, A_HEADS, A_DQK), jnp.float32)
    m0 = jnp.zeros((N_A_LAYERS, bp, A_HEADS), jnp.float32)
    y_prompt, C_prompt, n_prompt, m_prompt, ckv_prompt, krope_prompt = run_trunk(
        x_prompt, pos_p, C0, n0, m0, None, None, p)

    bs, ts = x_sample.shape[:2]
    n_pages = page_table.shape[1]
    past_ckv = cache_ckv[page_table].reshape(bs, n_pages * PAGE_SIZE, KV_LORA)
    past_krope = cache_krope[page_table].reshape(bs, n_pages * PAGE_SIZE, ROPE_DIM)
    pos_s = n_pages * PAGE_SIZE + jnp.arange(ts, dtype=jnp.int32)
    y_sample, C_sample, n_sample, m_sample, ckv_sample, krope_sample = run_trunk(
        x_sample, pos_s, state_C, state_n, state_m, past_ckv, past_krope, p)

    return (y_prompt, y_sample, C_prompt, n_prompt, m_prompt, ckv_prompt, krope_prompt,
            C_sample, n_sample, m_sample, ckv_sample, krope_sample)
```

```python
import functools

import jax
import jax.numpy as jnp
from jax import lax
from jax.experimental import pallas as pl
from jax.experimental.pallas import tpu as pltpu

F32 = jnp.float32
BF16 = jnp.bfloat16

EPS = 1e-6
GATE_CAP = 15.0
ROPE_THETA = 10000.0
NEG = -1e30
LANES = 128
VMEM_LIMIT_BYTES = 56 * 2**20

SAMPLE_PAD = 8
TOKEN_TILE = 512
ATTN_TILE = 256
MLSTM_CHUNK = 256
FFN_CHUNK = 256
MOE_TILE = 512
MOE_CHUNK = 512
PAGES_PER_STEP = 8


def _cparams(*sem):
    return pltpu.CompilerParams(dimension_semantics=sem, vmem_limit_bytes=VMEM_LIMIT_BYTES)


def _dot(a, b):
    return jnp.dot(a, b, preferred_element_type=F32)


def _dot_nt(a, b):
    return lax.dot_general(a, b, (((1,), (1,)), ((), ())), preferred_element_type=F32)


def _rms(x, g):
    return x * lax.rsqrt(jnp.mean(x * x, axis=-1, keepdims=True) + EPS) * g


def _sigmoid(x):
    return 1.0 / (1.0 + jnp.exp(-x))


def _full(shape):
    nd = len(shape)
    return pl.BlockSpec(shape, lambda *_: (0,) * nd)


def _in_proj_kernel(x_ref, g_ref, wq_ref, wk_ref, wv_ref, wo_ref, wg_ref, b_ref,
                    q_ref, k_ref, v_ref, o_ref, gate_ref, *, n_heads, q_scale):
    xn = _rms(x_ref[...], g_ref[...]).astype(BF16)
    q_ref[...] = _dot(xn, wq_ref[...]) * q_scale
    k_ref[...] = _dot(xn, wk_ref[...])
    v_ref[...] = _dot(xn, wv_ref[...])
    o_ref[...] = _dot(xn, wo_ref[...])
    g = _dot(xn, wg_ref[...]) + b_ref[...]
    g = GATE_CAP * jnp.tanh(g / GATE_CAP)
    logf = jnp.minimum(g, 0.0) - jnp.log1p(jnp.exp(-jnp.abs(g)))
    lane = lax.broadcasted_iota(jnp.int32, g.shape, 1)
    gate_ref[...] = jnp.where(lane < n_heads, g, jnp.where(lane < 2 * n_heads, logf, 0.0))


def _in_proj(x, g, wq, wk, wv, wo, wg, b, *, n_heads, q_scale):
    n, d = x.shape
    tm = TOKEN_TILE
    row = lambda w: pl.BlockSpec((tm, w), lambda i: (i, 0))
    outs = [(wq.shape[1], F32), (wk.shape[1], F32), (wv.shape[1], F32), (wo.shape[1], F32), (LANES, F32)]
    return pl.pallas_call(
        functools.partial(_in_proj_kernel, n_heads=n_heads, q_scale=q_scale),
        grid=(n // tm,),
        in_specs=[row(d), _full(g.shape), _full(wq.shape), _full(wk.shape), _full(wv.shape),
                  _full(wo.shape), _full(wg.shape), _full(b.shape)],
        out_specs=[row(w) for w, _ in outs],
        out_shape=[jax.ShapeDtypeStruct((n, w), dt) for w, dt in outs],
        compiler_params=_cparams("parallel"),
        name="in_proj",
    )(x, g, wq, wk, wv, wo, wg, b)


def _mlstm_kernel(*refs, L, H, DK, DV, t_valid, has_state, mm_dtype):
    if has_state:
        (q_ref, k_ref, v_ref, o_ref, gate_ref, gh_ref, c0_ref, n0_ref, m0_ref, _,
         hg_ref, cout_ref, nout_ref, mout_ref, caug_ref, m_scr) = refs
    else:
        (q_ref, k_ref, v_ref, o_ref, gate_ref, gh_ref,
         hg_ref, cout_ref, nout_ref, mout_ref, caug_ref, m_scr) = refs
    c = pl.program_id(1)
    last = pl.num_programs(1) - 1

    rk = lax.broadcasted_iota(jnp.int32, (DK, DK), 0)
    ck = lax.broadcasted_iota(jnp.int32, (DK, DK), 1)
    eye_k = rk == ck

    @pl.when(c == 0)
    def _():
        if has_state:
            for h in range(H):
                caug_ref[h, :, :DV] = c0_ref[0, h]
                n_row = n0_ref[0, h:h + 1, :]
                n_col = jnp.sum(jnp.where(eye_k, jnp.broadcast_to(n_row, (DK, DK)), 0.0),
                                axis=1, keepdims=True)
                caug_ref[h, :, DV:] = jnp.broadcast_to(n_col, (DK, DV))
                m_scr[h:h + 1, :] = jnp.broadcast_to(m0_ref[0, :, h:h + 1], (1, LANES))
        else:
            caug_ref[...] = jnp.zeros_like(caug_ref)
            m_scr[...] = jnp.zeros_like(m_scr)

    gates = gate_ref[...]
    if t_valid < L:
        t_id = lax.broadcasted_iota(jnp.int32, gates.shape, 0)
        lane = lax.broadcasted_iota(jnp.int32, gates.shape, 1)
        gates = jnp.where(t_id < t_valid, gates, jnp.where(lane < H, NEG, 0.0))

    ri = lax.broadcasted_iota(jnp.int32, (L, L), 0)
    ci = lax.broadcasted_iota(jnp.int32, (L, L), 1)
    causal = ci <= ri
    eye = ci == ri
    ones_v = jnp.ones((L, DV), mm_dtype)

    for h in range(H):
        li_col = gates[:, h:h + 1]
        lf_col = gates[:, H + h:H + h + 1]
        lf_b = jnp.broadcast_to(lf_col, (L, L))
        lf_row = jnp.sum(jnp.where(eye, lf_b, 0.0), axis=0, keepdims=True)
        a_col = jnp.sum(jnp.where(causal, jnp.broadcast_to(lf_row, (L, L)), 0.0),
                        axis=1, keepdims=True)
        a_row = jnp.sum(jnp.where(ci >= ri, lf_b, 0.0), axis=0, keepdims=True)
        li_row = jnp.sum(jnp.where(eye, jnp.broadcast_to(li_col, (L, L)), 0.0),
                         axis=0, keepdims=True)
        b_row = li_row - a_row
        d = jnp.where(causal, a_col + b_row, NEG)
        m_prev = m_scr[h:h + 1, 0:1]
        inter = a_col + m_prev
        m_t = jnp.maximum(inter, jnp.max(d, axis=1, keepdims=True))

        qh = q_ref[:, h * DK:(h + 1) * DK].astype(mm_dtype)
        kh = k_ref[:, h * DK:(h + 1) * DK].astype(mm_dtype)
        vaug = jnp.concatenate([v_ref[:, h * DV:(h + 1) * DV].astype(mm_dtype), ones_v], axis=1)
        caug = caug_ref[h]

        w = jnp.exp(d - m_t) * _dot_nt(qh, kh)
        e_inter = jnp.exp(inter - m_t)
        num = _dot(w.astype(mm_dtype), vaug) + e_inter * _dot(qh, caug.astype(mm_dtype))
        hh = num[:, :DV] / jnp.maximum(jnp.abs(num[:, DV:]), jnp.exp(-m_t))
        hn = _rms(hh, gh_ref[h:h + 1, :])
        hg_ref[:, h * DV:(h + 1) * DV] = _sigmoid(o_ref[:, h * DV:(h + 1) * DV]) * hn

        m_new = m_t[L - 1:L, :]
        a_last = a_col[L - 1:L, :]
        e_end = jnp.exp(a_last + b_row - m_new)
        e_carry = jnp.exp(a_last + m_prev - m_new)
        k_t = _dot_nt(eye_k.astype(mm_dtype), kh)
        caug_new = e_carry * caug + _dot((k_t * e_end).astype(mm_dtype), vaug)
        caug_ref[h] = caug_new
        m_scr[h:h + 1, :] = jnp.broadcast_to(m_new, (1, LANES))

        @pl.when(c == last)
        def _():
            cout_ref[0, h] = caug_new[:, :DV]
            nout_ref[0, h:h + 1, :] = jnp.sum(jnp.where(eye_k, caug_new[:, DV:DV + DK], 0.0),
                                              axis=0, keepdims=True)
            mout_ref[0, :, h:h + 1] = m_new


def _mlstm(q, k, v, o, gates, g_head, *, row0, B, T, L, t_valid, state=None, hg_prev=None):
    n = q.shape[0]
    H, DV = g_head.shape
    DK = q.shape[1] // H
    nc = T // L
    blk0 = row0 // L
    row = lambda w: pl.BlockSpec((L, w), lambda b, c: (blk0 + b * nc + c, 0))
    in_specs = [row(H * DK), row(H * DK), row(H * DV), row(H * DV), row(LANES), _full(g_head.shape)]
    args = [q, k, v, o, gates, g_head]
    aliases = {}
    if state is not None:
        c0, n0, m0 = state
        in_specs += [pl.BlockSpec((1, H, DK, DV), lambda b, c: (b, 0, 0, 0)),
                     pl.BlockSpec((1, H, DK), lambda b, c: (b, 0, 0)),
                     pl.BlockSpec((1, 1, H), lambda b, c: (b, 0, 0)),
                     pl.BlockSpec(memory_space=pl.ANY)]
        args += [c0, n0, m0.reshape(B, 1, H), hg_prev]
        aliases = {len(args) - 1: 0}
    out_shape = [jax.ShapeDtypeStruct((n, H * DV), F32),
                 jax.ShapeDtypeStruct((B, H, DK, DV), F32),
                 jax.ShapeDtypeStruct((B, H, DK), F32),
                 jax.ShapeDtypeStruct((B, 1, H), F32)]
    out_specs = [row(H * DV),
                 pl.BlockSpec((1, H, DK, DV), lambda b, c: (b, 0, 0, 0)),
                 pl.BlockSpec((1, H, DK), lambda b, c: (b, 0, 0)),
                 pl.BlockSpec((1, 1, H), lambda b, c: (b, 0, 0))]
    kern = functools.partial(_mlstm_kernel, L=L, H=H, DK=DK, DV=DV, t_valid=t_valid,
                             has_state=state is not None,
                             mm_dtype=BF16 if L % 16 == 0 else F32)
    hg, c_out, n_out, m_out = pl.pallas_call(
        kern, grid=(B, nc), in_specs=in_specs, out_specs=out_specs, out_shape=out_shape,
        scratch_shapes=[pltpu.VMEM((H, DK, 2 * DV), F32), pltpu.VMEM((H, LANES), F32)],
        input_output_aliases=aliases,
        compiler_params=_cparams("parallel", "arbitrary"),
        name="mlstm_sample" if state is not None else "mlstm_prompt",
    )(*args)
    return hg, c_out, n_out, m_out.reshape(B, H)


def _ffn_kernel(x_ref, hg_ref, wout_ref, g_ref, wg_ref, wu_ref, wd_ref, out_ref, acc_ref):
    x1 = x_ref[...] + _dot(hg_ref[...].astype(BF16), wout_ref[...])
    xn = _rms(x1, g_ref[...]).astype(BF16)
    acc_ref[...] = jnp.zeros_like(acc_ref)

    def body(c, carry):
        gate = _dot(xn, wg_ref[c])
        up = _dot(xn, wu_ref[c])
        hmid = (gate * _sigmoid(gate) * up).astype(BF16)
        acc_ref[...] += _dot(hmid, wd_ref[c])
        return carry

    lax.fori_loop(0, wg_ref.shape[0], body, 0)
    out_ref[...] = x1 + acc_ref[...]


def _ffn(x, hg, wout, g, wg, wu, wd):
    n, d = x.shape
    tm = TOKEN_TILE
    row = pl.BlockSpec((tm, d), lambda i: (i, 0))
    return pl.pallas_call(
        _ffn_kernel, grid=(n // tm,),
        in_specs=[row, row, _full(wout.shape), _full(g.shape), _full(wg.shape), _full(wu.shape),
                  _full(wd.shape)],
        out_specs=row, out_shape=jax.ShapeDtypeStruct((n, d), F32),
        scratch_shapes=[pltpu.VMEM((tm, d), F32)],
        compiler_params=_cparams("parallel"),
        name="outproj_ffn",
    )(x, hg, wout, g, wg, wu, wd)


def _latq_kernel(x_ref, cos_ref, sin_ref, gkv_ref, wdkv_ref, gckv_ref, gnb_ref, wdq_ref, gq_ref,
                 wn_ref, wr_ref, wrr_ref, wuk_ref,
                 ckv_ref, kr_ref, kcat_ref, q_ref, *, n_heads, kv_lora, rope_dim, nope_dim):
    x = x_ref[...]
    xs = x * lax.rsqrt(jnp.mean(x * x, axis=-1, keepdims=True) + EPS)
    cos = cos_ref[...]
    sin = sin_ref[...]
    lane = lax.broadcasted_iota(jnp.int32, cos.shape, 1)
    lo = lane < rope_dim

    lat = _dot((xs * gkv_ref[...]).astype(BF16), wdkv_ref[...])
    ckv = _rms(lat[:, :kv_lora], gckv_ref[...])
    ckv_ref[...] = ckv
    t = lat[:, kv_lora:] * jnp.where(lo, cos, sin)
    kr2 = t + pltpu.roll(t, rope_dim, axis=1)
    kr_ref[...] = kr2[:, :rope_dim]
    kcat_ref[...] = jnp.concatenate([ckv, kr2], axis=1).astype(BF16)

    cq = _dot((xs * gnb_ref[...]).astype(BF16), wdq_ref[...])
    cqn = _rms(cq, gq_ref[...]).astype(BF16)
    qn = _dot(cqn, wn_ref[...]).astype(BF16)
    reps = n_heads * rope_dim // LANES
    cos_h = jnp.concatenate([cos] * reps, axis=1)
    sin_h = jnp.concatenate([sin] * reps, axis=1)
    qr = _dot(cqn, wr_ref[...]) * cos_h + _dot(cqn, wrr_ref[...]) * sin_h
    for h in range(n_heads):
        ql = _dot(qn[:, h * nope_dim:(h + 1) * nope_dim], wuk_ref[h])
        pair = qr[:, (h // 2) * LANES:(h // 2 + 1) * LANES]
        slot = jnp.where(lo if h % 2 == 0 else jnp.logical_not(lo), pair, 0.0)
        q_ref[0, h] = jnp.concatenate([ql, slot], axis=1).astype(BF16)


def _latq(x, cos_tab, sin_tab, n_prompt_blocks, blocks_per_seq, weights, *, n_heads, kv_lora,
          rope_dim, nope_dim):
    n, d = x.shape
    tm = ATTN_TILE
    kw = kv_lora + 2 * rope_dim

    def tab_map(i):
        return (jnp.where(i < n_prompt_blocks, i % blocks_per_seq, blocks_per_seq), 0)

    tab = pl.BlockSpec((tm, LANES), tab_map)
    row = lambda w: pl.BlockSpec((tm, w), lambda i: (i, 0))
    return pl.pallas_call(
        functools.partial(_latq_kernel, n_heads=n_heads, kv_lora=kv_lora, rope_dim=rope_dim,
                          nope_dim=nope_dim),
        grid=(n // tm,),
        in_specs=[row(d), tab, tab] + [_full(w.shape) for w in weights],
        out_specs=[row(kv_lora), row(rope_dim), row(kw),
                   pl.BlockSpec((1, n_heads, tm, kw), lambda i: (i, 0, 0, 0))],
        out_shape=[jax.ShapeDtypeStruct((n, kv_lora), F32),
                   jax.ShapeDtypeStruct((n, rope_dim), F32),
                   jax.ShapeDtypeStruct((n, kw), BF16),
                   jax.ShapeDtypeStruct((n // tm, n_heads, tm, kw), BF16)],
        compiler_params=_cparams("parallel"),
        name="latent_q",
    )(x, cos_tab, sin_tab, *weights)


def _attn_prompt_kernel(q_ref, k_ref, o_ref, m_scr, l_scr, acc_scr, *, tq, kv_lora, scale):
    qi = pl.program_id(1)
    n_heads = q_ref.shape[1]
    rows = n_heads * tq
    q = q_ref[0].reshape(rows, q_ref.shape[3])
    m_scr[...] = jnp.full_like(m_scr, NEG)
    l_scr[...] = jnp.zeros_like(l_scr)
    acc_scr[...] = jnp.zeros_like(acc_scr)

    def step(j, masked):
        kc = k_ref[pl.ds(pl.multiple_of(j * tq, tq), tq), :]
        s = _dot_nt(q, kc) * scale
        if masked:
            t = lax.broadcasted_iota(jnp.int32, s.shape, 0) % tq
            jj = lax.broadcasted_iota(jnp.int32, s.shape, 1)
            s = jnp.where(jj <= t, s, NEG)
        m_prev = m_scr[...]
        m_new = jnp.maximum(m_prev, jnp.max(s, axis=1, keepdims=True))
        p = jnp.exp(s - m_new)
        alpha = jnp.exp(m_prev - m_new)
        l_scr[...] = alpha * l_scr[...] + jnp.sum(p, axis=1, keepdims=True)
        acc_scr[...] = alpha * acc_scr[...] + _dot(p.astype(BF16), kc[:, :kv_lora])
        m_scr[...] = m_new

    def body(j, carry):
        step(j, False)
        return carry

    lax.fori_loop(0, qi, body, 0)
    step(qi, True)
    o = acc_scr[...] / l_scr[...]
    for h in range(n_heads):
        o_ref[:, h * kv_lora:(h + 1) * kv_lora] = o[h * tq:(h + 1) * tq].astype(BF16)


def _attn_prompt(q, kcat, n_rows, *, B, T, kv_lora, scale):
    tq = ATTN_TILE
    _, n_heads, _, kw = q.shape
    nq = T // tq
    return pl.pallas_call(
        functools.partial(_attn_prompt_kernel, tq=tq, kv_lora=kv_lora, scale=scale),
        grid=(B, nq),
        in_specs=[pl.BlockSpec((1, n_heads, tq, kw), lambda b, i: (b * nq + i, 0, 0, 0)),
                  pl.BlockSpec((T, kw), lambda b, i: (b, 0))],
        out_specs=pl.BlockSpec((tq, n_heads * kv_lora), lambda b, i: (b * nq + i, 0)),
        out_shape=jax.ShapeDtypeStruct((n_rows, n_heads * kv_lora), BF16),
        scratch_shapes=[pltpu.VMEM((n_heads * tq, 1), F32), pltpu.VMEM((n_heads * tq, 1), F32),
                        pltpu.VMEM((n_heads * tq, kv_lora), F32)],
        compiler_params=_cparams("parallel", "arbitrary"),
        name="attn_prompt",
    )(q, kcat)


def _attn_sample_kernel(pt_ref, q_ref, knew_ref, *refs, n_pages, t_valid, kv_lora, rope_dim, scale):
    ck_refs = refs[:n_pages]
    kr_refs = refs[n_pages:2 * n_pages]
    o_ref, m_scr, l_scr, acc_scr = refs[2 * n_pages:]
    g = pl.program_id(1)

    @pl.when(g == 0)
    def _():
        m_scr[...] = jnp.full_like(m_scr, NEG)
        l_scr[...] = jnp.zeros_like(l_scr)
        acc_scr[...] = jnp.zeros_like(acc_scr)

    q = q_ref[0]
    ql = q[:, :kv_lora]
    qr = q[:, kv_lora:]

    def update(s, values):
        m_prev = m_scr[...]
        m_new = jnp.maximum(m_prev, jnp.max(s, axis=1, keepdims=True))
        p = jnp.exp(s - m_new)
        alpha = jnp.exp(m_prev - m_new)
        l_scr[...] = alpha * l_scr[...] + jnp.sum(p, axis=1, keepdims=True)
        p = p.astype(BF16)
        pv = _dot(p[:, :values[0].shape[0]], values[0])
        for i in range(1, len(values)):
            rows = values[i].shape[0]
            pv += _dot(p[:, i * rows:(i + 1) * rows], values[i])
        acc_scr[...] = alpha * acc_scr[...] + pv
        m_scr[...] = m_new

    cks = [r[0].astype(BF16) for r in ck_refs]
    s = jnp.concatenate(
        [_dot_nt(ql, ck) + _dot_nt(qr, kr[0].astype(BF16)) for ck, kr in zip(cks, kr_refs)],
        axis=1) * scale
    update(s, cks)

    @pl.when(g == pl.num_programs(1) - 1)
    def _():
        kn = knew_ref[0]
        ckn = kn[:, :kv_lora]
        sn = (_dot_nt(ql, ckn) + _dot_nt(qr, kn[:, kv_lora:kv_lora + rope_dim])) * scale
        t = lax.broadcasted_iota(jnp.int32, sn.shape, 0) % t_valid
        j = lax.broadcasted_iota(jnp.int32, sn.shape, 1)
        update(jnp.where(j <= t, sn, NEG), [ckn])
        o_ref[0] = acc_scr[...] / l_scr[...]


def _attn_sample(page_table, q, knew, cache_ckv, cache_krope, *, t_valid, scale):
    nb, n_pages = page_table.shape
    _, rows, qw = q.shape
    _, page, kv_lora = cache_ckv.shape
    rope_dim = cache_krope.shape[2]
    pg = min(PAGES_PER_STEP, n_pages)
    assert n_pages % pg == 0
    ck_specs = [pl.BlockSpec((1, page, kv_lora),
                             functools.partial(lambda b, g, pt, i: (pt[b, g * pg + i], 0, 0), i=i))
                for i in range(pg)]
    kr_specs = [pl.BlockSpec((1, page, rope_dim),
                             functools.partial(lambda b, g, pt, i: (pt[b, g * pg + i], 0, 0), i=i))
                for i in range(pg)]
    grid_spec = pltpu.PrefetchScalarGridSpec(
        num_scalar_prefetch=1, grid=(nb, n_pages // pg),
        in_specs=[pl.BlockSpec((1, rows, qw), lambda b, g, pt: (b, 0, 0)),
                  pl.BlockSpec((1,) + knew.shape[1:], lambda b, g, pt: (b, 0, 0))]
        + ck_specs + kr_specs,
        out_specs=pl.BlockSpec((1, rows, kv_lora), lambda b, g, pt: (b, 0, 0)),
        scratch_shapes=[pltpu.VMEM((rows, 1), F32), pltpu.VMEM((rows, 1), F32),
                        pltpu.VMEM((rows, kv_lora), F32)])
    return pl.pallas_call(
        functools.partial(_attn_sample_kernel, n_pages=pg, t_valid=t_valid, kv_lora=kv_lora,
                          rope_dim=rope_dim, scale=scale),
        grid_spec=grid_spec,
        out_shape=jax.ShapeDtypeStruct((nb, rows, kv_lora), F32),
        compiler_params=_cparams("parallel", "arbitrary"),
        name="attn_sample",
    )(page_table, q, knew, *([cache_ckv] * pg), *([cache_krope] * pg))


def _attn_out_kernel(ol_ref, x_ref, wuv_ref, wo_ref, g_ref, wr_ref,
                     x3_ref, xn_ref, ids_ref, wts_ref, *, n_experts):
    n_heads, kv_lora, _ = wuv_ref.shape
    o = jnp.concatenate(
        [_dot(ol_ref[:, h * kv_lora:(h + 1) * kv_lora], wuv_ref[h]) for h in range(n_heads)],
        axis=1).astype(BF16)
    x3 = x_ref[...] + _dot(o, wo_ref[...])
    x3_ref[...] = x3
    xn = _rms(x3, g_ref[...]).astype(BF16)
    xn_ref[...] = xn
    logits = _dot(xn, wr_ref[...])
    lane = lax.broadcasted_iota(jnp.int32, logits.shape, 1)
    lane_f = lane.astype(F32)
    lg = jnp.where(lane < n_experts, logits, -jnp.inf)
    v1 = jnp.max(lg, axis=1, keepdims=True)
    i1 = jnp.min(jnp.where(lg == v1, lane_f, float(LANES)), axis=1, keepdims=True)
    lg2 = jnp.where(lane_f == i1, -jnp.inf, lg)
    v2 = jnp.max(lg2, axis=1, keepdims=True)
    i2 = jnp.min(jnp.where(lg2 == v2, lane_f, float(LANES)), axis=1, keepdims=True)
    e = jnp.exp(v2 - v1)
    w1 = 1.0 / (1.0 + e)
    w2 = e / (1.0 + e)
    ids_ref[...] = jnp.where(lane == 0, i1, jnp.where(lane == 1, i2, 0.0)).astype(jnp.int32)
    wts_ref[...] = jnp.where(lane == 0, w1, jnp.where(lane == 1, w2, 0.0))


def _attn_out(o_lat, x, wuv, wo, g, wr, *, n_experts):
    n, d = x.shape
    tm = TOKEN_TILE
    row = lambda w: pl.BlockSpec((tm, w), lambda i: (i, 0))
    return pl.pallas_call(
        functools.partial(_attn_out_kernel, n_experts=n_experts),
        grid=(n // tm,),
        in_specs=[row(o_lat.shape[1]), row(d), _full(wuv.shape), _full(wo.shape), _full(g.shape),
                  _full(wr.shape)],
        out_specs=[row(d), row(d), row(LANES), row(LANES)],
        out_shape=[jax.ShapeDtypeStruct((n, d), F32), jax.ShapeDtypeStruct((n, d), BF16),
                   jax.ShapeDtypeStruct((n, LANES), jnp.int32),
                   jax.ShapeDtypeStruct((n, LANES), F32)],
        compiler_params=_cparams("parallel"),
        name="attn_out_router",
    )(o_lat, x, wuv, wo, g, wr)


def _moe_kernel(te_ref, nu_ref, xs_ref, wg_ref, wu_ref, wd_ref, out_ref, *, chunk):
    t = pl.program_id(0)

    @pl.when(t < nu_ref[0])
    def _():
        xs = xs_ref[...]
        acc = jnp.zeros(out_ref.shape, F32)
        for c in range(wg_ref.shape[2] // chunk):
            sl = slice(c * chunk, (c + 1) * chunk)
            gate = _dot(xs, wg_ref[0, :, sl])
            up = _dot(xs, wu_ref[0, :, sl])
            acc += _dot((gate * _sigmoid(gate) * up).astype(BF16), wd_ref[0, sl, :])
        out_ref[...] = acc

    @pl.when(t >= nu_ref[0])
    def _():
        out_ref[...] = jnp.zeros_like(out_ref)


def _moe(tile_expert, n_used, xs, wg, wu, wd):
    rows, d = xs.shape
    tm = MOE_TILE
    f = wg.shape[2]
    one = pl.Buffered(1)
    grid_spec = pltpu.PrefetchScalarGridSpec(
        num_scalar_prefetch=2, grid=(rows // tm,),
        in_specs=[pl.BlockSpec((tm, d), lambda t, te, nu: (t, 0)),
                  pl.BlockSpec((1, d, f), lambda t, te, nu: (te[t], 0, 0), pipeline_mode=one),
                  pl.BlockSpec((1, d, f), lambda t, te, nu: (te[t], 0, 0), pipeline_mode=one),
                  pl.BlockSpec((1, f, d), lambda t, te, nu: (te[t], 0, 0), pipeline_mode=one)],
        out_specs=pl.BlockSpec((tm, d), lambda t, te, nu: (t, 0)))
    return pl.pallas_call(
        functools.partial(_moe_kernel, chunk=MOE_CHUNK),
        grid_spec=grid_spec,
        out_shape=jax.ShapeDtypeStruct((rows, d), F32),
        compiler_params=_cparams("arbitrary"),
        name="moe_experts",
    )(tile_expert, n_used, xs, wg, wu, wd)


def _final_kernel(x_ref, y1_ref, y2_ref, wts_ref, g_ref, out_ref):
    w = wts_ref[...]
    x4 = x_ref[...] + (w[:, 0:1] * y1_ref[...] + w[:, 1:2] * y2_ref[...])
    out_ref[...] = _rms(x4, g_ref[...])


def _final(x, y1, y2, wts, g):
    n, d = x.shape
    tm = TOKEN_TILE
    row = lambda w: pl.BlockSpec((tm, w), lambda i: (i, 0))
    return pl.pallas_call(
        _final_kernel, grid=(n // tm,),
        in_specs=[row(d), row(d), row(d), row(LANES), _full(g.shape)],
        out_specs=row(d), out_shape=jax.ShapeDtypeStruct((n, d), F32),
        compiler_params=_cparams("parallel"),
        name="combine_final",
    )(x, y1, y2, wts, g)


def _rot_cols(w, half):
    return jnp.concatenate([-w[..., half:], w[..., :half]], axis=-1)


def _route(ids, n_experts, tile):
    n2 = ids.shape[0]
    onehot = (ids[:, None] == jnp.arange(n_experts, dtype=jnp.int32)[None, :]).astype(jnp.int32)
    rank = jnp.take_along_axis(jnp.cumsum(onehot, axis=0) - onehot, ids[:, None], axis=1)[:, 0]
    counts = jnp.sum(onehot, axis=0)
    padded = (counts + tile - 1) // tile * tile
    ends = jnp.cumsum(padded)
    pos = (ends - padded)[ids] + rank
    n_tiles = -(-n2 // tile) + n_experts
    tile_start = jnp.arange(n_tiles, dtype=jnp.int32) * tile
    n_used = (ends[-1] // tile).astype(jnp.int32)
    te = jnp.sum(tile_start[:, None] >= ends[None, :], axis=1).astype(jnp.int32)
    te = jnp.minimum(te, te[jnp.maximum(n_used - 1, 0)])
    return pos, te, n_used.reshape(1), n_tiles * tile


def kernel(x_prompt, x_sample, state_C, state_n, state_m, cache_ckv, cache_krope, page_table, g_norm_a, w_in_a, b_gate_a, g_head_a, w_out_a, g_kv, w_dkv, g_ckv, w_uk, w_uv, g_norm_b, w_dq, g_q, w_uq, w_o_b, g_ffn_d, w_gate_d, w_up_d, w_down_d, g_ffn_m, w_router, w_gate_m, w_up_m, w_down_m, g_final):
    B, T, D = x_prompt.shape
    DB, TS, _ = x_sample.shape
    H, DV = g_head_a.shape[1:]
    DK = state_C.shape[3]
    kv_lora, n_bheads, nope_dim = w_uk.shape
    rope_dim = cache_krope.shape[2]
    page = cache_ckv.shape[1]
    past_len = page_table.shape[1] * page
    n_experts = w_router.shape[2]
    assert state_C.shape[0] == 1 and g_norm_b.shape[0] == 1 and g_ffn_d.shape[0] == 1
    assert TS <= SAMPLE_PAD and 2 * rope_dim == LANES and 2 * H <= LANES
    TP = SAMPLE_PAD
    n_p, n_s = B * T, DB * TP
    n = n_p + n_s
    assert n_p % TOKEN_TILE == 0 and n_s % TOKEN_TILE == 0 and T % ATTN_TILE == 0

    x = jnp.concatenate([x_prompt.reshape(n_p, D),
                         jnp.pad(x_sample, ((0, 0), (0, TP - TS), (0, 0))).reshape(n_s, D)], axis=0)

    w_in = w_in_a[0].astype(BF16)
    hq, hv = H * DK, H * DV
    wq, wk, wv, wo = (w_in[:, :hq], w_in[:, hq:2 * hq], w_in[:, 2 * hq:2 * hq + hv],
                      w_in[:, 2 * hq + hv:2 * hq + 2 * hv])
    wgate = jnp.pad(w_in[:, 2 * hq + 2 * hv:], ((0, 0), (0, LANES - 2 * H)))
    bgate = jnp.pad(b_gate_a[0], (0, LANES - 2 * H)).reshape(1, LANES)
    q, k, v, o, gates = _in_proj(x, g_norm_a, wq, wk, wv, wo, wgate, bgate,
                                 n_heads=H, q_scale=DK ** -0.5)
    hg, c_p, n_pr, m_p = _mlstm(q, k, v, o, gates, g_head_a[0], row0=0, B=B, T=T,
                                L=MLSTM_CHUNK, t_valid=MLSTM_CHUNK)
    hg, c_s, n_sm, m_s = _mlstm(q, k, v, o, gates, g_head_a[0], row0=n_p, B=DB, T=TP, L=TP,
                                t_valid=TS, state=(state_C[0], state_n[0], state_m[0]), hg_prev=hg)

    f = w_gate_d.shape[2]
    f_pad = -(-f // FFN_CHUNK) * FFN_CHUNK
    nch = f_pad // FFN_CHUNK
    col_chunks = lambda w: jnp.pad(w.astype(BF16), ((0, 0), (0, f_pad - f))).reshape(
        D, nch, FFN_CHUNK).transpose(1, 0, 2)
    wd_d = jnp.pad(w_down_d[0].astype(BF16), ((0, f_pad - f), (0, 0))).reshape(nch, FFN_CHUNK, D)
    x2 = _ffn(x, hg, w_out_a[0].astype(BF16), g_ffn_d, col_chunks(w_gate_d[0]),
              col_chunks(w_up_d[0]), wd_d)

    half = rope_dim // 2
    inv = ROPE_THETA ** (-jnp.arange(half, dtype=F32) / half)

    def tables(pos):
        ang = pos.astype(F32)[:, None] * inv[None, :]
        return (jnp.tile(jnp.cos(ang), (1, LANES // half)), jnp.tile(jnp.sin(ang), (1, LANES // half)))

    cos_p, sin_p = tables(jnp.arange(T, dtype=jnp.int32))
    cos_s, sin_s = tables(past_len + jnp.arange(TP, dtype=jnp.int32))
    reps = ATTN_TILE // TP
    cos_tab = jnp.concatenate([cos_p, jnp.tile(cos_s, (reps, 1))], axis=0)
    sin_tab = jnp.concatenate([sin_p, jnp.tile(sin_s, (reps, 1))], axis=0)

    w_kr = w_dkv[:, kv_lora:]
    wdkv = jnp.concatenate([w_dkv[:, :kv_lora], w_kr, _rot_cols(w_kr, half)], axis=1).astype(BF16)
    wuq = w_uq[0].reshape(-1, n_bheads, nope_dim + rope_dim)
    w_nope = wuq[:, :, :nope_dim].reshape(-1, n_bheads * nope_dim).astype(BF16)
    w_rope = wuq[:, :, nope_dim:]
    w_r = w_rope.reshape(-1, n_bheads * rope_dim).astype(BF16)
    w_rr = _rot_cols(w_rope, half).reshape(-1, n_bheads * rope_dim).astype(BF16)
    w_ukt = jnp.transpose(w_uk, (1, 2, 0)).astype(BF16)
    lat_weights = [g_kv.reshape(1, D), wdkv, g_ckv.reshape(1, kv_lora), g_norm_b,
                   w_dq[0].astype(BF16), g_q, w_nope, w_r, w_rr, w_ukt]
    ckv, krope, kcat, qcat = _latq(x2, cos_tab, sin_tab, n_p // ATTN_TILE, T // ATTN_TILE,
                                   lat_weights, n_heads=n_bheads, kv_lora=kv_lora,
                                   rope_dim=rope_dim, nope_dim=nope_dim)

    scale = (nope_dim + rope_dim) ** -0.5
    o_lat = _attn_prompt(qcat, kcat, n, B=B, T=T, kv_lora=kv_lora, scale=scale)

    kw = kv_lora + 2 * rope_dim
    q_s = qcat[n_p // ATTN_TILE:].reshape(-1, n_bheads, ATTN_TILE // TP, TP, kw)[:, :, :, :TS]
    q_s = q_s.transpose(0, 2, 1, 3, 4).reshape(DB, n_bheads * TS, kw)
    q_s = jnp.concatenate([q_s[..., :kv_lora],
                           q_s[..., kv_lora:kv_lora + rope_dim] + q_s[..., kv_lora + rope_dim:]], axis=-1)
    k_new = jnp.pad(kcat[n_p:].reshape(DB, TP, kw), ((0, 0), (0, LANES - TP), (0, 0)))
    o_s = _attn_sample(page_table, q_s, k_new, cache_ckv, cache_krope, t_valid=TS, scale=scale)
    o_s = o_s.reshape(DB, n_bheads, TS, kv_lora).transpose(0, 2, 1, 3)
    o_s = jnp.pad(o_s, ((0, 0), (0, TP - TS), (0, 0), (0, 0))).reshape(n_s, n_bheads * kv_lora)
    o_lat = lax.dynamic_update_slice(o_lat, o_s.astype(BF16), (n_p, 0))

    w_uvh = jnp.transpose(w_uv, (1, 0, 2)).astype(BF16)
    w_rt = jnp.pad(w_router[0], ((0, 0), (0, LANES - n_experts))).astype(BF16)
    x3, xn_m, ids, wts = _attn_out(o_lat, x2, w_uvh, w_o_b[0].astype(BF16), g_ffn_m, w_rt,
                                   n_experts=n_experts)

    top_k = 2
    pos, tile_expert, n_used, n_rows = _route(ids[:, :top_k].reshape(-1), n_experts, MOE_TILE)
    row_token = jnp.zeros((n_rows,), jnp.int32).at[pos].set(
        jnp.arange(n * top_k, dtype=jnp.int32) // top_k)
    ys = _moe(tile_expert, n_used, xn_m[row_token], w_gate_m[0].astype(BF16),
              w_up_m[0].astype(BF16), w_down_m[0].astype(BF16))
    pos = pos.reshape(n, top_k)
    y = _final(x3, ys[pos[:, 0]], ys[pos[:, 1]], wts, g_final.reshape(1, D))

    def split(a):
        w = a.shape[1]
        return a[:n_p].reshape(B, T, w), a[n_p:].reshape(DB, TP, w)[:, :TS]

    y_p, y_s = split(y)
    ckv_p, ckv_s = split(ckv)
    kr_p, kr_s = split(krope)
    return (y_p, y_s, c_p[None], n_pr[None], m_p[None], ckv_p, kr_p,
            c_s[None], n_sm[None], m_s[None], ckv_s, kr_s)
```

```python
import functools

import jax
import jax.numpy as jnp
from jax import lax
from jax.experimental import pallas as pl
from jax.experimental.pallas import tpu as pltpu

F32 = jnp.float32
BF16 = jnp.bfloat16

EPS = 1e-6
GATE_CAP = 15.0
ROPE_THETA = 10000.0
NEG = -1e30
LANES = 128
VMEM_LIMIT_BYTES = 56 * 2**20

SAMPLE_PAD = 8
TOKEN_TILE = 512
ATTN_TILE = 512
MLSTM_CHUNK = 256
FFN_CHUNK = 256
MOE_TILE = 512
MOE_CHUNK = 512
CAST_ROWS = 512
PAGES_PER_STEP = 64


def _cparams(*sem, flags=None):
    return pltpu.CompilerParams(dimension_semantics=sem, vmem_limit_bytes=VMEM_LIMIT_BYTES,
                                flags=flags)


def _dot(a, b):
    return jnp.dot(a, b, preferred_element_type=F32)


def _dot_nt(a, b):
    return lax.dot_general(a, b, (((1,), (1,)), ((), ())), preferred_element_type=F32)


def _rms(x, g):
    return x * lax.rsqrt(jnp.mean(x * x, axis=-1, keepdims=True) + EPS) * g


def _sigmoid(x):
    return 1.0 / (1.0 + jnp.exp(-x))


def _split3(x):
    hi = x.astype(BF16)
    r1 = x - hi.astype(F32)
    mid = r1.astype(BF16)
    lo = (r1 - mid.astype(F32)).astype(BF16)
    return hi, mid, lo


def _full(shape):
    nd = len(shape)
    return pl.BlockSpec(shape, lambda *_: (0,) * nd)


def _cast_kernel(x_ref, o_ref):
    o_ref[...] = x_ref[...].astype(o_ref.dtype)


def _to_bf16(w, rows):
    e, r, c = w.shape
    spec = pl.BlockSpec((1, rows, c), lambda i, j: (i, j, 0))
    return pl.pallas_call(
        _cast_kernel, grid=(e, r // rows), in_specs=[spec], out_specs=spec,
        out_shape=jax.ShapeDtypeStruct(w.shape, BF16),
        compiler_params=_cparams("parallel", "parallel"),
        name="cast_bf16",
    )(w)


def _in_proj_kernel(x_ref, g_ref, wq_ref, wk_ref, wv_ref, wo_ref, wg_ref, b_ref,
                    q_ref, k_ref, v_ref, o_ref, gate_ref, *, n_heads, q_scale):
    xn = _rms(x_ref[...], g_ref[...]).astype(BF16)
    q_ref[...] = _dot(xn, wq_ref[...]) * q_scale
    k_ref[...] = _dot(xn, wk_ref[...])
    v_ref[...] = _dot(xn, wv_ref[...])
    o_ref[...] = _dot(xn, wo_ref[...])
    g = _dot(xn, wg_ref[...]) + b_ref[...]
    g = GATE_CAP * jnp.tanh(g / GATE_CAP)
    logf = jnp.minimum(g, 0.0) - jnp.log1p(jnp.exp(-jnp.abs(g)))
    lane = lax.broadcasted_iota(jnp.int32, g.shape, 1)
    gate_ref[...] = jnp.where(lane < n_heads, g, jnp.where(lane < 2 * n_heads, logf, 0.0))


def _in_proj(x, g, wq, wk, wv, wo, wg, b, *, n_heads, q_scale):
    n, d = x.shape
    tm = TOKEN_TILE
    row = lambda w: pl.BlockSpec((tm, w), lambda i: (i, 0))
    outs = [(wq.shape[1], F32), (wk.shape[1], F32), (wv.shape[1], F32), (wo.shape[1], F32), (LANES, F32)]
    return pl.pallas_call(
        functools.partial(_in_proj_kernel, n_heads=n_heads, q_scale=q_scale),
        grid=(n // tm,),
        in_specs=[row(d), _full(g.shape), _full(wq.shape), _full(wk.shape), _full(wv.shape),
                  _full(wo.shape), _full(wg.shape), _full(b.shape)],
        out_specs=[row(w) for w, _ in outs],
        out_shape=[jax.ShapeDtypeStruct((n, w), dt) for w, dt in outs],
        compiler_params=_cparams("parallel"),
        name="in_proj",
    )(x, g, wq, wk, wv, wo, wg, b)


def _mlstm_kernel(*refs, L, H, DK, DV, t_valid, has_state, mm_dtype):
    if has_state:
        (q_ref, k_ref, v_ref, o_ref, gate_ref, gh_ref, c0_ref, n0_ref, m0_ref, _,
         hg_ref, cout_ref, nout_ref, mout_ref, caug_ref, m_scr) = refs
    else:
        (q_ref, k_ref, v_ref, o_ref, gate_ref, gh_ref,
         hg_ref, cout_ref, nout_ref, mout_ref, caug_ref, m_scr) = refs
    c = pl.program_id(1)
    last = pl.num_programs(1) - 1

    rk = lax.broadcasted_iota(jnp.int32, (DK, DK), 0)
    ck = lax.broadcasted_iota(jnp.int32, (DK, DK), 1)
    eye_k = rk == ck

    @pl.when(c == 0)
    def _():
        if has_state:
            for h in range(H):
                caug_ref[h, :, :DV] = c0_ref[0, h]
                n_row = n0_ref[0, h:h + 1, :]
                n_col = jnp.sum(jnp.where(eye_k, jnp.broadcast_to(n_row, (DK, DK)), 0.0),
                                axis=1, keepdims=True)
                caug_ref[h, :, DV:] = jnp.broadcast_to(n_col, (DK, DV))
                m_scr[h:h + 1, :] = jnp.broadcast_to(m0_ref[0, :, h:h + 1], (1, LANES))
        else:
            caug_ref[...] = jnp.zeros_like(caug_ref)
            m_scr[...] = jnp.zeros_like(m_scr)

    gates = gate_ref[...]
    if t_valid < L:
        t_id = lax.broadcasted_iota(jnp.int32, gates.shape, 0)
        lane = lax.broadcasted_iota(jnp.int32, gates.shape, 1)
        gates = jnp.where(t_id < t_valid, gates, jnp.where(lane < H, NEG, 0.0))

    ri = lax.broadcasted_iota(jnp.int32, (L, L), 0)
    ci = lax.broadcasted_iota(jnp.int32, (L, L), 1)
    causal = ci <= ri
    eye = ci == ri
    ones_v = jnp.ones((L, DV), mm_dtype)

    use_mxu_cumsum = L % LANES == 0
    if use_mxu_cumsum:
        gates_t = gates.T
        cum = sum(_dot(causal.astype(BF16), p) for p in _split3(gates))
        cum_t = sum(_dot(p, (ri <= ci).astype(BF16)) for p in _split3(gates_t))

    for h in range(H):
        if use_mxu_cumsum:
            a_col = cum[:, H + h:H + h + 1]
            b_row = gates_t[h:h + 1, :] - cum_t[H + h:H + h + 1, :]
        else:
            li_col = gates[:, h:h + 1]
            lf_col = gates[:, H + h:H + h + 1]
            lf_b = jnp.broadcast_to(lf_col, (L, L))
            lf_row = jnp.sum(jnp.where(eye, lf_b, 0.0), axis=0, keepdims=True)
            a_col = jnp.sum(jnp.where(causal, jnp.broadcast_to(lf_row, (L, L)), 0.0),
                            axis=1, keepdims=True)
            a_row = jnp.sum(jnp.where(ci >= ri, lf_b, 0.0), axis=0, keepdims=True)
            li_row = jnp.sum(jnp.where(eye, jnp.broadcast_to(li_col, (L, L)), 0.0),
                             axis=0, keepdims=True)
            b_row = li_row - a_row
        d = jnp.where(causal, a_col + b_row, NEG)
        m_prev = m_scr[h:h + 1, 0:1]
        inter = a_col + m_prev
        m_t = jnp.maximum(inter, jnp.max(d, axis=1, keepdims=True))

        qh = q_ref[:, h * DK:(h + 1) * DK].astype(mm_dtype)
        kh = k_ref[:, h * DK:(h + 1) * DK].astype(mm_dtype)
        vaug = jnp.concatenate([v_ref[:, h * DV:(h + 1) * DV].astype(mm_dtype), ones_v], axis=1)
        caug = caug_ref[h]

        w = jnp.exp(d - m_t) * _dot_nt(qh, kh)
        e_inter = jnp.exp(inter - m_t)
        num = _dot(w.astype(mm_dtype), vaug) + e_inter * _dot(qh, caug.astype(mm_dtype))
        hh = num[:, :DV] / jnp.maximum(jnp.abs(num[:, DV:]), jnp.exp(-m_t))
        hn = _rms(hh, gh_ref[h:h + 1, :])
        hg_ref[:, h * DV:(h + 1) * DV] = _sigmoid(o_ref[:, h * DV:(h + 1) * DV]) * hn

        m_new = m_t[L - 1:L, :]
        a_last = a_col[L - 1:L, :]
        e_end = jnp.exp(a_last + b_row - m_new)
        e_carry = jnp.exp(a_last + m_prev - m_new)
        k_t = _dot_nt(eye_k.astype(mm_dtype), kh)
        caug_new = e_carry * caug + _dot((k_t * e_end).astype(mm_dtype), vaug)
        caug_ref[h] = caug_new
        m_scr[h:h + 1, :] = jnp.broadcast_to(m_new, (1, LANES))

        @pl.when(c == last)
        def _():
            cout_ref[0, h] = caug_new[:, :DV]
            nout_ref[0, h:h + 1, :] = jnp.sum(jnp.where(eye_k, caug_new[:, DV:DV + DK], 0.0),
                                              axis=0, keepdims=True)
            mout_ref[0, :, h:h + 1] = m_new


def _mlstm(q, k, v, o, gates, g_head, *, row0, B, T, L, t_valid, state=None, hg_prev=None):
    n = q.shape[0]
    H, DV = g_head.shape
    DK = q.shape[1] // H
    nc = T // L
    blk0 = row0 // L
    row = lambda w: pl.BlockSpec((L, w), lambda b, c: (blk0 + b * nc + c, 0))
    in_specs = [row(H * DK), row(H * DK), row(H * DV), row(H * DV), row(LANES), _full(g_head.shape)]
    args = [q, k, v, o, gates, g_head]
    aliases = {}
    if state is not None:
        c0, n0, m0 = state
        in_specs += [pl.BlockSpec((1, H, DK, DV), lambda b, c: (b, 0, 0, 0)),
                     pl.BlockSpec((1, H, DK), lambda b, c: (b, 0, 0)),
                     pl.BlockSpec((1, 1, H), lambda b, c: (b, 0, 0)),
                     pl.BlockSpec(memory_space=pl.ANY)]
        args += [c0, n0, m0.reshape(B, 1, H), hg_prev]
        aliases = {len(args) - 1: 0}
    out_shape = [jax.ShapeDtypeStruct((n, H * DV), F32),
                 jax.ShapeDtypeStruct((B, H, DK, DV), F32),
                 jax.ShapeDtypeStruct((B, H, DK), F32),
                 jax.ShapeDtypeStruct((B, 1, H), F32)]
    out_specs = [row(H * DV),
                 pl.BlockSpec((1, H, DK, DV), lambda b, c: (b, 0, 0, 0)),
                 pl.BlockSpec((1, H, DK), lambda b, c: (b, 0, 0)),
                 pl.BlockSpec((1, 1, H), lambda b, c: (b, 0, 0))]
    kern = functools.partial(_mlstm_kernel, L=L, H=H, DK=DK, DV=DV, t_valid=t_valid,
                             has_state=state is not None,
                             mm_dtype=BF16 if L % 16 == 0 else F32)
    hg, c_out, n_out, m_out = pl.pallas_call(
        kern, grid=(B, nc), in_specs=in_specs, out_specs=out_specs, out_shape=out_shape,
        scratch_shapes=[pltpu.VMEM((H, DK, 2 * DV), F32), pltpu.VMEM((H, LANES), F32)],
        input_output_aliases=aliases,
        compiler_params=_cparams("parallel", "arbitrary"),
        name="mlstm_sample" if state is not None else "mlstm_prompt",
    )(*args)
    return hg, c_out, n_out, m_out.reshape(B, H)


def _ffn_kernel(x_ref, hg_ref, wout_ref, g_ref, wg_ref, wu_ref, wd_ref, out_ref, acc_ref):
    x1 = x_ref[...] + _dot(hg_ref[...].astype(BF16), wout_ref[...])
    xn = _rms(x1, g_ref[...]).astype(BF16)
    acc_ref[...] = jnp.zeros_like(acc_ref)

    def body(c, carry):
        gate = _dot(xn, wg_ref[c])
        up = _dot(xn, wu_ref[c])
        hmid = (gate * _sigmoid(gate) * up).astype(BF16)
        acc_ref[...] += _dot(hmid, wd_ref[c])
        return carry

    lax.fori_loop(0, wg_ref.shape[0], body, 0)
    out_ref[...] = x1 + acc_ref[...]


def _ffn(x, hg, wout, g, wg, wu, wd):
    n, d = x.shape
    tm = TOKEN_TILE
    row = pl.BlockSpec((tm, d), lambda i: (i, 0))
    return pl.pallas_call(
        _ffn_kernel, grid=(n // tm,),
        in_specs=[row, row, _full(wout.shape), _full(g.shape), _full(wg.shape), _full(wu.shape),
                  _full(wd.shape)],
        out_specs=row, out_shape=jax.ShapeDtypeStruct((n, d), F32),
        scratch_shapes=[pltpu.VMEM((tm, d), F32)],
        compiler_params=_cparams("parallel"),
        name="outproj_ffn",
    )(x, hg, wout, g, wg, wu, wd)


def _latq_kernel(x_ref, cos_ref, sin_ref, gkv_ref, wdkv_ref, gckv_ref, gnb_ref, wdq_ref, gq_ref,
                 wn_ref, wr_ref, wrr_ref, wuk_ref,
                 ckv_ref, kr_ref, kcat_ref, kvt_ref, q_ref, *, n_heads, kv_lora, rope_dim, nope_dim):
    x = x_ref[...]
    xs = x * lax.rsqrt(jnp.mean(x * x, axis=-1, keepdims=True) + EPS)
    cos = cos_ref[...]
    sin = sin_ref[...]
    lane = lax.broadcasted_iota(jnp.int32, cos.shape, 1)
    lo = lane < rope_dim

    lat = _dot((xs * gkv_ref[...]).astype(BF16), wdkv_ref[...])
    ckv = _rms(lat[:, :kv_lora], gckv_ref[...])
    ckv_ref[...] = ckv
    t = lat[:, kv_lora:] * jnp.where(lo, cos, sin)
    kr2 = t + pltpu.roll(t, rope_dim, axis=1)
    kr_ref[...] = kr2[:, :rope_dim]
    kcat_ref[...] = jnp.concatenate([ckv, kr2], axis=1).astype(BF16)
    kvt_ref[0] = ckv.T.astype(BF16)

    cq = _dot((xs * gnb_ref[...]).astype(BF16), wdq_ref[...])
    cqn = _rms(cq, gq_ref[...]).astype(BF16)
    qn = _dot(cqn, wn_ref[...]).astype(BF16)
    reps = n_heads * rope_dim // LANES
    cos_h = jnp.concatenate([cos] * reps, axis=1)
    sin_h = jnp.concatenate([sin] * reps, axis=1)
    qr = _dot(cqn, wr_ref[...]) * cos_h + _dot(cqn, wrr_ref[...]) * sin_h
    for h in range(n_heads):
        ql = _dot(qn[:, h * nope_dim:(h + 1) * nope_dim], wuk_ref[h])
        pair = qr[:, (h // 2) * LANES:(h // 2 + 1) * LANES]
        slot = jnp.where(lo if h % 2 == 0 else jnp.logical_not(lo), pair, 0.0)
        q_ref[0, h] = jnp.concatenate([ql, slot], axis=1).astype(BF16)


def _latq(x, cos_tab, sin_tab, n_prompt_blocks, blocks_per_seq, weights, *, n_heads, kv_lora,
          rope_dim, nope_dim):
    n, d = x.shape
    tm = ATTN_TILE
    kw = kv_lora + 2 * rope_dim

    def tab_map(i):
        return (jnp.where(i < n_prompt_blocks, i % blocks_per_seq, blocks_per_seq), 0)

    tab = pl.BlockSpec((tm, LANES), tab_map)
    row = lambda w: pl.BlockSpec((tm, w), lambda i: (i, 0))
    return pl.pallas_call(
        functools.partial(_latq_kernel, n_heads=n_heads, kv_lora=kv_lora, rope_dim=rope_dim,
                          nope_dim=nope_dim),
        grid=(n // tm,),
        in_specs=[row(d), tab, tab] + [_full(w.shape) for w in weights],
        out_specs=[row(kv_lora), row(rope_dim), row(kw),
                   pl.BlockSpec((1, kv_lora, tm), lambda i: (i, 0, 0)),
                   pl.BlockSpec((1, n_heads, tm, kw), lambda i: (i, 0, 0, 0))],
        out_shape=[jax.ShapeDtypeStruct((n, kv_lora), F32),
                   jax.ShapeDtypeStruct((n, rope_dim), F32),
                   jax.ShapeDtypeStruct((n, kw), BF16),
                   jax.ShapeDtypeStruct((n // tm, kv_lora, tm), BF16),
                   jax.ShapeDtypeStruct((n // tm, n_heads, tm, kw), BF16)],
        compiler_params=_cparams("parallel"),
        name="latent_q",
    )(x, cos_tab, sin_tab, *weights)


def _attn_prompt_kernel(q_ref, k_ref, kt_ref, o_ref, m_scr, l_scr, acc_scr, *, tq, kv_lora, scale):
    qi = pl.program_id(1)
    n_heads = q_ref.shape[1]
    m_scr[...] = jnp.full_like(m_scr, NEG)
    l_scr[...] = jnp.zeros_like(l_scr)
    acc_scr[...] = jnp.zeros_like(acc_scr)

    def step(j, masked):
        kc = k_ref[pl.ds(pl.multiple_of(j * tq, tq), tq), :]
        kt = kt_ref[j]
        if masked:
            key = lax.broadcasted_iota(jnp.int32, (tq, tq), 0)
            qry = lax.broadcasted_iota(jnp.int32, (tq, tq), 1)
            keep = key <= qry
        for h in range(n_heads):
            st = _dot_nt(kc, q_ref[0, h]) * scale
            if masked:
                st = jnp.where(keep, st, NEG)
            m_prev = m_scr[h]
            m_new = jnp.maximum(m_prev, jnp.max(st, axis=0, keepdims=True))
            p = jnp.exp(st - m_new)
            alpha = jnp.exp(m_prev - m_new)
            l_scr[h] = alpha * l_scr[h] + jnp.sum(p, axis=0, keepdims=True)
            acc_scr[h] = alpha * acc_scr[h] + _dot(kt, p.astype(BF16))
            m_scr[h] = m_new

    def body(j, carry):
        step(j, False)
        return carry

    lax.fori_loop(0, qi, body, 0)
    step(qi, True)
    for h in range(n_heads):
        o_ref[:, h * kv_lora:(h + 1) * kv_lora] = (acc_scr[h] / l_scr[h]).T.astype(BF16)


def _attn_prompt(q, kcat, kvt, n_rows, *, B, T, kv_lora, scale):
    tq = ATTN_TILE
    _, n_heads, _, kw = q.shape
    nq = T // tq
    return pl.pallas_call(
        functools.partial(_attn_prompt_kernel, tq=tq, kv_lora=kv_lora, scale=scale),
        grid=(B, nq),
        in_specs=[pl.BlockSpec((1, n_heads, tq, kw), lambda b, i: (b * nq + i, 0, 0, 0)),
                  pl.BlockSpec((T, kw), lambda b, i: (b, 0)),
                  pl.BlockSpec((nq, kv_lora, tq), lambda b, i: (b, 0, 0))],
        out_specs=pl.BlockSpec((tq, n_heads * kv_lora), lambda b, i: (b * nq + i, 0)),
        out_shape=jax.ShapeDtypeStruct((n_rows, n_heads * kv_lora), BF16),
        scratch_shapes=[pltpu.VMEM((n_heads, 1, tq), F32), pltpu.VMEM((n_heads, 1, tq), F32),
                        pltpu.VMEM((n_heads, kv_lora, tq), F32)],
        compiler_params=_cparams("parallel", "arbitrary"),
        name="attn_prompt",
    )(q, kcat, kvt)


def _attn_sample_kernel(pt_ref, q_ref, knew_ref, *refs, n_pages, t_valid, kv_lora, rope_dim, scale):
    ck_refs = refs[:n_pages]
    kr_refs = refs[n_pages:2 * n_pages]
    o_ref, m_scr, l_scr, acc_scr = refs[2 * n_pages:]
    g = pl.program_id(1)

    @pl.when(g == 0)
    def _():
        m_scr[...] = jnp.full_like(m_scr, NEG)
        l_scr[...] = jnp.zeros_like(l_scr)
        acc_scr[...] = jnp.zeros_like(acc_scr)

    q = q_ref[0]
    ql = q[:, :kv_lora]
    qr = q[:, kv_lora:]

    def update(s, values):
        m_prev = m_scr[...]
        m_new = jnp.maximum(m_prev, jnp.max(s, axis=1, keepdims=True))
        p = jnp.exp(s - m_new)
        alpha = jnp.exp(m_prev - m_new)
        l_scr[...] = alpha * l_scr[...] + jnp.sum(p, axis=1, keepdims=True)
        p = p.astype(BF16)
        pv = _dot(p[:, :values[0].shape[0]], values[0])
        for i in range(1, len(values)):
            rows = values[i].shape[0]
            pv += _dot(p[:, i * rows:(i + 1) * rows], values[i])
        acc_scr[...] = alpha * acc_scr[...] + pv
        m_scr[...] = m_new

    cks = [r[0].astype(BF16) for r in ck_refs]
    s = jnp.concatenate(
        [_dot_nt(ql, ck) + _dot(qr, kr[0].astype(BF16)) for ck, kr in zip(cks, kr_refs)],
        axis=1) * scale
    update(s, cks)

    @pl.when(g == pl.num_programs(1) - 1)
    def _():
        kn = knew_ref[0]
        ckn = kn[:, :kv_lora]
        sn = (_dot_nt(ql, ckn) + _dot_nt(qr, kn[:, kv_lora:kv_lora + rope_dim])) * scale
        t = lax.broadcasted_iota(jnp.int32, sn.shape, 0) % t_valid
        j = lax.broadcasted_iota(jnp.int32, sn.shape, 1)
        update(jnp.where(j <= t, sn, NEG), [ckn])
        o_ref[0] = acc_scr[...] / l_scr[...]


def _attn_sample(page_table, q, knew, cache_ckv, cache_krope_t, *, t_valid, scale):
    nb, n_pages = page_table.shape
    _, rows, qw = q.shape
    _, page, kv_lora = cache_ckv.shape
    rope_dim = cache_krope_t.shape[1]
    pg = min(PAGES_PER_STEP, n_pages)
    assert n_pages % pg == 0
    ck_specs = [pl.BlockSpec((1, page, kv_lora),
                             functools.partial(lambda b, g, pt, i: (pt[b, g * pg + i], 0, 0), i=i))
                for i in range(pg)]
    kr_specs = [pl.BlockSpec((1, rope_dim, page),
                             functools.partial(lambda b, g, pt, i: (pt[b, g * pg + i], 0, 0), i=i))
                for i in range(pg)]
    grid_spec = pltpu.PrefetchScalarGridSpec(
        num_scalar_prefetch=1, grid=(nb, n_pages // pg),
        in_specs=[pl.BlockSpec((1, rows, qw), lambda b, g, pt: (b, 0, 0)),
                  pl.BlockSpec((1,) + knew.shape[1:], lambda b, g, pt: (b, 0, 0))]
        + ck_specs + kr_specs,
        out_specs=pl.BlockSpec((1, rows, kv_lora), lambda b, g, pt: (b, 0, 0)),
        scratch_shapes=[pltpu.VMEM((rows, 1), F32), pltpu.VMEM((rows, 1), F32),
                        pltpu.VMEM((rows, kv_lora), F32)])
    return pl.pallas_call(
        functools.partial(_attn_sample_kernel, n_pages=pg, t_valid=t_valid, kv_lora=kv_lora,
                          rope_dim=rope_dim, scale=scale),
        grid_spec=grid_spec,
        out_shape=jax.ShapeDtypeStruct((nb, rows, kv_lora), F32),
        compiler_params=_cparams("parallel", "arbitrary"),
        name="attn_sample",
    )(page_table, q, knew, *([cache_ckv] * pg), *([cache_krope_t] * pg))


def _attn_out_kernel(ol_ref, x_ref, wuv_ref, wo_ref, g_ref, wr_ref,
                     x3_ref, xn_ref, ids_ref, wts_ref, *, n_experts):
    n_heads, kv_lora, _ = wuv_ref.shape
    o = jnp.concatenate(
        [_dot(ol_ref[:, h * kv_lora:(h + 1) * kv_lora], wuv_ref[h]) for h in range(n_heads)],
        axis=1).astype(BF16)
    x3 = x_ref[...] + _dot(o, wo_ref[...])
    x3_ref[...] = x3
    xn = _rms(x3, g_ref[...]).astype(BF16)
    xn_ref[...] = xn
    logits = _dot(xn, wr_ref[...])
    lane = lax.broadcasted_iota(jnp.int32, logits.shape, 1)
    lane_f = lane.astype(F32)
    lg = jnp.where(lane < n_experts, logits, -jnp.inf)
    v1 = jnp.max(lg, axis=1, keepdims=True)
    i1 = jnp.min(jnp.where(lg == v1, lane_f, float(LANES)), axis=1, keepdims=True)
    lg2 = jnp.where(lane_f == i1, -jnp.inf, lg)
    v2 = jnp.max(lg2, axis=1, keepdims=True)
    i2 = jnp.min(jnp.where(lg2 == v2, lane_f, float(LANES)), axis=1, keepdims=True)
    e = jnp.exp(v2 - v1)
    w1 = 1.0 / (1.0 + e)
    w2 = e / (1.0 + e)
    ids_ref[...] = jnp.where(lane == 0, i1, jnp.where(lane == 1, i2, 0.0)).astype(jnp.int32)
    wts_ref[...] = jnp.where(lane == 0, w1, jnp.where(lane == 1, w2, 0.0))


def _attn_out(o_lat, x, wuv, wo, g, wr, *, n_experts):
    n, d = x.shape
    tm = TOKEN_TILE
    row = lambda w: pl.BlockSpec((tm, w), lambda i: (i, 0))
    return pl.pallas_call(
        functools.partial(_attn_out_kernel, n_experts=n_experts),
        grid=(n // tm,),
        in_specs=[row(o_lat.shape[1]), row(d), _full(wuv.shape), _full(wo.shape), _full(g.shape),
                  _full(wr.shape)],
        out_specs=[row(d), row(d), row(LANES), row(LANES)],
        out_shape=[jax.ShapeDtypeStruct((n, d), F32), jax.ShapeDtypeStruct((n, d), BF16),
                   jax.ShapeDtypeStruct((n, LANES), jnp.int32),
                   jax.ShapeDtypeStruct((n, LANES), F32)],
        compiler_params=_cparams("parallel"),
        name="attn_out_router",
    )(o_lat, x, wuv, wo, g, wr)


def _moe_kernel(te_ref, nu_ref, xs_ref, wg_ref, wu_ref, wd_ref, out_ref, *, chunk):
    t = pl.program_id(0)

    @pl.when(t < nu_ref[0])
    def _():
        xs = xs_ref[...]
        acc = jnp.zeros(out_ref.shape, F32)
        for c in range(wg_ref.shape[2] // chunk):
            sl = slice(c * chunk, (c + 1) * chunk)
            gate = _dot(xs, wg_ref[0, :, sl])
            up = _dot(xs, wu_ref[0, :, sl])
            acc += _dot((gate * _sigmoid(gate) * up).astype(BF16), wd_ref[0, sl, :])
        out_ref[...] = acc

    @pl.when(t >= nu_ref[0])
    def _():
        out_ref[...] = jnp.zeros_like(out_ref)


def _moe(tile_expert, n_used, xs, wg, wu, wd):
    rows, d = xs.shape
    tm = MOE_TILE
    f = wg.shape[2]
    one = pl.Buffered(1)
    grid_spec = pltpu.PrefetchScalarGridSpec(
        num_scalar_prefetch=2, grid=(rows // tm,),
        in_specs=[pl.BlockSpec((tm, d), lambda t, te, nu: (t, 0)),
                  pl.BlockSpec((1, d, f), lambda t, te, nu: (te[t], 0, 0), pipeline_mode=one),
                  pl.BlockSpec((1, d, f), lambda t, te, nu: (te[t], 0, 0), pipeline_mode=one),
                  pl.BlockSpec((1, f, d), lambda t, te, nu: (te[t], 0, 0), pipeline_mode=one)],
        out_specs=pl.BlockSpec((tm, d), lambda t, te, nu: (t, 0)))
    return pl.pallas_call(
        functools.partial(_moe_kernel, chunk=MOE_CHUNK),
        grid_spec=grid_spec,
        out_shape=jax.ShapeDtypeStruct((rows, d), F32),
        compiler_params=_cparams("arbitrary"),
        name="moe_experts",
    )(tile_expert, n_used, xs, wg, wu, wd)


def _final_kernel(x_ref, y1_ref, y2_ref, wts_ref, g_ref, out_ref):
    w = wts_ref[...]
    x4 = x_ref[...] + (w[:, 0:1] * y1_ref[...] + w[:, 1:2] * y2_ref[...])
    out_ref[...] = _rms(x4, g_ref[...])


def _final(x, y1, y2, wts, g):
    n, d = x.shape
    tm = TOKEN_TILE
    row = lambda w: pl.BlockSpec((tm, w), lambda i: (i, 0))
    return pl.pallas_call(
        _final_kernel, grid=(n // tm,),
        in_specs=[row(d), row(d), row(d), row(LANES), _full(g.shape)],
        out_specs=row(d), out_shape=jax.ShapeDtypeStruct((n, d), F32),
        compiler_params=_cparams("parallel"),
        name="combine_final",
    )(x, y1, y2, wts, g)


def _rot_cols(w, half):
    return jnp.concatenate([-w[..., half:], w[..., :half]], axis=-1)


def _route(ids, n_experts, tile):
    n2 = ids.shape[0]
    onehot = (ids[:, None] == jnp.arange(n_experts, dtype=jnp.int32)[None, :]).astype(jnp.int32)
    rank = jnp.take_along_axis(jnp.cumsum(onehot, axis=0) - onehot, ids[:, None], axis=1)[:, 0]
    counts = jnp.sum(onehot, axis=0)
    padded = (counts + tile - 1) // tile * tile
    ends = jnp.cumsum(padded)
    pos = (ends - padded)[ids] + rank
    n_tiles = -(-n2 // tile) + n_experts
    tile_start = jnp.arange(n_tiles, dtype=jnp.int32) * tile
    n_used = (ends[-1] // tile).astype(jnp.int32)
    te = jnp.sum(tile_start[:, None] >= ends[None, :], axis=1).astype(jnp.int32)
    te = jnp.minimum(te, te[jnp.maximum(n_used - 1, 0)])
    return pos, te, n_used.reshape(1), n_tiles * tile


def kernel(x_prompt, x_sample, state_C, state_n, state_m, cache_ckv, cache_krope, page_table, g_norm_a, w_in_a, b_gate_a, g_head_a, w_out_a, g_kv, w_dkv, g_ckv, w_uk, w_uv, g_norm_b, w_dq, g_q, w_uq, w_o_b, g_ffn_d, w_gate_d, w_up_d, w_down_d, g_ffn_m, w_router, w_gate_m, w_up_m, w_down_m, g_final):
    B, T, D = x_prompt.shape
    DB, TS, _ = x_sample.shape
    H, DV = g_head_a.shape[1:]
    DK = state_C.shape[3]
    kv_lora, n_bheads, nope_dim = w_uk.shape
    rope_dim = cache_krope.shape[2]
    page = cache_ckv.shape[1]
    past_len = page_table.shape[1] * page
    n_experts = w_router.shape[2]
    assert state_C.shape[0] == 1 and g_norm_b.shape[0] == 1 and g_ffn_d.shape[0] == 1
    assert TS <= SAMPLE_PAD and 2 * rope_dim == LANES and 2 * H <= LANES
    TP = SAMPLE_PAD
    n_p, n_s = B * T, DB * TP
    n = n_p + n_s
    assert n_p % TOKEN_TILE == 0 and n_s % TOKEN_TILE == 0 and T % ATTN_TILE == 0

    x = jnp.concatenate([x_prompt.reshape(n_p, D),
                         jnp.pad(x_sample, ((0, 0), (0, TP - TS), (0, 0))).reshape(n_s, D)], axis=0)

    w_in = w_in_a[0].astype(BF16)
    hq, hv = H * DK, H * DV
    wq, wk, wv, wo = (w_in[:, :hq], w_in[:, hq:2 * hq], w_in[:, 2 * hq:2 * hq + hv],
                      w_in[:, 2 * hq + hv:2 * hq + 2 * hv])
    wgate = jnp.pad(w_in[:, 2 * hq + 2 * hv:], ((0, 0), (0, LANES - 2 * H)))
    bgate = jnp.pad(b_gate_a[0], (0, LANES - 2 * H)).reshape(1, LANES)
    q, k, v, o, gates = _in_proj(x, g_norm_a, wq, wk, wv, wo, wgate, bgate,
                                 n_heads=H, q_scale=DK ** -0.5)
    hg, c_p, n_pr, m_p = _mlstm(q, k, v, o, gates, g_head_a[0], row0=0, B=B, T=T,
                                L=MLSTM_CHUNK, t_valid=MLSTM_CHUNK)
    hg, c_s, n_sm, m_s = _mlstm(q, k, v, o, gates, g_head_a[0], row0=n_p, B=DB, T=TP, L=TP,
                                t_valid=TS, state=(state_C[0], state_n[0], state_m[0]), hg_prev=hg)

    f = w_gate_d.shape[2]
    f_pad = -(-f // FFN_CHUNK) * FFN_CHUNK
    nch = f_pad // FFN_CHUNK
    col_chunks = lambda w: jnp.pad(w.astype(BF16), ((0, 0), (0, f_pad - f))).reshape(
        D, nch, FFN_CHUNK).transpose(1, 0, 2)
    wd_d = jnp.pad(w_down_d[0].astype(BF16), ((0, f_pad - f), (0, 0))).reshape(nch, FFN_CHUNK, D)
    x2 = _ffn(x, hg, w_out_a[0].astype(BF16), g_ffn_d, col_chunks(w_gate_d[0]),
              col_chunks(w_up_d[0]), wd_d)

    half = rope_dim // 2
    inv = ROPE_THETA ** (-jnp.arange(half, dtype=F32) / half)

    def tables(pos):
        ang = pos.astype(F32)[:, None] * inv[None, :]
        return (jnp.tile(jnp.cos(ang), (1, LANES // half)), jnp.tile(jnp.sin(ang), (1, LANES // half)))

    cos_p, sin_p = tables(jnp.arange(T, dtype=jnp.int32))
    cos_s, sin_s = tables(past_len + jnp.arange(TP, dtype=jnp.int32))
    reps = ATTN_TILE // TP
    cos_tab = jnp.concatenate([cos_p, jnp.tile(cos_s, (reps, 1))], axis=0)
    sin_tab = jnp.concatenate([sin_p, jnp.tile(sin_s, (reps, 1))], axis=0)

    w_kr = w_dkv[:, kv_lora:]
    wdkv = jnp.concatenate([w_dkv[:, :kv_lora], w_kr, _rot_cols(w_kr, half)], axis=1).astype(BF16)
    wuq = w_uq[0].reshape(-1, n_bheads, nope_dim + rope_dim)
    w_nope = wuq[:, :, :nope_dim].reshape(-1, n_bheads * nope_dim).astype(BF16)
    w_rope = wuq[:, :, nope_dim:]
    w_r = w_rope.reshape(-1, n_bheads * rope_dim).astype(BF16)
    w_rr = _rot_cols(w_rope, half).reshape(-1, n_bheads * rope_dim).astype(BF16)
    w_ukt = jnp.transpose(w_uk, (1, 2, 0)).astype(BF16)
    lat_weights = [g_kv.reshape(1, D), wdkv, g_ckv.reshape(1, kv_lora), g_norm_b,
                   w_dq[0].astype(BF16), g_q, w_nope, w_r, w_rr, w_ukt]
    ckv, krope, kcat, kvt, qcat = _latq(x2, cos_tab, sin_tab, n_p // ATTN_TILE, T // ATTN_TILE,
                                   lat_weights, n_heads=n_bheads, kv_lora=kv_lora,
                                   rope_dim=rope_dim, nope_dim=nope_dim)

    scale = (nope_dim + rope_dim) ** -0.5
    o_lat = _attn_prompt(qcat, kcat, kvt, n, B=B, T=T, kv_lora=kv_lora, scale=scale)

    kw = kv_lora + 2 * rope_dim
    q_s = qcat[n_p // ATTN_TILE:].reshape(-1, n_bheads, ATTN_TILE // TP, TP, kw)[:, :, :, :TS]
    q_s = q_s.transpose(0, 2, 1, 3, 4).reshape(DB, n_bheads * TS, kw)
    q_s = jnp.concatenate([q_s[..., :kv_lora],
                           q_s[..., kv_lora:kv_lora + rope_dim] + q_s[..., kv_lora + rope_dim:]], axis=-1)
    k_new = jnp.pad(kcat[n_p:].reshape(DB, TP, kw), ((0, 0), (0, LANES - TP), (0, 0)))
    o_s = _attn_sample(page_table, q_s, k_new, cache_ckv, jnp.swapaxes(cache_krope, 1, 2),
                       t_valid=TS, scale=scale)
    o_s = o_s.reshape(DB, n_bheads, TS, kv_lora).transpose(0, 2, 1, 3)
    o_s = jnp.pad(o_s, ((0, 0), (0, TP - TS), (0, 0), (0, 0))).reshape(n_s, n_bheads * kv_lora)
    o_lat = lax.dynamic_update_slice(o_lat, o_s.astype(BF16), (n_p, 0))

    w_uvh = jnp.transpose(w_uv, (1, 0, 2)).astype(BF16)
    w_rt = jnp.pad(w_router[0], ((0, 0), (0, LANES - n_experts))).astype(BF16)
    x3, xn_m, ids, wts = _attn_out(o_lat, x2, w_uvh, w_o_b[0].astype(BF16), g_ffn_m, w_rt,
                                   n_experts=n_experts)

    top_k = 2
    pos, tile_expert, n_used, n_rows = _route(ids[:, :top_k].reshape(-1), n_experts, MOE_TILE)
    row_token = jnp.zeros((n_rows,), jnp.int32).at[pos].set(
        jnp.arange(n * top_k, dtype=jnp.int32) // top_k)
    ys = _moe(tile_expert, n_used, xn_m[row_token], _to_bf16(w_gate_m[0], CAST_ROWS),
              _to_bf16(w_up_m[0], CAST_ROWS), _to_bf16(w_down_m[0], CAST_ROWS))
    pos = pos.reshape(n, top_k)
    y = _final(x3, ys[pos[:, 0]], ys[pos[:, 1]], wts, g_final.reshape(1, D))

    def split(a):
        w = a.shape[1]
        return a[:n_p].reshape(B, T, w), a[n_p:].reshape(DB, TP, w)[:, :TS]

    y_p, y_s = split(y)
    ckv_p, ckv_s = split(ckv)
    kr_p, kr_s = split(krope)
    return (y_p, y_s, c_p[None], n_pr[None], m_p[None], ckv_p, kr_p,
            c_s[None], n_sm[None], m_s[None], ckv_s, kr_s)
```

```python
import functools

import jax
import jax.numpy as jnp
from jax import lax
from jax.experimental import pallas as pl
from jax.experimental.pallas import tpu as pltpu

F32 = jnp.float32
BF16 = jnp.bfloat16

EPS = 1e-6
GATE_CAP = 15.0
ROPE_THETA = 10000.0
NEG = -1e30
LANES = 128
VMEM_LIMIT_BYTES = 56 * 2**20

SAMPLE_PAD = 8
TOKEN_TILE = 512
ATTN_TILE = 512
MLSTM_CHUNK = 256
FFN_CHUNK = 1408
MOE_TILE = 512
MOE_CHUNK = 512
CAST_BLOCK = 512
PAGES_PER_STEP = 64


def _cparams(*sem, flags=None):
    return pltpu.CompilerParams(dimension_semantics=sem, vmem_limit_bytes=VMEM_LIMIT_BYTES,
                                flags=flags)


def _dot(a, b):
    return jnp.dot(a, b, preferred_element_type=F32)


def _dot_nt(a, b):
    return lax.dot_general(a, b, (((1,), (1,)), ((), ())), preferred_element_type=F32)


def _rms(x, g):
    return x * lax.rsqrt(jnp.mean(x * x, axis=-1, keepdims=True) + EPS) * g


def _sigmoid(x):
    return 1.0 / (1.0 + jnp.exp(-x))


def _split3(x):
    hi = x.astype(BF16)
    r1 = x - hi.astype(F32)
    mid = r1.astype(BF16)
    lo = (r1 - mid.astype(F32)).astype(BF16)
    return hi, mid, lo


def _split_specs(tm, w, npb):
    return [pl.BlockSpec((tm, w), lambda i: (jnp.minimum(i, npb - 1), 0)),
            pl.BlockSpec((tm, w), lambda i: (jnp.maximum(i - npb, 0), 0))]


def _load_split(npb, p_ref, s_ref):
    return jnp.where(pl.program_id(0) < npb, p_ref[...], s_ref[...])


def _store_split(npb, p_ref, s_ref, val):
    i = pl.program_id(0)

    @pl.when(i < npb)
    def _():
        p_ref[...] = val

    @pl.when(i >= npb)
    def _():
        s_ref[...] = val


def _full(shape):
    nd = len(shape)
    return pl.BlockSpec(shape, lambda *_: (0,) * nd)


def _cast_kernel(*refs):
    k = len(refs) // 2
    for x_ref, o_ref in zip(refs[:k], refs[k:]):
        o_ref[...] = x_ref[...].astype(o_ref.dtype)


def _to_bf16_chunks(w, chunk, axis):
    e, r, c = w.shape
    k = w.shape[axis] // chunk
    blk = CAST_BLOCK
    if axis == 2:
        grid = (e, r // blk)
        in_specs = [pl.BlockSpec((1, blk, chunk), functools.partial(lambda i, j, q: (i, j, q), q=q))
                    for q in range(k)]
        out_spec = pl.BlockSpec((1, blk, chunk), lambda i, j: (i, j, 0))
        out_shape = jax.ShapeDtypeStruct((e, r, chunk), BF16)
    else:
        grid = (e, c // blk)
        in_specs = [pl.BlockSpec((1, chunk, blk), functools.partial(lambda i, j, q: (i, q, j), q=q))
                    for q in range(k)]
        out_spec = pl.BlockSpec((1, chunk, blk), lambda i, j: (i, 0, j))
        out_shape = jax.ShapeDtypeStruct((e, chunk, c), BF16)
    return pl.pallas_call(
        _cast_kernel, grid=grid, in_specs=in_specs, out_specs=[out_spec] * k,
        out_shape=[out_shape] * k,
        compiler_params=_cparams("parallel", "parallel"),
        name="cast_bf16",
    )(*([w] * k))


def _in_proj_kernel(xp_ref, xs_ref, g_ref, wq_ref, wk_ref, wv_ref, wo_ref, wg_ref, b_ref,
                    q_ref, k_ref, v_ref, o_ref, gate_ref, *, n_heads, q_scale, npb):
    xn = _rms(_load_split(npb, xp_ref, xs_ref), g_ref[...]).astype(BF16)
    q_ref[...] = _dot(xn, wq_ref[...]) * q_scale
    k_ref[...] = _dot(xn, wk_ref[...])
    v_ref[...] = _dot(xn, wv_ref[...])
    o_ref[...] = _dot(xn, wo_ref[...])
    g = _dot(xn, wg_ref[...]) + b_ref[...]
    g = GATE_CAP * jnp.tanh(g / GATE_CAP)
    logf = jnp.minimum(g, 0.0) - jnp.log1p(jnp.exp(-jnp.abs(g)))
    lane = lax.broadcasted_iota(jnp.int32, g.shape, 1)
    gate_ref[...] = jnp.where(lane < n_heads, g, jnp.where(lane < 2 * n_heads, logf, 0.0))


def _in_proj(xp, xs, g, wq, wk, wv, wo, wg, b, *, n_heads, q_scale):
    d = xp.shape[1]
    n = xp.shape[0] + xs.shape[0]
    tm = TOKEN_TILE
    npb = xp.shape[0] // tm
    row = lambda w: pl.BlockSpec((tm, w), lambda i: (i, 0))
    outs = [(wq.shape[1], F32), (wk.shape[1], F32), (wv.shape[1], F32), (wo.shape[1], F32), (LANES, F32)]
    return pl.pallas_call(
        functools.partial(_in_proj_kernel, n_heads=n_heads, q_scale=q_scale, npb=npb),
        grid=(n // tm,),
        in_specs=_split_specs(tm, d, npb) + [
            _full(g.shape), _full(wq.shape), _full(wk.shape), _full(wv.shape),
            _full(wo.shape), _full(wg.shape), _full(b.shape)],
        out_specs=[row(w) for w, _ in outs],
        out_shape=[jax.ShapeDtypeStruct((n, w), dt) for w, dt in outs],
        compiler_params=_cparams("parallel"),
        name="in_proj",
    )(xp, xs, g, wq, wk, wv, wo, wg, b)


def _mlstm_kernel(*refs, L, H, DK, DV, t_valid, has_state, mm_dtype):
    if has_state:
        (q_ref, k_ref, v_ref, o_ref, gate_ref, gh_ref, c0_ref, n0_ref, m0_ref, _,
         hg_ref, cout_ref, nout_ref, mout_ref, caug_ref, m_scr) = refs
    else:
        (q_ref, k_ref, v_ref, o_ref, gate_ref, gh_ref,
         hg_ref, cout_ref, nout_ref, mout_ref, caug_ref, m_scr) = refs
    c = pl.program_id(1)
    last = pl.num_programs(1) - 1

    rk = lax.broadcasted_iota(jnp.int32, (DK, DK), 0)
    ck = lax.broadcasted_iota(jnp.int32, (DK, DK), 1)
    eye_k = rk == ck

    @pl.when(c == 0)
    def _():
        if has_state:
            for h in range(H):
                caug_ref[h, :, :DV] = c0_ref[0, h]
                n_row = n0_ref[0, h:h + 1, :]
                n_col = jnp.sum(jnp.where(eye_k, jnp.broadcast_to(n_row, (DK, DK)), 0.0),
                                axis=1, keepdims=True)
                caug_ref[h, :, DV:] = jnp.broadcast_to(n_col, (DK, DV))
                m_scr[h:h + 1, :] = jnp.broadcast_to(m0_ref[0, :, h:h + 1], (1, LANES))
        else:
            caug_ref[...] = jnp.zeros_like(caug_ref)
            m_scr[...] = jnp.zeros_like(m_scr)

    gates = gate_ref[...]
    if t_valid < L:
        t_id = lax.broadcasted_iota(jnp.int32, gates.shape, 0)
        lane = lax.broadcasted_iota(jnp.int32, gates.shape, 1)
        gates = jnp.where(t_id < t_valid, gates, jnp.where(lane < H, NEG, 0.0))

    ri = lax.broadcasted_iota(jnp.int32, (L, L), 0)
    ci = lax.broadcasted_iota(jnp.int32, (L, L), 1)
    causal = ci <= ri
    eye = ci == ri
    ones_v = jnp.ones((L, DV), mm_dtype)

    use_mxu_cumsum = L % LANES == 0
    if use_mxu_cumsum:
        gates_t = gates.T
        cum = sum(_dot(causal.astype(BF16), p) for p in _split3(gates))
        cum_t = sum(_dot(p, (ri <= ci).astype(BF16)) for p in _split3(gates_t))

    for h in range(H):
        if use_mxu_cumsum:
            a_col = cum[:, H + h:H + h + 1]
            b_row = gates_t[h:h + 1, :] - cum_t[H + h:H + h + 1, :]
        else:
            li_col = gates[:, h:h + 1]
            lf_col = gates[:, H + h:H + h + 1]
            lf_b = jnp.broadcast_to(lf_col, (L, L))
            lf_row = jnp.sum(jnp.where(eye, lf_b, 0.0), axis=0, keepdims=True)
            a_col = jnp.sum(jnp.where(causal, jnp.broadcast_to(lf_row, (L, L)), 0.0),
                            axis=1, keepdims=True)
            a_row = jnp.sum(jnp.where(ci >= ri, lf_b, 0.0), axis=0, keepdims=True)
            li_row = jnp.sum(jnp.where(eye, jnp.broadcast_to(li_col, (L, L)), 0.0),
                             axis=0, keepdims=True)
            b_row = li_row - a_row
        d = jnp.where(causal, a_col + b_row, NEG)
        m_prev = m_scr[h:h + 1, 0:1]
        inter = a_col + m_prev
        m_t = jnp.maximum(inter, jnp.max(d, axis=1, keepdims=True))

        qh = q_ref[:, h * DK:(h + 1) * DK].astype(mm_dtype)
        kh = k_ref[:, h * DK:(h + 1) * DK].astype(mm_dtype)
        vaug = jnp.concatenate([v_ref[:, h * DV:(h + 1) * DV].astype(mm_dtype), ones_v], axis=1)
        caug = caug_ref[h]

        w = jnp.exp(d - m_t) * _dot_nt(qh, kh)
        e_inter = jnp.exp(inter - m_t)
        num = _dot(w.astype(mm_dtype), vaug) + e_inter * _dot(qh, caug.astype(mm_dtype))
        hh = num[:, :DV] / jnp.maximum(jnp.abs(num[:, DV:]), jnp.exp(-m_t))
        hn = _rms(hh, gh_ref[h:h + 1, :])
        hg_ref[:, h * DV:(h + 1) * DV] = _sigmoid(o_ref[:, h * DV:(h + 1) * DV]) * hn

        m_new = m_t[L - 1:L, :]
        a_last = a_col[L - 1:L, :]
        e_end = jnp.exp(a_last + b_row - m_new)
        e_carry = jnp.exp(a_last + m_prev - m_new)
        k_t = _dot_nt(eye_k.astype(mm_dtype), kh)
        caug_new = e_carry * caug + _dot((k_t * e_end).astype(mm_dtype), vaug)
        caug_ref[h] = caug_new
        m_scr[h:h + 1, :] = jnp.broadcast_to(m_new, (1, LANES))

        @pl.when(c == last)
        def _():
            cout_ref[0, h] = caug_new[:, :DV]
            nout_ref[0, h:h + 1, :] = jnp.sum(jnp.where(eye_k, caug_new[:, DV:DV + DK], 0.0),
                                              axis=0, keepdims=True)
            mout_ref[0, :, h:h + 1] = m_new


def _mlstm(q, k, v, o, gates, g_head, *, row0, B, T, L, t_valid, state=None, hg_prev=None):
    n = q.shape[0]
    H, DV = g_head.shape
    DK = q.shape[1] // H
    nc = T // L
    blk0 = row0 // L
    row = lambda w: pl.BlockSpec((L, w), lambda b, c: (blk0 + b * nc + c, 0))
    in_specs = [row(H * DK), row(H * DK), row(H * DV), row(H * DV), row(LANES), _full(g_head.shape)]
    args = [q, k, v, o, gates, g_head]
    aliases = {}
    if state is not None:
        c0, n0, m0 = state
        in_specs += [pl.BlockSpec((1, H, DK, DV), lambda b, c: (b, 0, 0, 0)),
                     pl.BlockSpec((1, H, DK), lambda b, c: (b, 0, 0)),
                     pl.BlockSpec((1, 1, H), lambda b, c: (b, 0, 0)),
                     pl.BlockSpec(memory_space=pl.ANY)]
        args += [c0, n0, m0.reshape(B, 1, H), hg_prev]
        aliases = {len(args) - 1: 0}
    out_shape = [jax.ShapeDtypeStruct((n, H * DV), F32),
                 jax.ShapeDtypeStruct((B, H, DK, DV), F32),
                 jax.ShapeDtypeStruct((B, H, DK), F32),
                 jax.ShapeDtypeStruct((B, 1, H), F32)]
    out_specs = [row(H * DV),
                 pl.BlockSpec((1, H, DK, DV), lambda b, c: (b, 0, 0, 0)),
                 pl.BlockSpec((1, H, DK), lambda b, c: (b, 0, 0)),
                 pl.BlockSpec((1, 1, H), lambda b, c: (b, 0, 0))]
    kern = functools.partial(_mlstm_kernel, L=L, H=H, DK=DK, DV=DV, t_valid=t_valid,
                             has_state=state is not None,
                             mm_dtype=BF16 if L % 16 == 0 else F32)
    hg, c_out, n_out, m_out = pl.pallas_call(
        kern, grid=(B, nc), in_specs=in_specs, out_specs=out_specs, out_shape=out_shape,
        scratch_shapes=[pltpu.VMEM((H, DK, 2 * DV), F32), pltpu.VMEM((H, LANES), F32)],
        input_output_aliases=aliases,
        compiler_params=_cparams("parallel", "arbitrary"),
        name="mlstm_sample" if state is not None else "mlstm_prompt",
    )(*args)
    return hg, c_out, n_out, m_out.reshape(B, H)


def _ffn_kernel(xp_ref, xs_ref, hg_ref, wout_ref, g_ref, wg_ref, wu_ref, wd_ref, out_ref, acc_ref,
                *, npb):
    x1 = _load_split(npb, xp_ref, xs_ref) + _dot(hg_ref[...].astype(BF16), wout_ref[...])
    xn = _rms(x1, g_ref[...]).astype(BF16)
    acc_ref[...] = jnp.zeros_like(acc_ref)

    def body(c, carry):
        gate = _dot(xn, wg_ref[c])
        up = _dot(xn, wu_ref[c])
        hmid = (gate * _sigmoid(gate) * up).astype(BF16)
        acc_ref[...] += _dot(hmid, wd_ref[c])
        return carry

    lax.fori_loop(0, wg_ref.shape[0], body, 0)
    out_ref[...] = x1 + acc_ref[...]


def _ffn(xp, xs, hg, wout, g, wg, wu, wd):
    n, d = hg.shape
    tm = TOKEN_TILE
    npb = xp.shape[0] // tm
    row = pl.BlockSpec((tm, d), lambda i: (i, 0))
    return pl.pallas_call(
        functools.partial(_ffn_kernel, npb=npb), grid=(n // tm,),
        in_specs=_split_specs(tm, d, npb) + [
            row, _full(wout.shape), _full(g.shape), _full(wg.shape), _full(wu.shape),
            _full(wd.shape)],
        out_specs=row, out_shape=jax.ShapeDtypeStruct((n, d), F32),
        scratch_shapes=[pltpu.VMEM((tm, d), F32)],
        compiler_params=_cparams("parallel"),
        name="outproj_ffn",
    )(xp, xs, hg, wout, g, wg, wu, wd)


def _latq_kernel(x_ref, cos_ref, sin_ref, gkv_ref, wdkv_ref, gckv_ref, gnb_ref, wdq_ref, gq_ref,
                 wn_ref, wr_ref, wrr_ref, wuk_ref,
                 ckvp_ref, ckvs_ref, krp_ref, krs_ref, kcat_ref, kvt_ref, q_ref,
                 *, n_heads, kv_lora, rope_dim, nope_dim, npb):
    x = x_ref[...]
    xs = x * lax.rsqrt(jnp.mean(x * x, axis=-1, keepdims=True) + EPS)
    cos = cos_ref[...]
    sin = sin_ref[...]
    lane = lax.broadcasted_iota(jnp.int32, cos.shape, 1)
    lo = lane < rope_dim

    lat = _dot((xs * gkv_ref[...]).astype(BF16), wdkv_ref[...])
    ckv = _rms(lat[:, :kv_lora], gckv_ref[...])
    _store_split(npb, ckvp_ref, ckvs_ref, ckv)
    t = lat[:, kv_lora:] * jnp.where(lo, cos, sin)
    kr2 = t + pltpu.roll(t, rope_dim, axis=1)
    _store_split(npb, krp_ref, krs_ref, kr2[:, :rope_dim])
    kcat_ref[...] = jnp.concatenate([ckv, kr2], axis=1).astype(BF16)
    kvt_ref[0] = ckv.T.astype(BF16)

    cq = _dot((xs * gnb_ref[...]).astype(BF16), wdq_ref[...])
    cqn = _rms(cq, gq_ref[...]).astype(BF16)
    qn = _dot(cqn, wn_ref[...]).astype(BF16)
    reps = n_heads * rope_dim // LANES
    cos_h = jnp.concatenate([cos] * reps, axis=1)
    sin_h = jnp.concatenate([sin] * reps, axis=1)
    qr = _dot(cqn, wr_ref[...]) * cos_h + _dot(cqn, wrr_ref[...]) * sin_h
    for h in range(n_heads):
        ql = _dot(qn[:, h * nope_dim:(h + 1) * nope_dim], wuk_ref[h])
        pair = qr[:, (h // 2) * LANES:(h // 2 + 1) * LANES]
        slot = jnp.where(lo if h % 2 == 0 else jnp.logical_not(lo), pair, 0.0)
        q_ref[0, h] = jnp.concatenate([ql, slot], axis=1).astype(BF16)


def _latq(x, cos_tab, sin_tab, n_prompt_blocks, blocks_per_seq, weights, *, n_heads, kv_lora,
          rope_dim, nope_dim):
    n, d = x.shape
    tm = ATTN_TILE
    kw = kv_lora + 2 * rope_dim

    def tab_map(i):
        return (jnp.where(i < n_prompt_blocks, i % blocks_per_seq, blocks_per_seq), 0)

    tab = pl.BlockSpec((tm, LANES), tab_map)
    row = lambda w: pl.BlockSpec((tm, w), lambda i: (i, 0))
    npb = n_prompt_blocks
    n_p, n_s = npb * tm, n - npb * tm
    return pl.pallas_call(
        functools.partial(_latq_kernel, n_heads=n_heads, kv_lora=kv_lora, rope_dim=rope_dim,
                          nope_dim=nope_dim, npb=npb),
        grid=(n // tm,),
        in_specs=[row(d), tab, tab] + [_full(w.shape) for w in weights],
        out_specs=_split_specs(tm, kv_lora, npb) + _split_specs(tm, rope_dim, npb) + [
            row(kw),
            pl.BlockSpec((1, kv_lora, tm), lambda i: (i, 0, 0)),
            pl.BlockSpec((1, n_heads, tm, kw), lambda i: (i, 0, 0, 0))],
        out_shape=[jax.ShapeDtypeStruct((n_p, kv_lora), F32),
                   jax.ShapeDtypeStruct((n_s, kv_lora), F32),
                   jax.ShapeDtypeStruct((n_p, rope_dim), F32),
                   jax.ShapeDtypeStruct((n_s, rope_dim), F32),
                   jax.ShapeDtypeStruct((n, kw), BF16),
                   jax.ShapeDtypeStruct((n // tm, kv_lora, tm), BF16),
                   jax.ShapeDtypeStruct((n // tm, n_heads, tm, kw), BF16)],
        compiler_params=_cparams("arbitrary"),
        name="latent_q",
    )(x, cos_tab, sin_tab, *weights)


def _attn_prompt_kernel(q_ref, k_ref, kt_ref, o_ref, m_scr, l_scr, acc_scr, *, tq, kv_lora, scale):
    qi = pl.program_id(1)
    n_heads = q_ref.shape[1]
    m_scr[...] = jnp.full_like(m_scr, NEG)
    l_scr[...] = jnp.zeros_like(l_scr)
    acc_scr[...] = jnp.zeros_like(acc_scr)

    def step(j, masked):
        kc = k_ref[pl.ds(pl.multiple_of(j * tq, tq), tq), :]
        kt = kt_ref[j]
        if masked:
            key = lax.broadcasted_iota(jnp.int32, (tq, tq), 0)
            qry = lax.broadcasted_iota(jnp.int32, (tq, tq), 1)
            keep = key <= qry
        for h in range(n_heads):
            st = _dot_nt(kc, q_ref[0, h]) * scale
            if masked:
                st = jnp.where(keep, st, NEG)
            m_prev = m_scr[h]
            m_new = jnp.maximum(m_prev, jnp.max(st, axis=0, keepdims=True))
            p = jnp.exp(st - m_new)
            alpha = jnp.exp(m_prev - m_new)
            l_scr[h] = alpha * l_scr[h] + jnp.sum(p, axis=0, keepdims=True)
            acc_scr[h] = alpha * acc_scr[h] + _dot(kt, p.astype(BF16))
            m_scr[h] = m_new

    def body(j, carry):
        step(j, False)
        return carry

    lax.fori_loop(0, qi, body, 0)
    step(qi, True)
    for h in range(n_heads):
        o_ref[:, h * kv_lora:(h + 1) * kv_lora] = (acc_scr[h] / l_scr[h]).T.astype(BF16)


def _attn_prompt(q, kcat, kvt, n_rows, *, B, T, kv_lora, scale):
    tq = ATTN_TILE
    _, n_heads, _, kw = q.shape
    nq = T // tq
    return pl.pallas_call(
        functools.partial(_attn_prompt_kernel, tq=tq, kv_lora=kv_lora, scale=scale),
        grid=(B, nq),
        in_specs=[pl.BlockSpec((1, n_heads, tq, kw), lambda b, i: (b * nq + i, 0, 0, 0)),
                  pl.BlockSpec((T, kw), lambda b, i: (b, 0)),
                  pl.BlockSpec((nq, kv_lora, tq), lambda b, i: (b, 0, 0))],
        out_specs=pl.BlockSpec((tq, n_heads * kv_lora), lambda b, i: (b * nq + i, 0)),
        out_shape=jax.ShapeDtypeStruct((n_rows, n_heads * kv_lora), BF16),
        scratch_shapes=[pltpu.VMEM((n_heads, 1, tq), F32), pltpu.VMEM((n_heads, 1, tq), F32),
                        pltpu.VMEM((n_heads, kv_lora, tq), F32)],
        compiler_params=_cparams("parallel", "arbitrary"),
        name="attn_prompt",
    )(q, kcat, kvt)


def _attn_sample_kernel(pt_ref, q_ref, knew_ref, *refs, n_pages, t_valid, kv_lora, rope_dim, scale):
    ck_refs = refs[:n_pages]
    kr_refs = refs[n_pages:2 * n_pages]
    o_ref, m_scr, l_scr, acc_scr = refs[2 * n_pages:]
    g = pl.program_id(1)

    @pl.when(g == 0)
    def _():
        m_scr[...] = jnp.full_like(m_scr, NEG)
        l_scr[...] = jnp.zeros_like(l_scr)
        acc_scr[...] = jnp.zeros_like(acc_scr)

    q = q_ref[0]
    ql = q[:, :kv_lora]
    qr = q[:, kv_lora:]

    def update(s, values):
        m_prev = m_scr[...]
        m_new = jnp.maximum(m_prev, jnp.max(s, axis=1, keepdims=True))
        p = jnp.exp(s - m_new)
        alpha = jnp.exp(m_prev - m_new)
        l_scr[...] = alpha * l_scr[...] + jnp.sum(p, axis=1, keepdims=True)
        p = p.astype(BF16)
        pv = _dot(p[:, :values[0].shape[0]], values[0])
        for i in range(1, len(values)):
            rows = values[i].shape[0]
            pv += _dot(p[:, i * rows:(i + 1) * rows], values[i])
        acc_scr[...] = alpha * acc_scr[...] + pv
        m_scr[...] = m_new

    cks = [r[0].astype(BF16) for r in ck_refs]
    s = jnp.concatenate(
        [_dot_nt(ql, ck) + _dot(qr, kr[0].astype(BF16)) for ck, kr in zip(cks, kr_refs)],
        axis=1) * scale
    update(s, cks)

    @pl.when(g == pl.num_programs(1) - 1)
    def _():
        kn = knew_ref[0]
        ckn = kn[:, :kv_lora]
        sn = (_dot_nt(ql, ckn) + _dot_nt(qr, kn[:, kv_lora:kv_lora + rope_dim])) * scale
        t = lax.broadcasted_iota(jnp.int32, sn.shape, 0) % t_valid
        j = lax.broadcasted_iota(jnp.int32, sn.shape, 1)
        update(jnp.where(j <= t, sn, NEG), [ckn])
        o_ref[0] = acc_scr[...] / l_scr[...]


def _attn_sample(page_table, q, knew, cache_ckv, cache_krope_t, *, t_valid, scale):
    nb, n_pages = page_table.shape
    _, rows, qw = q.shape
    _, page, kv_lora = cache_ckv.shape
    rope_dim = cache_krope_t.shape[1]
    pg = min(PAGES_PER_STEP, n_pages)
    assert n_pages % pg == 0
    ck_specs = [pl.BlockSpec((1, page, kv_lora),
                             functools.partial(lambda b, g, pt, i: (pt[b, g * pg + i], 0, 0), i=i))
                for i in range(pg)]
    kr_specs = [pl.BlockSpec((1, rope_dim, page),
                             functools.partial(lambda b, g, pt, i: (pt[b, g * pg + i], 0, 0), i=i))
                for i in range(pg)]
    grid_spec = pltpu.PrefetchScalarGridSpec(
        num_scalar_prefetch=1, grid=(nb, n_pages // pg),
        in_specs=[pl.BlockSpec((1, rows, qw), lambda b, g, pt: (b, 0, 0)),
                  pl.BlockSpec((1,) + knew.shape[1:], lambda b, g, pt: (b, 0, 0))]
        + ck_specs + kr_specs,
        out_specs=pl.BlockSpec((1, rows, kv_lora), lambda b, g, pt: (b, 0, 0)),
        scratch_shapes=[pltpu.VMEM((rows, 1), F32), pltpu.VMEM((rows, 1), F32),
                        pltpu.VMEM((rows, kv_lora), F32)])
    return pl.pallas_call(
        functools.partial(_attn_sample_kernel, n_pages=pg, t_valid=t_valid, kv_lora=kv_lora,
                          rope_dim=rope_dim, scale=scale),
        grid_spec=grid_spec,
        out_shape=jax.ShapeDtypeStruct((nb, rows, kv_lora), F32),
        compiler_params=_cparams("parallel", "arbitrary"),
        name="attn_sample",
    )(page_table, q, knew, *([cache_ckv] * pg), *([cache_krope_t] * pg))


def _attn_out_kernel(ol_ref, x_ref, wuv_ref, wo_ref, g_ref, wr_ref,
                     x3_ref, xn_ref, ids_ref, wts_ref, *, n_experts):
    n_heads, kv_lora, _ = wuv_ref.shape
    o = jnp.concatenate(
        [_dot(ol_ref[:, h * kv_lora:(h + 1) * kv_lora], wuv_ref[h]) for h in range(n_heads)],
        axis=1).astype(BF16)
    x3 = x_ref[...] + _dot(o, wo_ref[...])
    x3_ref[...] = x3
    xn = _rms(x3, g_ref[...]).astype(BF16)
    xn_ref[...] = xn
    logits = _dot(xn, wr_ref[...])
    lane = lax.broadcasted_iota(jnp.int32, logits.shape, 1)
    lane_f = lane.astype(F32)
    lg = jnp.where(lane < n_experts, logits, -jnp.inf)
    v1 = jnp.max(lg, axis=1, keepdims=True)
    i1 = jnp.min(jnp.where(lg == v1, lane_f, float(LANES)), axis=1, keepdims=True)
    lg2 = jnp.where(lane_f == i1, -jnp.inf, lg)
    v2 = jnp.max(lg2, axis=1, keepdims=True)
    i2 = jnp.min(jnp.where(lg2 == v2, lane_f, float(LANES)), axis=1, keepdims=True)
    e = jnp.exp(v2 - v1)
    w1 = 1.0 / (1.0 + e)
    w2 = e / (1.0 + e)
    ids_ref[...] = jnp.where(lane == 0, i1, jnp.where(lane == 1, i2, 0.0)).astype(jnp.int32)
    wts_ref[...] = jnp.where(lane == 0, w1, jnp.where(lane == 1, w2, 0.0))


def _attn_out(o_lat, x, wuv, wo, g, wr, *, n_experts):
    n, d = x.shape
    tm = TOKEN_TILE
    row = lambda w: pl.BlockSpec((tm, w), lambda i: (i, 0))
    return pl.pallas_call(
        functools.partial(_attn_out_kernel, n_experts=n_experts),
        grid=(n // tm,),
        in_specs=[row(o_lat.shape[1]), row(d), _full(wuv.shape), _full(wo.shape), _full(g.shape),
                  _full(wr.shape)],
        out_specs=[row(d), row(d), row(LANES), row(LANES)],
        out_shape=[jax.ShapeDtypeStruct((n, d), F32), jax.ShapeDtypeStruct((n, d), BF16),
                   jax.ShapeDtypeStruct((n, LANES), jnp.int32),
                   jax.ShapeDtypeStruct((n, LANES), F32)],
        compiler_params=_cparams("parallel"),
        name="attn_out_router",
    )(o_lat, x, wuv, wo, g, wr)


def _moe_kernel(te_ref, nu_ref, xs_ref, *refs, n_chunks):
    wg_refs = refs[:n_chunks]
    wu_refs = refs[n_chunks:2 * n_chunks]
    wd_refs = refs[2 * n_chunks:3 * n_chunks]
    out_ref = refs[3 * n_chunks]
    t = pl.program_id(0)

    @pl.when(t < nu_ref[0])
    def _():
        xs = xs_ref[...]
        acc = jnp.zeros(out_ref.shape, F32)
        for wg_ref, wu_ref, wd_ref in zip(wg_refs, wu_refs, wd_refs):
            gate = _dot(xs, wg_ref[0])
            up = _dot(xs, wu_ref[0])
            acc += _dot((gate * _sigmoid(gate) * up).astype(BF16), wd_ref[0])
        out_ref[...] = acc

    @pl.when(t >= nu_ref[0])
    def _():
        out_ref[...] = jnp.zeros_like(out_ref)


def _moe(tile_expert, n_used, xs, wg_chunks, wu_chunks, wd_chunks):
    rows, d = xs.shape
    tm = MOE_TILE
    expert = lambda t, te, nu: (te[t], 0, 0)
    weights = list(wg_chunks) + list(wu_chunks) + list(wd_chunks)
    w_specs = [pl.BlockSpec((1,) + w.shape[1:], expert, pipeline_mode=pl.Buffered(1))
               for w in weights]
    grid_spec = pltpu.PrefetchScalarGridSpec(
        num_scalar_prefetch=2, grid=(rows // tm,),
        in_specs=[pl.BlockSpec((tm, d), lambda t, te, nu: (t, 0))] + w_specs,
        out_specs=pl.BlockSpec((tm, d), lambda t, te, nu: (t, 0)))
    return pl.pallas_call(
        functools.partial(_moe_kernel, n_chunks=len(wg_chunks)),
        grid_spec=grid_spec,
        out_shape=jax.ShapeDtypeStruct((rows, d), F32),
        compiler_params=_cparams("arbitrary"),
        name="moe_experts",
    )(tile_expert, n_used, xs, *weights)


def _final_kernel(x_ref, y1_ref, y2_ref, wts_ref, g_ref, outp_ref, outs_ref, *, npb):
    w = wts_ref[...]
    x4 = x_ref[...] + (w[:, 0:1] * y1_ref[...] + w[:, 1:2] * y2_ref[...])
    _store_split(npb, outp_ref, outs_ref, _rms(x4, g_ref[...]))


def _final(x, y1, y2, wts, g, n_p):
    n, d = x.shape
    tm = TOKEN_TILE
    npb = n_p // tm
    row = lambda w: pl.BlockSpec((tm, w), lambda i: (i, 0))
    return pl.pallas_call(
        functools.partial(_final_kernel, npb=npb), grid=(n // tm,),
        in_specs=[row(d), row(d), row(d), row(LANES), _full(g.shape)],
        out_specs=_split_specs(tm, d, npb),
        out_shape=[jax.ShapeDtypeStruct((n_p, d), F32), jax.ShapeDtypeStruct((n - n_p, d), F32)],
        compiler_params=_cparams("arbitrary"),
        name="combine_final",
    )(x, y1, y2, wts, g)


def _rot_cols(w, half):
    return jnp.concatenate([-w[..., half:], w[..., :half]], axis=-1)


def _route(ids, n_experts, tile):
    n2 = ids.shape[0]
    onehot = (ids[:, None] == jnp.arange(n_experts, dtype=jnp.int32)[None, :]).astype(jnp.int32)
    rank = jnp.take_along_axis(jnp.cumsum(onehot, axis=0) - onehot, ids[:, None], axis=1)[:, 0]
    counts = jnp.sum(onehot, axis=0)
    padded = (counts + tile - 1) // tile * tile
    ends = jnp.cumsum(padded)
    pos = (ends - padded)[ids] + rank
    n_tiles = -(-n2 // tile) + n_experts
    tile_start = jnp.arange(n_tiles, dtype=jnp.int32) * tile
    n_used = (ends[-1] // tile).astype(jnp.int32)
    te = jnp.sum(tile_start[:, None] >= ends[None, :], axis=1).astype(jnp.int32)
    te = jnp.minimum(te, te[jnp.maximum(n_used - 1, 0)])
    return pos, te, n_used.reshape(1), n_tiles * tile


def kernel(x_prompt, x_sample, state_C, state_n, state_m, cache_ckv, cache_krope, page_table, g_norm_a, w_in_a, b_gate_a, g_head_a, w_out_a, g_kv, w_dkv, g_ckv, w_uk, w_uv, g_norm_b, w_dq, g_q, w_uq, w_o_b, g_ffn_d, w_gate_d, w_up_d, w_down_d, g_ffn_m, w_router, w_gate_m, w_up_m, w_down_m, g_final):
    B, T, D = x_prompt.shape
    DB, TS, _ = x_sample.shape
    H, DV = g_head_a.shape[1:]
    DK = state_C.shape[3]
    kv_lora, n_bheads, nope_dim = w_uk.shape
    rope_dim = cache_krope.shape[2]
    page = cache_ckv.shape[1]
    past_len = page_table.shape[1] * page
    n_experts = w_router.shape[2]
    assert state_C.shape[0] == 1 and g_norm_b.shape[0] == 1 and g_ffn_d.shape[0] == 1
    assert TS <= SAMPLE_PAD and 2 * rope_dim == LANES and 2 * H <= LANES
    TP = SAMPLE_PAD
    n_p, n_s = B * T, DB * TP
    n = n_p + n_s
    assert n_p % TOKEN_TILE == 0 and n_s % TOKEN_TILE == 0 and T % ATTN_TILE == 0

    x_p = x_prompt.reshape(n_p, D)
    x_s = jnp.pad(x_sample, ((0, 0), (0, TP - TS), (0, 0))).reshape(n_s, D)

    w_in = w_in_a[0].astype(BF16)
    hq, hv = H * DK, H * DV
    wq, wk, wv, wo = (w_in[:, :hq], w_in[:, hq:2 * hq], w_in[:, 2 * hq:2 * hq + hv],
                      w_in[:, 2 * hq + hv:2 * hq + 2 * hv])
    wgate = jnp.pad(w_in[:, 2 * hq + 2 * hv:], ((0, 0), (0, LANES - 2 * H)))
    bgate = jnp.pad(b_gate_a[0], (0, LANES - 2 * H)).reshape(1, LANES)
    q, k, v, o, gates = _in_proj(x_p, x_s, g_norm_a, wq, wk, wv, wo, wgate, bgate,
                                 n_heads=H, q_scale=DK ** -0.5)
    hg, c_p, n_pr, m_p = _mlstm(q, k, v, o, gates, g_head_a[0], row0=0, B=B, T=T,
                                L=MLSTM_CHUNK, t_valid=MLSTM_CHUNK)
    hg, c_s, n_sm, m_s = _mlstm(q, k, v, o, gates, g_head_a[0], row0=n_p, B=DB, T=TP, L=TP,
                                t_valid=TS, state=(state_C[0], state_n[0], state_m[0]), hg_prev=hg)

    f = w_gate_d.shape[2]
    f_pad = -(-f // FFN_CHUNK) * FFN_CHUNK
    nch = f_pad // FFN_CHUNK
    col_chunks = lambda w: jnp.pad(w.astype(BF16), ((0, 0), (0, f_pad - f))).reshape(
        D, nch, FFN_CHUNK).transpose(1, 0, 2)
    wd_d = jnp.pad(w_down_d[0].astype(BF16), ((0, f_pad - f), (0, 0))).reshape(nch, FFN_CHUNK, D)
    x2 = _ffn(x_p, x_s, hg, w_out_a[0].astype(BF16), g_ffn_d, col_chunks(w_gate_d[0]),
              col_chunks(w_up_d[0]), wd_d)

    half = rope_dim // 2
    inv = ROPE_THETA ** (-jnp.arange(half, dtype=F32) / half)

    def tables(pos):
        ang = pos.astype(F32)[:, None] * inv[None, :]
        return (jnp.tile(jnp.cos(ang), (1, LANES // half)), jnp.tile(jnp.sin(ang), (1, LANES // half)))

    cos_p, sin_p = tables(jnp.arange(T, dtype=jnp.int32))
    cos_s, sin_s = tables(past_len + jnp.arange(TP, dtype=jnp.int32))
    reps = ATTN_TILE // TP
    cos_tab = jnp.concatenate([cos_p, jnp.tile(cos_s, (reps, 1))], axis=0)
    sin_tab = jnp.concatenate([sin_p, jnp.tile(sin_s, (reps, 1))], axis=0)

    w_kr = w_dkv[:, kv_lora:]
    wdkv = jnp.concatenate([w_dkv[:, :kv_lora], w_kr, _rot_cols(w_kr, half)], axis=1).astype(BF16)
    wuq = w_uq[0].reshape(-1, n_bheads, nope_dim + rope_dim)
    w_nope = wuq[:, :, :nope_dim].reshape(-1, n_bheads * nope_dim).astype(BF16)
    w_rope = wuq[:, :, nope_dim:]
    w_r = w_rope.reshape(-1, n_bheads * rope_dim).astype(BF16)
    w_rr = _rot_cols(w_rope, half).reshape(-1, n_bheads * rope_dim).astype(BF16)
    w_ukt = jnp.transpose(w_uk, (1, 2, 0)).astype(BF16)
    lat_weights = [g_kv.reshape(1, D), wdkv, g_ckv.reshape(1, kv_lora), g_norm_b,
                   w_dq[0].astype(BF16), g_q, w_nope, w_r, w_rr, w_ukt]
    ckv_p, ckv_s, kr_p, kr_s, kcat, kvt, qcat = _latq(
        x2, cos_tab, sin_tab, n_p // ATTN_TILE, T // ATTN_TILE, lat_weights, n_heads=n_bheads,
        kv_lora=kv_lora, rope_dim=rope_dim, nope_dim=nope_dim)

    scale = (nope_dim + rope_dim) ** -0.5
    o_lat = _attn_prompt(qcat, kcat, kvt, n, B=B, T=T, kv_lora=kv_lora, scale=scale)

    kw = kv_lora + 2 * rope_dim
    q_s = qcat[n_p // ATTN_TILE:].reshape(-1, n_bheads, ATTN_TILE // TP, TP, kw)[:, :, :, :TS]
    q_s = q_s.transpose(0, 2, 1, 3, 4).reshape(DB, n_bheads * TS, kw)
    q_s = jnp.concatenate([q_s[..., :kv_lora],
                           q_s[..., kv_lora:kv_lora + rope_dim] + q_s[..., kv_lora + rope_dim:]], axis=-1)
    k_new = jnp.pad(kcat[n_p:].reshape(DB, TP, kw), ((0, 0), (0, LANES - TP), (0, 0)))
    o_s = _attn_sample(page_table, q_s, k_new, cache_ckv, jnp.swapaxes(cache_krope, 1, 2),
                       t_valid=TS, scale=scale)
    o_s = o_s.reshape(DB, n_bheads, TS, kv_lora).transpose(0, 2, 1, 3)
    o_s = jnp.pad(o_s, ((0, 0), (0, TP - TS), (0, 0), (0, 0))).reshape(n_s, n_bheads * kv_lora)
    o_lat = lax.dynamic_update_slice(o_lat, o_s.astype(BF16), (n_p, 0))

    w_uvh = jnp.transpose(w_uv, (1, 0, 2)).astype(BF16)
    w_rt = jnp.pad(w_router[0], ((0, 0), (0, LANES - n_experts))).astype(BF16)
    x3, xn_m, ids, wts = _attn_out(o_lat, x2, w_uvh, w_o_b[0].astype(BF16), g_ffn_m, w_rt,
                                   n_experts=n_experts)

    top_k = 2
    pos, tile_expert, n_used, n_rows = _route(ids[:, :top_k].reshape(-1), n_experts, MOE_TILE)
    row_token = jnp.zeros((n_rows,), jnp.int32).at[pos].set(
        jnp.arange(n * top_k, dtype=jnp.int32) // top_k)
    ys = _moe(tile_expert, n_used, xn_m[row_token],
              _to_bf16_chunks(w_gate_m[0], MOE_CHUNK, 2), _to_bf16_chunks(w_up_m[0], MOE_CHUNK, 2),
              _to_bf16_chunks(w_down_m[0], MOE_CHUNK, 1))
    pos = pos.reshape(n, top_k)
    y_p, y_s = _final(x3, ys[pos[:, 0]], ys[pos[:, 1]], wts, g_final.reshape(1, D), n_p)

    prompt = lambda a: a.reshape(B, T, a.shape[1])
    sample = lambda a: a.reshape(DB, TP, a.shape[1])[:, :TS]
    return (prompt(y_p), sample(y_s), c_p[None], n_pr[None], m_p[None], prompt(ckv_p), prompt(kr_p),
            c_s[None], n_sm[None], m_s[None], sample(ckv_s), sample(kr_s))
```

```python
import functools

import jax
import jax.numpy as jnp
from jax import lax
from jax.experimental import pallas as pl
from jax.experimental.pallas import tpu as pltpu

F32 = jnp.float32
BF16 = jnp.bfloat16

EPS = 1e-6
GATE_CAP = 15.0
ROPE_THETA = 10000.0
NEG = -1e30
LANES = 128
VMEM_LIMIT_BYTES = 56 * 2**20

SAMPLE_PAD = 8
TOKEN_TILE = 512
ATTN_TILE = 512
MLSTM_CHUNK = 256
SHORT_SEQS_PER_STEP = 1
FFN_CHUNK = 1408
MOE_TILE = 512
MOE_CHUNK = 512
PAGES_PER_STEP = 64


def _cparams(*sem, flags=None):
    return pltpu.CompilerParams(dimension_semantics=sem, vmem_limit_bytes=VMEM_LIMIT_BYTES,
                                flags=flags)


def _dot(a, b):
    return jnp.dot(a, b, preferred_element_type=F32)


def _dot_nt(a, b):
    return lax.dot_general(a, b, (((1,), (1,)), ((), ())), preferred_element_type=F32)


def _rms(x, g):
    return x * lax.rsqrt(jnp.mean(x * x, axis=-1, keepdims=True) + EPS) * g


def _sigmoid(x):
    return 1.0 / (1.0 + jnp.exp(-x))


def _split3(x):
    hi = x.astype(BF16)
    r1 = x - hi.astype(F32)
    mid = r1.astype(BF16)
    lo = (r1 - mid.astype(F32)).astype(BF16)
    return hi, mid, lo


def _split_specs(tm, w, npb):
    return [pl.BlockSpec((tm, w), lambda i: (jnp.minimum(i, npb - 1), 0)),
            pl.BlockSpec((tm, w), lambda i: (jnp.maximum(i - npb, 0), 0))]


def _load_split(npb, p_ref, s_ref):
    return jnp.where(pl.program_id(0) < npb, p_ref[...], s_ref[...])


def _store_split(npb, p_ref, s_ref, val):
    i = pl.program_id(0)

    @pl.when(i < npb)
    def _():
        p_ref[...] = val

    @pl.when(i >= npb)
    def _():
        s_ref[...] = val


def _full(shape):
    nd = len(shape)
    return pl.BlockSpec(shape, lambda *_: (0,) * nd)


def _in_proj_kernel(xp_ref, xs_ref, g_ref, wq_ref, wk_ref, wv_ref, wo_ref, wg_ref, b_ref,
                    q_ref, k_ref, v_ref, o_ref, gate_ref, *, n_heads, q_scale, npb):
    xn = _rms(_load_split(npb, xp_ref, xs_ref), g_ref[...]).astype(BF16)
    q_ref[...] = _dot(xn, wq_ref[...]) * q_scale
    k_ref[...] = _dot(xn, wk_ref[...])
    v_ref[...] = _dot(xn, wv_ref[...])
    o_ref[...] = _dot(xn, wo_ref[...])
    g = _dot(xn, wg_ref[...]) + b_ref[...]
    g = GATE_CAP * jnp.tanh(g / GATE_CAP)
    logf = jnp.minimum(g, 0.0) - jnp.log1p(jnp.exp(-jnp.abs(g)))
    lane = lax.broadcasted_iota(jnp.int32, g.shape, 1)
    gate_ref[...] = jnp.where(lane < n_heads, g, jnp.where(lane < 2 * n_heads, logf, 0.0))


def _in_proj(xp, xs, g, wq, wk, wv, wo, wg, b, *, n_heads, q_scale):
    d = xp.shape[1]
    n = xp.shape[0] + xs.shape[0]
    tm = TOKEN_TILE
    npb = xp.shape[0] // tm
    row = lambda w: pl.BlockSpec((tm, w), lambda i: (i, 0))
    outs = [(wq.shape[1], F32), (wk.shape[1], F32), (wv.shape[1], F32), (wo.shape[1], F32), (LANES, F32)]
    return pl.pallas_call(
        functools.partial(_in_proj_kernel, n_heads=n_heads, q_scale=q_scale, npb=npb),
        grid=(n // tm,),
        in_specs=_split_specs(tm, d, npb) + [
            _full(g.shape), _full(wq.shape), _full(wk.shape), _full(wv.shape),
            _full(wo.shape), _full(wg.shape), _full(b.shape)],
        out_specs=[row(w) for w, _ in outs],
        out_shape=[jax.ShapeDtypeStruct((n, w), dt) for w, dt in outs],
        compiler_params=_cparams("parallel"),
        name="in_proj",
    )(xp, xs, g, wq, wk, wv, wo, wg, b)


def _mlstm_kernel(*refs, L, H, DK, DV, nb, t_valid, has_state, mm_dtype):
    if has_state:
        (q_ref, k_ref, v_ref, o_ref, gate_ref, gh_ref, c0_ref, n0_ref, m0_ref, _,
         hg_ref, cout_ref, nout_ref, mout_ref, caug_ref, m_scr) = refs
    else:
        (q_ref, k_ref, v_ref, o_ref, gate_ref, gh_ref,
         hg_ref, cout_ref, nout_ref, mout_ref, caug_ref, m_scr) = refs
    c = pl.program_id(1)
    last = pl.num_programs(1) - 1

    rk = lax.broadcasted_iota(jnp.int32, (DK, DK), 0)
    ck = lax.broadcasted_iota(jnp.int32, (DK, DK), 1)
    eye_k = rk == ck

    @pl.when(c == 0)
    def _():
        if has_state:
            for sh in range(nb * H):
                s, h = divmod(sh, H)
                caug_ref[sh, :, :DV] = c0_ref[s, h]
                n_row = n0_ref[s, h:h + 1, :]
                n_col = jnp.sum(jnp.where(eye_k, jnp.broadcast_to(n_row, (DK, DK)), 0.0),
                                axis=1, keepdims=True)
                caug_ref[sh, :, DV:] = jnp.broadcast_to(n_col, (DK, DV))
                m_scr[sh:sh + 1, :] = jnp.broadcast_to(m0_ref[s, :, h:h + 1], (1, LANES))
        else:
            caug_ref[...] = jnp.zeros_like(caug_ref)
            m_scr[...] = jnp.zeros_like(m_scr)

    ri = lax.broadcasted_iota(jnp.int32, (L, L), 0)
    ci = lax.broadcasted_iota(jnp.int32, (L, L), 1)
    causal = ci <= ri
    eye = ci == ri
    ones_v = jnp.ones((L, DV), mm_dtype)
    use_mxu_cumsum = L % LANES == 0

    def head(s, h, gates, cums):
        rows = slice(s * L, (s + 1) * L)
        sh = s * H + h
        if use_mxu_cumsum:
            gates_t, cum, cum_t = cums
            a_col = cum[:, H + h:H + h + 1]
            b_row = gates_t[h:h + 1, :] - cum_t[H + h:H + h + 1, :]
        else:
            li_col = gates[:, h:h + 1]
            lf_col = gates[:, H + h:H + h + 1]
            lf_b = jnp.broadcast_to(lf_col, (L, L))
            lf_row = jnp.sum(jnp.where(eye, lf_b, 0.0), axis=0, keepdims=True)
            a_col = jnp.sum(jnp.where(causal, jnp.broadcast_to(lf_row, (L, L)), 0.0),
                            axis=1, keepdims=True)
            a_row = jnp.sum(jnp.where(ci >= ri, lf_b, 0.0), axis=0, keepdims=True)
            li_row = jnp.sum(jnp.where(eye, jnp.broadcast_to(li_col, (L, L)), 0.0),
                             axis=0, keepdims=True)
            b_row = li_row - a_row
        d = jnp.where(causal, a_col + b_row, NEG)
        m_prev = m_scr[sh:sh + 1, 0:1]
        inter = a_col + m_prev
        m_t = jnp.maximum(inter, jnp.max(d, axis=1, keepdims=True))

        qh = q_ref[rows, h * DK:(h + 1) * DK].astype(mm_dtype)
        kh = k_ref[rows, h * DK:(h + 1) * DK].astype(mm_dtype)
        vaug = jnp.concatenate([v_ref[rows, h * DV:(h + 1) * DV].astype(mm_dtype), ones_v], axis=1)
        caug = caug_ref[sh]

        w = jnp.exp(d - m_t) * _dot_nt(qh, kh)
        e_inter = jnp.exp(inter - m_t)
        num = _dot(w.astype(mm_dtype), vaug) + e_inter * _dot(qh, caug.astype(mm_dtype))
        hh = num[:, :DV] / jnp.maximum(jnp.abs(num[:, DV:]), jnp.exp(-m_t))
        hn = _rms(hh, gh_ref[h:h + 1, :])
        hg_ref[rows, h * DV:(h + 1) * DV] = _sigmoid(o_ref[rows, h * DV:(h + 1) * DV]) * hn

        m_new = m_t[L - 1:L, :]
        a_last = a_col[L - 1:L, :]
        e_end = jnp.exp(a_last + b_row - m_new)
        e_carry = jnp.exp(a_last + m_prev - m_new)
        k_t = _dot_nt(eye_k.astype(mm_dtype), kh)
        caug_new = e_carry * caug + _dot((k_t * e_end).astype(mm_dtype), vaug)
        caug_ref[sh] = caug_new
        m_scr[sh:sh + 1, :] = jnp.broadcast_to(m_new, (1, LANES))

        @pl.when(c == last)
        def _():
            cout_ref[s, h] = caug_new[:, :DV]
            nout_ref[s, h:h + 1, :] = jnp.sum(jnp.where(eye_k, caug_new[:, DV:DV + DK], 0.0),
                                              axis=0, keepdims=True)
            mout_ref[s, :, h:h + 1] = m_new

    for s in range(nb):
        gates = gate_ref[s * L:(s + 1) * L, :]
        if t_valid < L:
            t_id = lax.broadcasted_iota(jnp.int32, gates.shape, 0)
            lane = lax.broadcasted_iota(jnp.int32, gates.shape, 1)
            gates = jnp.where(t_id < t_valid, gates, jnp.where(lane < H, NEG, 0.0))
        cums = None
        if use_mxu_cumsum:
            gates_t = gates.T
            cum = sum(_dot(causal.astype(BF16), p) for p in _split3(gates))
            cum_t = sum(_dot(p, (ri <= ci).astype(BF16)) for p in _split3(gates_t))
            cums = (gates_t, cum, cum_t)
        for h in range(H):
            head(s, h, gates, cums)


def _mlstm(q, k, v, o, gates, g_head, *, row0, B, T, L, t_valid, state=None, hg_prev=None):
    n = q.shape[0]
    H, DV = g_head.shape
    DK = q.shape[1] // H
    nc = T // L
    nb = SHORT_SEQS_PER_STEP if nc == 1 else 1
    assert B % nb == 0 and row0 % (nb * L) == 0
    blk0 = row0 // (nb * L)
    row = lambda w: pl.BlockSpec((nb * L, w), lambda b, c: (blk0 + b * nc + c, 0))
    c_spec = pl.BlockSpec((nb, H, DK, DV), lambda b, c: (b, 0, 0, 0))
    n_spec = pl.BlockSpec((nb, H, DK), lambda b, c: (b, 0, 0))
    m_spec = pl.BlockSpec((nb, 1, H), lambda b, c: (b, 0, 0))
    in_specs = [row(H * DK), row(H * DK), row(H * DV), row(H * DV), row(LANES), _full(g_head.shape)]
    args = [q, k, v, o, gates, g_head]
    aliases = {}
    if state is not None:
        c0, n0, m0 = state
        in_specs += [c_spec, n_spec, m_spec, pl.BlockSpec(memory_space=pl.ANY)]
        args += [c0, n0, m0.reshape(B, 1, H), hg_prev]
        aliases = {len(args) - 1: 0}
    out_shape = [jax.ShapeDtypeStruct((n, H * DV), F32),
                 jax.ShapeDtypeStruct((B, H, DK, DV), F32),
                 jax.ShapeDtypeStruct((B, H, DK), F32),
                 jax.ShapeDtypeStruct((B, 1, H), F32)]
    out_specs = [row(H * DV), c_spec, n_spec, m_spec]
    kern = functools.partial(_mlstm_kernel, L=L, H=H, DK=DK, DV=DV, nb=nb, t_valid=t_valid,
                             has_state=state is not None,
                             mm_dtype=BF16 if L % 16 == 0 else F32)
    hg, c_out, n_out, m_out = pl.pallas_call(
        kern, grid=(B // nb, nc), in_specs=in_specs, out_specs=out_specs, out_shape=out_shape,
        scratch_shapes=[pltpu.VMEM((nb * H, DK, 2 * DV), F32), pltpu.VMEM((nb * H, LANES), F32)],
        input_output_aliases=aliases,
        compiler_params=_cparams("parallel", "arbitrary"),
        name="mlstm_sample" if state is not None else "mlstm_prompt",
    )(*args)
    return hg, c_out, n_out, m_out.reshape(B, H)


def _ffn_kernel(xp_ref, xs_ref, hg_ref, wout_ref, g_ref, wg_ref, wu_ref, wd_ref, out_ref, acc_ref,
                *, npb):
    x1 = _load_split(npb, xp_ref, xs_ref) + _dot(hg_ref[...].astype(BF16), wout_ref[...])
    xn = _rms(x1, g_ref[...]).astype(BF16)
    acc_ref[...] = jnp.zeros_like(acc_ref)

    def body(c, carry):
        gate = _dot(xn, wg_ref[c])
        up = _dot(xn, wu_ref[c])
        hmid = (gate * _sigmoid(gate) * up).astype(BF16)
        acc_ref[...] += _dot(hmid, wd_ref[c])
        return carry

    lax.fori_loop(0, wg_ref.shape[0], body, 0)
    out_ref[...] = x1 + acc_ref[...]


def _ffn(xp, xs, hg, wout, g, wg, wu, wd):
    n, d = hg.shape
    tm = TOKEN_TILE
    npb = xp.shape[0] // tm
    row = pl.BlockSpec((tm, d), lambda i: (i, 0))
    return pl.pallas_call(
        functools.partial(_ffn_kernel, npb=npb), grid=(n // tm,),
        in_specs=_split_specs(tm, d, npb) + [
            row, _full(wout.shape), _full(g.shape), _full(wg.shape), _full(wu.shape),
            _full(wd.shape)],
        out_specs=row, out_shape=jax.ShapeDtypeStruct((n, d), F32),
        scratch_shapes=[pltpu.VMEM((tm, d), F32)],
        compiler_params=_cparams("parallel"),
        name="outproj_ffn",
    )(xp, xs, hg, wout, g, wg, wu, wd)


def _latq_kernel(x_ref, cos_ref, sin_ref, gkv_ref, wdkv_ref, gckv_ref, gnb_ref, wdq_ref, gq_ref,
                 wn_ref, wr_ref, wrr_ref, wuk_ref,
                 ckvp_ref, ckvs_ref, krp_ref, krs_ref, kcat_ref, kvt_ref, q_ref,
                 *, n_heads, kv_lora, rope_dim, nope_dim, npb):
    x = x_ref[...]
    xs = x * lax.rsqrt(jnp.mean(x * x, axis=-1, keepdims=True) + EPS)
    cos = cos_ref[...]
    sin = sin_ref[...]
    lane = lax.broadcasted_iota(jnp.int32, cos.shape, 1)
    lo = lane < rope_dim

    lat = _dot((xs * gkv_ref[...]).astype(BF16), wdkv_ref[...])
    ckv = _rms(lat[:, :kv_lora], gckv_ref[...])
    _store_split(npb, ckvp_ref, ckvs_ref, ckv)
    t = lat[:, kv_lora:] * jnp.where(lo, cos, sin)
    kr2 = t + pltpu.roll(t, rope_dim, axis=1)
    _store_split(npb, krp_ref, krs_ref, kr2[:, :rope_dim])
    kcat_ref[...] = jnp.concatenate([ckv, kr2], axis=1).astype(BF16)
    kvt_ref[0] = ckv.T.astype(BF16)

    cq = _dot((xs * gnb_ref[...]).astype(BF16), wdq_ref[...])
    cqn = _rms(cq, gq_ref[...]).astype(BF16)
    qn = _dot(cqn, wn_ref[...]).astype(BF16)
    reps = n_heads * rope_dim // LANES
    cos_h = jnp.concatenate([cos] * reps, axis=1)
    sin_h = jnp.concatenate([sin] * reps, axis=1)
    qr = _dot(cqn, wr_ref[...]) * cos_h + _dot(cqn, wrr_ref[...]) * sin_h
    for h in range(n_heads):
        ql = _dot(qn[:, h * nope_dim:(h + 1) * nope_dim], wuk_ref[h])
        pair = qr[:, (h // 2) * LANES:(h // 2 + 1) * LANES]
        slot = jnp.where(lo if h % 2 == 0 else jnp.logical_not(lo), pair, 0.0)
        q_ref[0, h] = jnp.concatenate([ql, slot], axis=1).astype(BF16)


def _latq(x, cos_tab, sin_tab, n_prompt_blocks, blocks_per_seq, weights, *, n_heads, kv_lora,
          rope_dim, nope_dim):
    n, d = x.shape
    tm = ATTN_TILE
    kw = kv_lora + 2 * rope_dim

    def tab_map(i):
        return (jnp.where(i < n_prompt_blocks, i % blocks_per_seq, blocks_per_seq), 0)

    tab = pl.BlockSpec((tm, LANES), tab_map)
    row = lambda w: pl.BlockSpec((tm, w), lambda i: (i, 0))
    npb = n_prompt_blocks
    n_p, n_s = npb * tm, n - npb * tm
    return pl.pallas_call(
        functools.partial(_latq_kernel, n_heads=n_heads, kv_lora=kv_lora, rope_dim=rope_dim,
                          nope_dim=nope_dim, npb=npb),
        grid=(n // tm,),
        in_specs=[row(d), tab, tab] + [_full(w.shape) for w in weights],
        out_specs=_split_specs(tm, kv_lora, npb) + _split_specs(tm, rope_dim, npb) + [
            row(kw),
            pl.BlockSpec((1, kv_lora, tm), lambda i: (i, 0, 0)),
            pl.BlockSpec((1, n_heads, tm, kw), lambda i: (i, 0, 0, 0))],
        out_shape=[jax.ShapeDtypeStruct((n_p, kv_lora), F32),
                   jax.ShapeDtypeStruct((n_s, kv_lora), F32),
                   jax.ShapeDtypeStruct((n_p, rope_dim), F32),
                   jax.ShapeDtypeStruct((n_s, rope_dim), F32),
                   jax.ShapeDtypeStruct((n, kw), BF16),
                   jax.ShapeDtypeStruct((n // tm, kv_lora, tm), BF16),
                   jax.ShapeDtypeStruct((n // tm, n_heads, tm, kw), BF16)],
        compiler_params=_cparams("arbitrary"),
        name="latent_q",
    )(x, cos_tab, sin_tab, *weights)


def _attn_prompt_kernel(q_ref, k_ref, kt_ref, *refs, tq, kv_lora, scale, n_cast):
    cast_in = refs[:n_cast]
    o_ref = refs[n_cast]
    cast_out = refs[n_cast + 1:2 * n_cast + 1]
    m_scr, l_scr, acc_scr = refs[2 * n_cast + 1:]
    for src, dst in zip(cast_in, cast_out):
        dst[...] = src[...].astype(dst.dtype)

    qi = pl.program_id(1)
    n_heads = q_ref.shape[1]
    m_scr[...] = jnp.full_like(m_scr, NEG)
    l_scr[...] = jnp.zeros_like(l_scr)
    acc_scr[...] = jnp.zeros_like(acc_scr)

    def step(j, masked):
        kc = k_ref[pl.ds(pl.multiple_of(j * tq, tq), tq), :]
        kt = kt_ref[j]
        if masked:
            key = lax.broadcasted_iota(jnp.int32, (tq, tq), 0)
            qry = lax.broadcasted_iota(jnp.int32, (tq, tq), 1)
            keep = key <= qry
        for h in range(n_heads):
            st = _dot_nt(kc, q_ref[0, h]) * scale
            if masked:
                st = jnp.where(keep, st, NEG)
            m_prev = m_scr[h]
            m_new = jnp.maximum(m_prev, jnp.max(st, axis=0, keepdims=True))
            p = jnp.exp(st - m_new)
            alpha = jnp.exp(m_prev - m_new)
            l_scr[h] = alpha * l_scr[h] + jnp.sum(p, axis=0, keepdims=True)
            acc_scr[h] = alpha * acc_scr[h] + _dot(kt, p.astype(BF16))
            m_scr[h] = m_new

    def body(j, carry):
        step(j, False)
        return carry

    lax.fori_loop(0, qi, body, 0)
    step(qi, True)
    for h in range(n_heads):
        o_ref[:, h * kv_lora:(h + 1) * kv_lora] = (acc_scr[h] / l_scr[h]).T.astype(BF16)


def _attn_prompt(q, kcat, kvt, n_rows, to_cast, *, B, T, kv_lora, scale):
    tq = ATTN_TILE
    _, n_heads, _, kw = q.shape
    nq = T // tq
    steps = B * nq
    cast_specs = []
    for w in to_cast:
        rows = w.shape[0] // steps
        assert rows * steps == w.shape[0] and rows % 16 == 0
        cast_specs.append(pl.BlockSpec((rows, w.shape[1]), lambda b, i: (b * nq + i, 0)))
    out = pl.pallas_call(
        functools.partial(_attn_prompt_kernel, tq=tq, kv_lora=kv_lora, scale=scale,
                          n_cast=len(to_cast)),
        grid=(B, nq),
        in_specs=[pl.BlockSpec((1, n_heads, tq, kw), lambda b, i: (b * nq + i, 0, 0, 0)),
                  pl.BlockSpec((T, kw), lambda b, i: (b, 0)),
                  pl.BlockSpec((nq, kv_lora, tq), lambda b, i: (b, 0, 0))] + cast_specs,
        out_specs=[pl.BlockSpec((tq, n_heads * kv_lora), lambda b, i: (b * nq + i, 0))] + cast_specs,
        out_shape=[jax.ShapeDtypeStruct((n_rows, n_heads * kv_lora), BF16)]
        + [jax.ShapeDtypeStruct(w.shape, BF16) for w in to_cast],
        scratch_shapes=[pltpu.VMEM((n_heads, 1, tq), F32), pltpu.VMEM((n_heads, 1, tq), F32),
                        pltpu.VMEM((n_heads, kv_lora, tq), F32)],
        compiler_params=_cparams("parallel", "arbitrary"),
        name="attn_prompt",
    )(q, kcat, kvt, *to_cast)
    return out[0], out[1:]


def _attn_sample_kernel(pt_ref, q_ref, knew_ref, *refs, n_pages, t_valid, kv_lora, rope_dim, scale):
    ck_refs = refs[:n_pages]
    kr_refs = refs[n_pages:2 * n_pages]
    o_ref, m_scr, l_scr, acc_scr = refs[2 * n_pages:]
    g = pl.program_id(1)

    @pl.when(g == 0)
    def _():
        m_scr[...] = jnp.full_like(m_scr, NEG)
        l_scr[...] = jnp.zeros_like(l_scr)
        acc_scr[...] = jnp.zeros_like(acc_scr)

    q = q_ref[0]
    ql = q[:, :kv_lora]
    qr = q[:, kv_lora:]

    def update(s, values):
        m_prev = m_scr[...]
        m_new = jnp.maximum(m_prev, jnp.max(s, axis=1, keepdims=True))
        p = jnp.exp(s - m_new)
        alpha = jnp.exp(m_prev - m_new)
        l_scr[...] = alpha * l_scr[...] + jnp.sum(p, axis=1, keepdims=True)
        p = p.astype(BF16)
        pv = _dot(p[:, :values[0].shape[0]], values[0])
        for i in range(1, len(values)):
            rows = values[i].shape[0]
            pv += _dot(p[:, i * rows:(i + 1) * rows], values[i])
        acc_scr[...] = alpha * acc_scr[...] + pv
        m_scr[...] = m_new

    cks = [r[0].astype(BF16) for r in ck_refs]
    s = jnp.concatenate(
        [_dot_nt(ql, ck) + _dot(qr, kr[0].astype(BF16)) for ck, kr in zip(cks, kr_refs)],
        axis=1) * scale
    update(s, cks)

    @pl.when(g == pl.num_programs(1) - 1)
    def _():
        kn = knew_ref[0]
        ckn = kn[:, :kv_lora]
        sn = (_dot_nt(ql, ckn) + _dot_nt(qr, kn[:, kv_lora:kv_lora + rope_dim])) * scale
        t = lax.broadcasted_iota(jnp.int32, sn.shape, 0) % t_valid
        j = lax.broadcasted_iota(jnp.int32, sn.shape, 1)
        update(jnp.where(j <= t, sn, NEG), [ckn])
        o_ref[0] = acc_scr[...] / l_scr[...]


def _attn_sample(page_table, q, knew, cache_ckv, cache_krope_t, *, t_valid, scale):
    nb, n_pages = page_table.shape
    _, rows, qw = q.shape
    _, page, kv_lora = cache_ckv.shape
    rope_dim = cache_krope_t.shape[1]
    pg = min(PAGES_PER_STEP, n_pages)
    assert n_pages % pg == 0
    ck_specs = [pl.BlockSpec((1, page, kv_lora),
                             functools.partial(lambda b, g, pt, i: (pt[b, g * pg + i], 0, 0), i=i))
                for i in range(pg)]
    kr_specs = [pl.BlockSpec((1, rope_dim, page),
                             functools.partial(lambda b, g, pt, i: (pt[b, g * pg + i], 0, 0), i=i))
                for i in range(pg)]
    grid_spec = pltpu.PrefetchScalarGridSpec(
        num_scalar_prefetch=1, grid=(nb, n_pages // pg),
        in_specs=[pl.BlockSpec((1, rows, qw), lambda b, g, pt: (b, 0, 0)),
                  pl.BlockSpec((1,) + knew.shape[1:], lambda b, g, pt: (b, 0, 0))]
        + ck_specs + kr_specs,
        out_specs=pl.BlockSpec((1, rows, kv_lora), lambda b, g, pt: (b, 0, 0)),
        scratch_shapes=[pltpu.VMEM((rows, 1), F32), pltpu.VMEM((rows, 1), F32),
                        pltpu.VMEM((rows, kv_lora), F32)])
    return pl.pallas_call(
        functools.partial(_attn_sample_kernel, n_pages=pg, t_valid=t_valid, kv_lora=kv_lora,
                          rope_dim=rope_dim, scale=scale),
        grid_spec=grid_spec,
        out_shape=jax.ShapeDtypeStruct((nb, rows, kv_lora), F32),
        compiler_params=_cparams("parallel", "arbitrary"),
        name="attn_sample",
    )(page_table, q, knew, *([cache_ckv] * pg), *([cache_krope_t] * pg))


def _attn_out_kernel(ol_ref, x_ref, wuv_ref, wo_ref, g_ref, wr_ref,
                     x3_ref, xn_ref, ids_ref, wts_ref, *, n_experts):
    n_heads, kv_lora, _ = wuv_ref.shape
    o = jnp.concatenate(
        [_dot(ol_ref[:, h * kv_lora:(h + 1) * kv_lora], wuv_ref[h]) for h in range(n_heads)],
        axis=1).astype(BF16)
    x3 = x_ref[...] + _dot(o, wo_ref[...])
    x3_ref[...] = x3
    xn = _rms(x3, g_ref[...]).astype(BF16)
    xn_ref[...] = xn
    logits = _dot(xn, wr_ref[...])
    lane = lax.broadcasted_iota(jnp.int32, logits.shape, 1)
    lane_f = lane.astype(F32)
    lg = jnp.where(lane < n_experts, logits, -jnp.inf)
    v1 = jnp.max(lg, axis=1, keepdims=True)
    i1 = jnp.min(jnp.where(lg == v1, lane_f, float(LANES)), axis=1, keepdims=True)
    lg2 = jnp.where(lane_f == i1, -jnp.inf, lg)
    v2 = jnp.max(lg2, axis=1, keepdims=True)
    i2 = jnp.min(jnp.where(lg2 == v2, lane_f, float(LANES)), axis=1, keepdims=True)
    e = jnp.exp(v2 - v1)
    w1 = 1.0 / (1.0 + e)
    w2 = e / (1.0 + e)
    ids_ref[...] = jnp.where(lane == 0, i1, jnp.where(lane == 1, i2, 0.0)).astype(jnp.int32)
    wts_ref[...] = jnp.where(lane == 0, w1, jnp.where(lane == 1, w2, 0.0))


def _attn_out(o_lat, x, wuv, wo, g, wr, *, n_experts):
    n, d = x.shape
    tm = TOKEN_TILE
    row = lambda w: pl.BlockSpec((tm, w), lambda i: (i, 0))
    return pl.pallas_call(
        functools.partial(_attn_out_kernel, n_experts=n_experts),
        grid=(n // tm,),
        in_specs=[row(o_lat.shape[1]), row(d), _full(wuv.shape), _full(wo.shape), _full(g.shape),
                  _full(wr.shape)],
        out_specs=[row(d), row(d), row(LANES), row(LANES)],
        out_shape=[jax.ShapeDtypeStruct((n, d), F32), jax.ShapeDtypeStruct((n, d), BF16),
                   jax.ShapeDtypeStruct((n, LANES), jnp.int32),
                   jax.ShapeDtypeStruct((n, LANES), F32)],
        compiler_params=_cparams("parallel"),
        name="attn_out_router",
    )(o_lat, x, wuv, wo, g, wr)


def _moe_kernel(te_ref, nu_ref, xs_ref, wg_ref, wu_ref, wd_ref, out_ref, *, chunk):
    t = pl.program_id(0)

    @pl.when(t < nu_ref[0])
    def _():
        xs = xs_ref[...]
        acc = jnp.zeros(out_ref.shape, F32)
        for c in range(wg_ref.shape[2] // chunk):
            sl = slice(c * chunk, (c + 1) * chunk)
            gate = _dot(xs, wg_ref[0, :, sl])
            up = _dot(xs, wu_ref[0, :, sl])
            acc += _dot((gate * _sigmoid(gate) * up).astype(BF16), wd_ref[0, sl, :])
        out_ref[...] = acc.astype(out_ref.dtype)

    @pl.when(t >= nu_ref[0])
    def _():
        out_ref[...] = jnp.zeros_like(out_ref)


def _moe(tile_expert, n_used, xs, wg, wu, wd):
    rows, d = xs.shape
    tm = MOE_TILE
    expert = lambda t, te, nu: (te[t], 0, 0)
    w_specs = [pl.BlockSpec((1,) + w.shape[1:], expert, pipeline_mode=pl.Buffered(1))
               for w in (wg, wu, wd)]
    grid_spec = pltpu.PrefetchScalarGridSpec(
        num_scalar_prefetch=2, grid=(rows // tm,),
        in_specs=[pl.BlockSpec((tm, d), lambda t, te, nu: (t, 0))] + w_specs,
        out_specs=pl.BlockSpec((tm, d), lambda t, te, nu: (t, 0)))
    return pl.pallas_call(
        functools.partial(_moe_kernel, chunk=MOE_CHUNK),
        grid_spec=grid_spec,
        out_shape=jax.ShapeDtypeStruct((rows, d), BF16),
        compiler_params=_cparams("arbitrary"),
        name="moe_experts",
    )(tile_expert, n_used, xs, wg, wu, wd)


def _final_kernel(x_ref, y1_ref, y2_ref, wts_ref, g_ref, outp_ref, outs_ref, *, npb):
    w = wts_ref[...]
    x4 = x_ref[...] + (w[:, 0:1] * y1_ref[...].astype(F32) + w[:, 1:2] * y2_ref[...].astype(F32))
    _store_split(npb, outp_ref, outs_ref, _rms(x4, g_ref[...]))


def _final(x, y1, y2, wts, g, n_p):
    n, d = x.shape
    tm = TOKEN_TILE
    npb = n_p // tm
    row = lambda w: pl.BlockSpec((tm, w), lambda i: (i, 0))
    return pl.pallas_call(
        functools.partial(_final_kernel, npb=npb), grid=(n // tm,),
        in_specs=[row(d), row(d), row(d), row(LANES), _full(g.shape)],
        out_specs=_split_specs(tm, d, npb),
        out_shape=[jax.ShapeDtypeStruct((n_p, d), F32), jax.ShapeDtypeStruct((n - n_p, d), F32)],
        compiler_params=_cparams("arbitrary"),
        name="combine_final",
    )(x, y1, y2, wts, g)


def _rot_cols(w, half):
    return jnp.concatenate([-w[..., half:], w[..., :half]], axis=-1)


def _route(ids, n_experts, tile):
    n2 = ids.shape[0]
    onehot = (ids[:, None] == jnp.arange(n_experts, dtype=jnp.int32)[None, :]).astype(jnp.int32)
    rank = jnp.take_along_axis(jnp.cumsum(onehot, axis=0) - onehot, ids[:, None], axis=1)[:, 0]
    counts = jnp.sum(onehot, axis=0)
    padded = (counts + tile - 1) // tile * tile
    ends = jnp.cumsum(padded)
    pos = (ends - padded)[ids] + rank
    n_tiles = -(-n2 // tile) + n_experts
    tile_start = jnp.arange(n_tiles, dtype=jnp.int32) * tile
    n_used = (ends[-1] // tile).astype(jnp.int32)
    te = jnp.sum(tile_start[:, None] >= ends[None, :], axis=1).astype(jnp.int32)
    te = jnp.minimum(te, te[jnp.maximum(n_used - 1, 0)])
    return pos, te, n_used.reshape(1), n_tiles * tile


def kernel(x_prompt, x_sample, state_C, state_n, state_m, cache_ckv, cache_krope, page_table, g_norm_a, w_in_a, b_gate_a, g_head_a, w_out_a, g_kv, w_dkv, g_ckv, w_uk, w_uv, g_norm_b, w_dq, g_q, w_uq, w_o_b, g_ffn_d, w_gate_d, w_up_d, w_down_d, g_ffn_m, w_router, w_gate_m, w_up_m, w_down_m, g_final):
    B, T, D = x_prompt.shape
    DB, TS, _ = x_sample.shape
    H, DV = g_head_a.shape[1:]
    DK = state_C.shape[3]
    kv_lora, n_bheads, nope_dim = w_uk.shape
    rope_dim = cache_krope.shape[2]
    page = cache_ckv.shape[1]
    past_len = page_table.shape[1] * page
    n_experts = w_router.shape[2]
    assert state_C.shape[0] == 1 and g_norm_b.shape[0] == 1 and g_ffn_d.shape[0] == 1
    assert TS <= SAMPLE_PAD and 2 * rope_dim == LANES and 2 * H <= LANES
    TP = SAMPLE_PAD
    n_p, n_s = B * T, DB * TP
    n = n_p + n_s
    assert n_p % TOKEN_TILE == 0 and n_s % TOKEN_TILE == 0 and T % ATTN_TILE == 0

    x_p = x_prompt.reshape(n_p, D)
    x_s = jnp.pad(x_sample, ((0, 0), (0, TP - TS), (0, 0))).reshape(n_s, D)

    w_in = w_in_a[0].astype(BF16)
    hq, hv = H * DK, H * DV
    wq, wk, wv, wo = (w_in[:, :hq], w_in[:, hq:2 * hq], w_in[:, 2 * hq:2 * hq + hv],
                      w_in[:, 2 * hq + hv:2 * hq + 2 * hv])
    wgate = jnp.pad(w_in[:, 2 * hq + 2 * hv:], ((0, 0), (0, LANES - 2 * H)))
    bgate = jnp.pad(b_gate_a[0], (0, LANES - 2 * H)).reshape(1, LANES)
    q, k, v, o, gates = _in_proj(x_p, x_s, g_norm_a, wq, wk, wv, wo, wgate, bgate,
                                 n_heads=H, q_scale=DK ** -0.5)
    hg, c_p, n_pr, m_p = _mlstm(q, k, v, o, gates, g_head_a[0], row0=0, B=B, T=T,
                                L=MLSTM_CHUNK, t_valid=MLSTM_CHUNK)
    hg, c_s, n_sm, m_s = _mlstm(q, k, v, o, gates, g_head_a[0], row0=n_p, B=DB, T=TP, L=TP,
                                t_valid=TS, state=(state_C[0], state_n[0], state_m[0]), hg_prev=hg)

    f = w_gate_d.shape[2]
    f_pad = -(-f // FFN_CHUNK) * FFN_CHUNK
    nch = f_pad // FFN_CHUNK
    col_chunks = lambda w: jnp.pad(w.astype(BF16), ((0, 0), (0, f_pad - f))).reshape(
        D, nch, FFN_CHUNK).transpose(1, 0, 2)
    wd_d = jnp.pad(w_down_d[0].astype(BF16), ((0, f_pad - f), (0, 0))).reshape(nch, FFN_CHUNK, D)
    x2 = _ffn(x_p, x_s, hg, w_out_a[0].astype(BF16), g_ffn_d, col_chunks(w_gate_d[0]),
              col_chunks(w_up_d[0]), wd_d)

    half = rope_dim // 2
    inv = ROPE_THETA ** (-jnp.arange(half, dtype=F32) / half)

    def tables(pos):
        ang = pos.astype(F32)[:, None] * inv[None, :]
        return (jnp.tile(jnp.cos(ang), (1, LANES // half)), jnp.tile(jnp.sin(ang), (1, LANES // half)))

    cos_p, sin_p = tables(jnp.arange(T, dtype=jnp.int32))
    cos_s, sin_s = tables(past_len + jnp.arange(TP, dtype=jnp.int32))
    reps = ATTN_TILE // TP
    cos_tab = jnp.concatenate([cos_p, jnp.tile(cos_s, (reps, 1))], axis=0)
    sin_tab = jnp.concatenate([sin_p, jnp.tile(sin_s, (reps, 1))], axis=0)

    w_kr = w_dkv[:, kv_lora:]
    wdkv = jnp.concatenate([w_dkv[:, :kv_lora], w_kr, _rot_cols(w_kr, half)], axis=1).astype(BF16)
    wuq = w_uq[0].reshape(-1, n_bheads, nope_dim + rope_dim)
    w_nope = wuq[:, :, :nope_dim].reshape(-1, n_bheads * nope_dim).astype(BF16)
    w_rope = wuq[:, :, nope_dim:]
    w_r = w_rope.reshape(-1, n_bheads * rope_dim).astype(BF16)
    w_rr = _rot_cols(w_rope, half).reshape(-1, n_bheads * rope_dim).astype(BF16)
    w_ukt = jnp.transpose(w_uk, (1, 2, 0)).astype(BF16)
    lat_weights = [g_kv.reshape(1, D), wdkv, g_ckv.reshape(1, kv_lora), g_norm_b,
                   w_dq[0].astype(BF16), g_q, w_nope, w_r, w_rr, w_ukt]
    ckv_p, ckv_s, kr_p, kr_s, kcat, kvt, qcat = _latq(
        x2, cos_tab, sin_tab, n_p // ATTN_TILE, T // ATTN_TILE, lat_weights, n_heads=n_bheads,
        kv_lora=kv_lora, rope_dim=rope_dim, nope_dim=nope_dim)

    scale = (nope_dim + rope_dim) ** -0.5
    f_m = w_gate_m.shape[3]
    o_lat, (wg_m, wu_m, wd_m) = _attn_prompt(
        qcat, kcat, kvt, n,
        [w_gate_m[0].reshape(n_experts * D, f_m), w_up_m[0].reshape(n_experts * D, f_m),
         w_down_m[0].reshape(n_experts * f_m, D)],
        B=B, T=T, kv_lora=kv_lora, scale=scale)

    kw = kv_lora + 2 * rope_dim
    q_s = qcat[n_p // ATTN_TILE:].reshape(-1, n_bheads, ATTN_TILE // TP, TP, kw)[:, :, :, :TS]
    q_s = q_s.transpose(0, 2, 1, 3, 4).reshape(DB, n_bheads * TS, kw)
    q_s = jnp.concatenate([q_s[..., :kv_lora],
                           q_s[..., kv_lora:kv_lora + rope_dim] + q_s[..., kv_lora + rope_dim:]], axis=-1)
    k_new = jnp.pad(kcat[n_p:].reshape(DB, TP, kw), ((0, 0), (0, LANES - TP), (0, 0)))
    o_s = _attn_sample(page_table, q_s, k_new, cache_ckv, jnp.swapaxes(cache_krope, 1, 2),
                       t_valid=TS, scale=scale)
    o_s = o_s.reshape(DB, n_bheads, TS, kv_lora).transpose(0, 2, 1, 3)
    o_s = jnp.pad(o_s, ((0, 0), (0, TP - TS), (0, 0), (0, 0))).reshape(n_s, n_bheads * kv_lora)
    o_lat = lax.dynamic_update_slice(o_lat, o_s.astype(BF16), (n_p, 0))

    w_uvh = jnp.transpose(w_uv, (1, 0, 2)).astype(BF16)
    w_rt = jnp.pad(w_router[0], ((0, 0), (0, LANES - n_experts))).astype(BF16)
    x3, xn_m, ids, wts = _attn_out(o_lat, x2, w_uvh, w_o_b[0].astype(BF16), g_ffn_m, w_rt,
                                   n_experts=n_experts)

    top_k = 2
    pos, tile_expert, n_used, n_rows = _route(ids[:, :top_k].reshape(-1), n_experts, MOE_TILE)
    row_token = jnp.zeros((n_rows,), jnp.int32).at[pos].set(
        jnp.arange(n * top_k, dtype=jnp.int32) // top_k)
    ys = _moe(tile_expert, n_used, xn_m[row_token], wg_m.reshape(n_experts, D, f_m),
              wu_m.reshape(n_experts, D, f_m), wd_m.reshape(n_experts, f_m, D))
    pos = pos.reshape(n, top_k)
    y_p, y_s = _final(x3, ys[pos[:, 0]], ys[pos[:, 1]], wts, g_final.reshape(1, D), n_p)

    prompt = lambda a: a.reshape(B, T, a.shape[1])
    sample = lambda a: a.reshape(DB, TP, a.shape[1])[:, :TS]
    return (prompt(y_p), sample(y_s), c_p[None], n_pr[None], m_p[None], prompt(ckv_p), prompt(kr_p),
            c_s[None], n_sm[None], m_s[None], sample(ckv_s), sample(kr_s))
```

```python
import functools
import math

import jax
import jax.numpy as jnp
from jax import lax
from jax.experimental import pallas as pl
from jax.experimental.pallas import tpu as pltpu

F32 = jnp.float32
BF16 = jnp.bfloat16

EPS = 1e-6
GATE_CAP = 15.0
ROPE_THETA = 10000.0
NEG = -1e30
LANES = 128
VMEM_LIMIT_BYTES = 56 * 2**20

SAMPLE_PAD = 8
TOKEN_TILE = 512
ATTN_TILE = 512
MLSTM_CHUNK = 512
SHORT_SEQS_PER_STEP = 1
FFN_CHUNK = 1408
MOE_TILE = 512
MOE_CHUNK = 512
MOE_PARTS = 4
PAGES_PER_STEP = 64


def _cparams(*sem, flags=None):
    return pltpu.CompilerParams(dimension_semantics=sem, vmem_limit_bytes=VMEM_LIMIT_BYTES,
                                flags=flags)


def _dot(a, b):
    return jnp.dot(a, b, preferred_element_type=F32)


def _dot_nt(a, b):
    return lax.dot_general(a, b, (((1,), (1,)), ((), ())), preferred_element_type=F32)


def _rms(x, g):
    return x * lax.rsqrt(jnp.mean(x * x, axis=-1, keepdims=True) + EPS) * g


def _sigmoid(x):
    return 1.0 / (1.0 + jnp.exp(-x))


def _split3(x):
    hi = x.astype(BF16)
    r1 = x - hi.astype(F32)
    mid = r1.astype(BF16)
    lo = (r1 - mid.astype(F32)).astype(BF16)
    return hi, mid, lo


def _split_specs(tm, w, npb):
    return [pl.BlockSpec((tm, w), lambda i: (jnp.minimum(i, npb - 1), 0)),
            pl.BlockSpec((tm, w), lambda i: (jnp.maximum(i - npb, 0), 0))]


def _load_split(npb, p_ref, s_ref):
    return jnp.where(pl.program_id(0) < npb, p_ref[...], s_ref[...])


def _store_split(npb, p_ref, s_ref, val):
    i = pl.program_id(0)

    @pl.when(i < npb)
    def _():
        p_ref[...] = val

    @pl.when(i >= npb)
    def _():
        s_ref[...] = val


def _full(shape):
    nd = len(shape)
    return pl.BlockSpec(shape, lambda *_: (0,) * nd)


def _in_proj_kernel(xp_ref, xs_ref, g_ref, wq_ref, wk_ref, wv_ref, wo_ref, wg_ref, b_ref,
                    q_ref, k_ref, v_ref, o_ref, gate_ref, *, n_heads, q_scale, npb):
    xn = _rms(_load_split(npb, xp_ref, xs_ref), g_ref[...]).astype(BF16)
    q_ref[...] = _dot(xn, wq_ref[...]) * q_scale
    k_ref[...] = _dot(xn, wk_ref[...])
    v_ref[...] = _dot(xn, wv_ref[...])
    o_ref[...] = _dot(xn, wo_ref[...])
    g = _dot(xn, wg_ref[...]) + b_ref[...]
    g = GATE_CAP * jnp.tanh(g / GATE_CAP)
    logf = jnp.minimum(g, 0.0) - jnp.log1p(jnp.exp(-jnp.abs(g)))
    lane = lax.broadcasted_iota(jnp.int32, g.shape, 1)
    gate_ref[...] = jnp.where(lane < n_heads, g, jnp.where(lane < 2 * n_heads, logf, 0.0))


def _in_proj(xp, xs, g, wq, wk, wv, wo, wg, b, *, n_heads, q_scale):
    d = xp.shape[1]
    n = xp.shape[0] + xs.shape[0]
    tm = TOKEN_TILE
    npb = xp.shape[0] // tm
    row = lambda w: pl.BlockSpec((tm, w), lambda i: (i, 0))
    outs = [(wq.shape[1], F32), (wk.shape[1], F32), (wv.shape[1], F32), (wo.shape[1], F32), (LANES, F32)]
    return pl.pallas_call(
        functools.partial(_in_proj_kernel, n_heads=n_heads, q_scale=q_scale, npb=npb),
        grid=(n // tm,),
        in_specs=_split_specs(tm, d, npb) + [
            _full(g.shape), _full(wq.shape), _full(wk.shape), _full(wv.shape),
            _full(wo.shape), _full(wg.shape), _full(b.shape)],
        out_specs=[row(w) for w, _ in outs],
        out_shape=[jax.ShapeDtypeStruct((n, w), dt) for w, dt in outs],
        compiler_params=_cparams("parallel"),
        name="in_proj",
    )(xp, xs, g, wq, wk, wv, wo, wg, b)


def _mlstm_kernel(*refs, L, H, DK, DV, nb, t_valid, has_state, mm_dtype):
    if has_state:
        (q_ref, k_ref, v_ref, o_ref, gate_ref, gh_ref, c0_ref, n0_ref, m0_ref, _,
         hg_ref, cout_ref, nout_ref, mout_ref, caug_ref, m_scr) = refs
    else:
        (q_ref, k_ref, v_ref, o_ref, gate_ref, gh_ref,
         hg_ref, cout_ref, nout_ref, mout_ref, caug_ref, m_scr) = refs
    c = pl.program_id(1)
    last = pl.num_programs(1) - 1

    rk = lax.broadcasted_iota(jnp.int32, (DK, DK), 0)
    ck = lax.broadcasted_iota(jnp.int32, (DK, DK), 1)
    eye_k = rk == ck

    @pl.when(c == 0)
    def _():
        if has_state:
            for sh in range(nb * H):
                s, h = divmod(sh, H)
                caug_ref[sh, :, :DV] = c0_ref[s, h]
                n_row = n0_ref[s, h:h + 1, :]
                n_col = jnp.sum(jnp.where(eye_k, jnp.broadcast_to(n_row, (DK, DK)), 0.0),
                                axis=1, keepdims=True)
                caug_ref[sh, :, DV:] = jnp.broadcast_to(n_col, (DK, DV))
                m_scr[sh:sh + 1, :] = jnp.broadcast_to(m0_ref[s, :, h:h + 1], (1, LANES))
        else:
            caug_ref[...] = jnp.zeros_like(caug_ref)
            m_scr[...] = jnp.zeros_like(m_scr)

    ri = lax.broadcasted_iota(jnp.int32, (L, L), 0)
    ci = lax.broadcasted_iota(jnp.int32, (L, L), 1)
    causal = ci <= ri
    eye = ci == ri
    ones_v = jnp.ones((L, DV), mm_dtype)
    use_mxu_cumsum = L % LANES == 0

    def head(s, h, gates, cums):
        rows = slice(s * L, (s + 1) * L)
        sh = s * H + h
        if use_mxu_cumsum:
            gates_t, cum, cum_t = cums
            a_col = cum[:, H + h:H + h + 1]
            b_row = gates_t[h:h + 1, :] - cum_t[H + h:H + h + 1, :]
        else:
            li_col = gates[:, h:h + 1]
            lf_col = gates[:, H + h:H + h + 1]
            lf_b = jnp.broadcast_to(lf_col, (L, L))
            lf_row = jnp.sum(jnp.where(eye, lf_b, 0.0), axis=0, keepdims=True)
            a_col = jnp.sum(jnp.where(causal, jnp.broadcast_to(lf_row, (L, L)), 0.0),
                            axis=1, keepdims=True)
            a_row = jnp.sum(jnp.where(ci >= ri, lf_b, 0.0), axis=0, keepdims=True)
            li_row = jnp.sum(jnp.where(eye, jnp.broadcast_to(li_col, (L, L)), 0.0),
                             axis=0, keepdims=True)
            b_row = li_row - a_row
        d = jnp.where(causal, a_col + b_row, NEG)
        m_prev = m_scr[sh:sh + 1, 0:1]
        inter = a_col + m_prev
        m_t = jnp.maximum(inter, jnp.max(d, axis=1, keepdims=True))

        qh = q_ref[rows, h * DK:(h + 1) * DK].astype(mm_dtype)
        kh = k_ref[rows, h * DK:(h + 1) * DK].astype(mm_dtype)
        vaug = jnp.concatenate([v_ref[rows, h * DV:(h + 1) * DV].astype(mm_dtype), ones_v], axis=1)
        caug = caug_ref[sh]

        w = jnp.exp(d - m_t) * _dot_nt(qh, kh)
        e_inter = jnp.exp(inter - m_t)
        num = _dot(w.astype(mm_dtype), vaug) + e_inter * _dot(qh, caug.astype(mm_dtype))
        hh = num[:, :DV] / jnp.maximum(jnp.abs(num[:, DV:]), jnp.exp(-m_t))
        hn = _rms(hh, gh_ref[h:h + 1, :])
        hg_ref[rows, h * DV:(h + 1) * DV] = _sigmoid(o_ref[rows, h * DV:(h + 1) * DV]) * hn

        m_new = m_t[L - 1:L, :]
        a_last = a_col[L - 1:L, :]
        e_end = jnp.exp(a_last + b_row - m_new)
        e_carry = jnp.exp(a_last + m_prev - m_new)
        k_t = _dot_nt(eye_k.astype(mm_dtype), kh)
        caug_new = e_carry * caug + _dot((k_t * e_end).astype(mm_dtype), vaug)
        caug_ref[sh] = caug_new
        m_scr[sh:sh + 1, :] = jnp.broadcast_to(m_new, (1, LANES))

        @pl.when(c == last)
        def _():
            cout_ref[s, h] = caug_new[:, :DV]
            nout_ref[s, h:h + 1, :] = jnp.sum(jnp.where(eye_k, caug_new[:, DV:DV + DK], 0.0),
                                              axis=0, keepdims=True)
            mout_ref[s, :, h:h + 1] = m_new

    for s in range(nb):
        gates = gate_ref[s * L:(s + 1) * L, :]
        if t_valid < L:
            t_id = lax.broadcasted_iota(jnp.int32, gates.shape, 0)
            lane = lax.broadcasted_iota(jnp.int32, gates.shape, 1)
            gates = jnp.where(t_id < t_valid, gates, jnp.where(lane < H, NEG, 0.0))
        cums = None
        if use_mxu_cumsum:
            gates_t = gates.T
            cum = sum(_dot(causal.astype(BF16), p) for p in _split3(gates))
            cum_t = sum(_dot(p, (ri <= ci).astype(BF16)) for p in _split3(gates_t))
            cums = (gates_t, cum, cum_t)
        for h in range(H):
            head(s, h, gates, cums)


def _mlstm(q, k, v, o, gates, g_head, *, row0, B, T, L, t_valid, state=None, hg_prev=None):
    n = q.shape[0]
    H, DV = g_head.shape
    DK = q.shape[1] // H
    nc = T // L
    nb = SHORT_SEQS_PER_STEP if nc == 1 else 1
    assert B % nb == 0 and row0 % (nb * L) == 0
    blk0 = row0 // (nb * L)
    row = lambda w: pl.BlockSpec((nb * L, w), lambda b, c: (blk0 + b * nc + c, 0))
    c_spec = pl.BlockSpec((nb, H, DK, DV), lambda b, c: (b, 0, 0, 0))
    n_spec = pl.BlockSpec((nb, H, DK), lambda b, c: (b, 0, 0))
    m_spec = pl.BlockSpec((nb, 1, H), lambda b, c: (b, 0, 0))
    in_specs = [row(H * DK), row(H * DK), row(H * DV), row(H * DV), row(LANES), _full(g_head.shape)]
    args = [q, k, v, o, gates, g_head]
    aliases = {}
    if state is not None:
        c0, n0, m0 = state
        in_specs += [c_spec, n_spec, m_spec, pl.BlockSpec(memory_space=pl.ANY)]
        args += [c0, n0, m0.reshape(B, 1, H), hg_prev]
        aliases = {len(args) - 1: 0}
    out_shape = [jax.ShapeDtypeStruct((n, H * DV), F32),
                 jax.ShapeDtypeStruct((B, H, DK, DV), F32),
                 jax.ShapeDtypeStruct((B, H, DK), F32),
                 jax.ShapeDtypeStruct((B, 1, H), F32)]
    out_specs = [row(H * DV), c_spec, n_spec, m_spec]
    kern = functools.partial(_mlstm_kernel, L=L, H=H, DK=DK, DV=DV, nb=nb, t_valid=t_valid,
                             has_state=state is not None,
                             mm_dtype=BF16 if L % 16 == 0 else F32)
    hg, c_out, n_out, m_out = pl.pallas_call(
        kern, grid=(B // nb, nc), in_specs=in_specs, out_specs=out_specs, out_shape=out_shape,
        scratch_shapes=[pltpu.VMEM((nb * H, DK, 2 * DV), F32), pltpu.VMEM((nb * H, LANES), F32)],
        input_output_aliases=aliases,
        compiler_params=_cparams("parallel", "arbitrary"),
        name="mlstm_sample" if state is not None else "mlstm_prompt",
    )(*args)
    return hg, c_out, n_out, m_out.reshape(B, H)


def _ffn_kernel(xp_ref, xs_ref, hg_ref, wout_ref, g_ref, wg_ref, wu_ref, wd_ref, out_ref, acc_ref,
                *, npb):
    x1 = _load_split(npb, xp_ref, xs_ref) + _dot(hg_ref[...].astype(BF16), wout_ref[...])
    xn = _rms(x1, g_ref[...]).astype(BF16)
    acc_ref[...] = jnp.zeros_like(acc_ref)

    def body(c, carry):
        gate = _dot(xn, wg_ref[c])
        up = _dot(xn, wu_ref[c])
        hmid = (gate * _sigmoid(gate) * up).astype(BF16)
        acc_ref[...] += _dot(hmid, wd_ref[c])
        return carry

    lax.fori_loop(0, wg_ref.shape[0], body, 0)
    out_ref[...] = x1 + acc_ref[...]


def _ffn(xp, xs, hg, wout, g, wg, wu, wd):
    n, d = hg.shape
    tm = TOKEN_TILE
    npb = xp.shape[0] // tm
    row = pl.BlockSpec((tm, d), lambda i: (i, 0))
    return pl.pallas_call(
        functools.partial(_ffn_kernel, npb=npb), grid=(n // tm,),
        in_specs=_split_specs(tm, d, npb) + [
            row, _full(wout.shape), _full(g.shape), _full(wg.shape), _full(wu.shape),
            _full(wd.shape)],
        out_specs=row, out_shape=jax.ShapeDtypeStruct((n, d), F32),
        scratch_shapes=[pltpu.VMEM((tm, d), F32)],
        compiler_params=_cparams("parallel"),
        name="outproj_ffn",
    )(xp, xs, hg, wout, g, wg, wu, wd)


def _latq_kernel(x_ref, cos_ref, sin_ref, gkv_ref, wdkv_ref, gckv_ref, gnb_ref, wdq_ref, gq_ref,
                 wn_ref, wr_ref, wrr_ref, wuk_ref,
                 ckvp_ref, ckvs_ref, krp_ref, krs_ref, kcat_ref, kvt_ref, q_ref,
                 *, n_heads, kv_lora, rope_dim, nope_dim, npb):
    x = x_ref[...]
    xs = x * lax.rsqrt(jnp.mean(x * x, axis=-1, keepdims=True) + EPS)
    cos = cos_ref[...]
    sin = sin_ref[...]
    lane = lax.broadcasted_iota(jnp.int32, cos.shape, 1)
    lo = lane < rope_dim

    lat = _dot((xs * gkv_ref[...]).astype(BF16), wdkv_ref[...])
    ckv = _rms(lat[:, :kv_lora], gckv_ref[...])
    _store_split(npb, ckvp_ref, ckvs_ref, ckv)
    t = lat[:, kv_lora:] * jnp.where(lo, cos, sin)
    kr2 = t + pltpu.roll(t, rope_dim, axis=1)
    _store_split(npb, krp_ref, krs_ref, kr2[:, :rope_dim])
    kcat_ref[...] = jnp.concatenate([ckv, kr2], axis=1).astype(BF16)
    kvt_ref[0] = ckv.T.astype(BF16)

    cq = _dot((xs * gnb_ref[...]).astype(BF16), wdq_ref[...])
    cqn = _rms(cq, gq_ref[...]).astype(BF16)
    qn = _dot(cqn, wn_ref[...]).astype(BF16)
    reps = n_heads * rope_dim // LANES
    cos_h = jnp.concatenate([cos] * reps, axis=1)
    sin_h = jnp.concatenate([sin] * reps, axis=1)
    qr = _dot(cqn, wr_ref[...]) * cos_h + _dot(cqn, wrr_ref[...]) * sin_h
    for h in range(n_heads):
        ql = _dot(qn[:, h * nope_dim:(h + 1) * nope_dim], wuk_ref[h])
        pair = qr[:, (h // 2) * LANES:(h // 2 + 1) * LANES]
        slot = jnp.where(lo if h % 2 == 0 else jnp.logical_not(lo), pair, 0.0)
        q_ref[0, h] = jnp.concatenate([ql, slot], axis=1).astype(BF16)


def _latq(x, cos_tab, sin_tab, n_prompt_blocks, blocks_per_seq, weights, *, n_heads, kv_lora,
          rope_dim, nope_dim):
    n, d = x.shape
    tm = ATTN_TILE
    kw = kv_lora + 2 * rope_dim

    def tab_map(i):
        return (jnp.where(i < n_prompt_blocks, i % blocks_per_seq, blocks_per_seq), 0)

    tab = pl.BlockSpec((tm, LANES), tab_map)
    row = lambda w: pl.BlockSpec((tm, w), lambda i: (i, 0))
    npb = n_prompt_blocks
    n_p, n_s = npb * tm, n - npb * tm
    return pl.pallas_call(
        functools.partial(_latq_kernel, n_heads=n_heads, kv_lora=kv_lora, rope_dim=rope_dim,
                          nope_dim=nope_dim, npb=npb),
        grid=(n // tm,),
        in_specs=[row(d), tab, tab] + [_full(w.shape) for w in weights],
        out_specs=_split_specs(tm, kv_lora, npb) + _split_specs(tm, rope_dim, npb) + [
            row(kw),
            pl.BlockSpec((1, kv_lora, tm), lambda i: (i, 0, 0)),
            pl.BlockSpec((1, n_heads, tm, kw), lambda i: (i, 0, 0, 0))],
        out_shape=[jax.ShapeDtypeStruct((n_p, kv_lora), F32),
                   jax.ShapeDtypeStruct((n_s, kv_lora), F32),
                   jax.ShapeDtypeStruct((n_p, rope_dim), F32),
                   jax.ShapeDtypeStruct((n_s, rope_dim), F32),
                   jax.ShapeDtypeStruct((n, kw), BF16),
                   jax.ShapeDtypeStruct((n // tm, kv_lora, tm), BF16),
                   jax.ShapeDtypeStruct((n // tm, n_heads, tm, kw), BF16)],
        compiler_params=_cparams("arbitrary"),
        name="latent_q",
    )(x, cos_tab, sin_tab, *weights)


def _attn_prompt_kernel(q_ref, k_ref, kt_ref, *refs, tq, kv_lora, scale, n_cast):
    cast_in = refs[:n_cast]
    o_ref = refs[n_cast]
    cast_out = refs[n_cast + 1:2 * n_cast + 1]
    m_scr, l_scr, acc_scr = refs[2 * n_cast + 1:]
    for src, dst in zip(cast_in, cast_out):
        dst[...] = src[...].astype(dst.dtype)

    qi = pl.program_id(1)
    n_heads = q_ref.shape[1]
    m_scr[...] = jnp.full_like(m_scr, NEG)
    l_scr[...] = jnp.zeros_like(l_scr)
    acc_scr[...] = jnp.zeros_like(acc_scr)

    def step(j, masked):
        kc = k_ref[pl.ds(pl.multiple_of(j * tq, tq), tq), :]
        kt = kt_ref[j]
        if masked:
            key = lax.broadcasted_iota(jnp.int32, (tq, tq), 0)
            qry = lax.broadcasted_iota(jnp.int32, (tq, tq), 1)
            keep = key <= qry
        for h in range(n_heads):
            st = _dot_nt(kc, q_ref[0, h]) * scale
            if masked:
                st = jnp.where(keep, st, NEG)
            m_prev = m_scr[h]
            m_new = jnp.maximum(m_prev, jnp.max(st, axis=0, keepdims=True))
            p = jnp.exp(st - m_new)
            alpha = jnp.exp(m_prev - m_new)
            l_scr[h] = alpha * l_scr[h] + jnp.sum(p, axis=0, keepdims=True)
            acc_scr[h] = alpha * acc_scr[h] + _dot(kt, p.astype(BF16))
            m_scr[h] = m_new

    def body(j, carry):
        step(j, False)
        return carry

    lax.fori_loop(0, qi, body, 0)
    step(qi, True)
    for h in range(n_heads):
        o_ref[:, h * kv_lora:(h + 1) * kv_lora] = (acc_scr[h] / l_scr[h]).T.astype(BF16)


def _attn_prompt(q, kcat, kvt, n_rows, to_cast, *, B, T, kv_lora, scale):
    tq = ATTN_TILE
    _, n_heads, _, kw = q.shape
    nq = T // tq
    steps = B * nq
    cast_specs = []
    for w in to_cast:
        rows = w.shape[0] // steps
        assert rows * steps == w.shape[0] and rows % 16 == 0
        cast_specs.append(pl.BlockSpec((rows, w.shape[1]), lambda b, i: (b * nq + i, 0)))
    out = pl.pallas_call(
        functools.partial(_attn_prompt_kernel, tq=tq, kv_lora=kv_lora, scale=scale,
                          n_cast=len(to_cast)),
        grid=(B, nq),
        in_specs=[pl.BlockSpec((1, n_heads, tq, kw), lambda b, i: (b * nq + i, 0, 0, 0)),
                  pl.BlockSpec((T, kw), lambda b, i: (b, 0)),
                  pl.BlockSpec((nq, kv_lora, tq), lambda b, i: (b, 0, 0))] + cast_specs,
        out_specs=[pl.BlockSpec((tq, n_heads * kv_lora), lambda b, i: (b * nq + i, 0))] + cast_specs,
        out_shape=[jax.ShapeDtypeStruct((n_rows, n_heads * kv_lora), BF16)]
        + [jax.ShapeDtypeStruct(w.shape, BF16) for w in to_cast],
        scratch_shapes=[pltpu.VMEM((n_heads, 1, tq), F32), pltpu.VMEM((n_heads, 1, tq), F32),
                        pltpu.VMEM((n_heads, kv_lora, tq), F32)],
        compiler_params=_cparams("parallel", "arbitrary"),
        name="attn_prompt",
    )(q, kcat, kvt, *to_cast)
    return out[0], out[1:]


def _attn_sample_kernel(pt_ref, q_ref, knew_ref, *refs, n_pages, t_valid, kv_lora, rope_dim, scale):
    ck_refs = refs[:n_pages]
    kr_refs = refs[n_pages:2 * n_pages]
    o_ref, m_scr, l_scr, acc_scr = refs[2 * n_pages:]
    g = pl.program_id(1)

    @pl.when(g == 0)
    def _():
        m_scr[...] = jnp.full_like(m_scr, NEG)
        l_scr[...] = jnp.zeros_like(l_scr)
        acc_scr[...] = jnp.zeros_like(acc_scr)

    q = q_ref[0]
    ql = q[:, :kv_lora]
    qr = q[:, kv_lora:]

    def update(s, values):
        m_prev = m_scr[...]
        m_new = jnp.maximum(m_prev, jnp.max(s, axis=1, keepdims=True))
        p = jnp.exp(s - m_new)
        alpha = jnp.exp(m_prev - m_new)
        l_scr[...] = alpha * l_scr[...] + jnp.sum(p, axis=1, keepdims=True)
        p = p.astype(BF16)
        pv = _dot(p[:, :values[0].shape[0]], values[0])
        for i in range(1, len(values)):
            rows = values[i].shape[0]
            pv += _dot(p[:, i * rows:(i + 1) * rows], values[i])
        acc_scr[...] = alpha * acc_scr[...] + pv
        m_scr[...] = m_new

    cks = [r[0].astype(BF16) for r in ck_refs]
    s = jnp.concatenate(
        [_dot_nt(ql, ck) + _dot(qr, kr[0].astype(BF16)) for ck, kr in zip(cks, kr_refs)],
        axis=1) * scale
    update(s, cks)

    @pl.when(g == pl.num_programs(1) - 1)
    def _():
        kn = knew_ref[0]
        ckn = kn[:, :kv_lora]
        sn = (_dot_nt(ql, ckn) + _dot_nt(qr, kn[:, kv_lora:kv_lora + rope_dim])) * scale
        t = lax.broadcasted_iota(jnp.int32, sn.shape, 0) % t_valid
        j = lax.broadcasted_iota(jnp.int32, sn.shape, 1)
        update(jnp.where(j <= t, sn, NEG), [ckn])
        o_ref[0] = acc_scr[...] / l_scr[...]


def _attn_sample(page_table, q, knew, cache_ckv, cache_krope_t, *, t_valid, scale):
    nb, n_pages = page_table.shape
    _, rows, qw = q.shape
    _, page, kv_lora = cache_ckv.shape
    rope_dim = cache_krope_t.shape[1]
    pg = min(PAGES_PER_STEP, n_pages)
    assert n_pages % pg == 0
    ck_specs = [pl.BlockSpec((1, page, kv_lora),
                             functools.partial(lambda b, g, pt, i: (pt[b, g * pg + i], 0, 0), i=i))
                for i in range(pg)]
    kr_specs = [pl.BlockSpec((1, rope_dim, page),
                             functools.partial(lambda b, g, pt, i: (pt[b, g * pg + i], 0, 0), i=i))
                for i in range(pg)]
    grid_spec = pltpu.PrefetchScalarGridSpec(
        num_scalar_prefetch=1, grid=(nb, n_pages // pg),
        in_specs=[pl.BlockSpec((1, rows, qw), lambda b, g, pt: (b, 0, 0)),
                  pl.BlockSpec((1,) + knew.shape[1:], lambda b, g, pt: (b, 0, 0))]
        + ck_specs + kr_specs,
        out_specs=pl.BlockSpec((1, rows, kv_lora), lambda b, g, pt: (b, 0, 0)),
        scratch_shapes=[pltpu.VMEM((rows, 1), F32), pltpu.VMEM((rows, 1), F32),
                        pltpu.VMEM((rows, kv_lora), F32)])
    return pl.pallas_call(
        functools.partial(_attn_sample_kernel, n_pages=pg, t_valid=t_valid, kv_lora=kv_lora,
                          rope_dim=rope_dim, scale=scale),
        grid_spec=grid_spec,
        out_shape=jax.ShapeDtypeStruct((nb, rows, kv_lora), F32),
        compiler_params=_cparams("parallel", "arbitrary"),
        name="attn_sample",
    )(page_table, q, knew, *([cache_ckv] * pg), *([cache_krope_t] * pg))


def _attn_out_kernel(ol_ref, x_ref, wuv_ref, wo_ref, g_ref, wr_ref,
                     x3_ref, xn_ref, ids_ref, wts_ref, *, n_experts):
    n_heads, kv_lora, _ = wuv_ref.shape
    o = jnp.concatenate(
        [_dot(ol_ref[:, h * kv_lora:(h + 1) * kv_lora], wuv_ref[h]) for h in range(n_heads)],
        axis=1).astype(BF16)
    x3 = x_ref[...] + _dot(o, wo_ref[...])
    x3_ref[...] = x3
    xn = _rms(x3, g_ref[...]).astype(BF16)
    xn_ref[...] = xn
    logits = _dot(xn, wr_ref[...])
    lane = lax.broadcasted_iota(jnp.int32, logits.shape, 1)
    lane_f = lane.astype(F32)
    lg = jnp.where(lane < n_experts, logits, -jnp.inf)
    v1 = jnp.max(lg, axis=1, keepdims=True)
    i1 = jnp.min(jnp.where(lg == v1, lane_f, float(LANES)), axis=1, keepdims=True)
    lg2 = jnp.where(lane_f == i1, -jnp.inf, lg)
    v2 = jnp.max(lg2, axis=1, keepdims=True)
    i2 = jnp.min(jnp.where(lg2 == v2, lane_f, float(LANES)), axis=1, keepdims=True)
    e = jnp.exp(v2 - v1)
    w1 = 1.0 / (1.0 + e)
    w2 = e / (1.0 + e)
    ids_ref[...] = jnp.where(lane == 0, i1, jnp.where(lane == 1, i2, 0.0)).astype(jnp.int32)
    wts_ref[...] = jnp.where(lane == 0, w1, jnp.where(lane == 1, w2, 0.0))


def _attn_out(o_lat, x, wuv, wo, g, wr, *, n_experts):
    n, d = x.shape
    tm = TOKEN_TILE
    row = lambda w: pl.BlockSpec((tm, w), lambda i: (i, 0))
    return pl.pallas_call(
        functools.partial(_attn_out_kernel, n_experts=n_experts),
        grid=(n // tm,),
        in_specs=[row(o_lat.shape[1]), row(d), _full(wuv.shape), _full(wo.shape), _full(g.shape),
                  _full(wr.shape)],
        out_specs=[row(d), row(d), row(LANES), row(LANES)],
        out_shape=[jax.ShapeDtypeStruct((n, d), F32), jax.ShapeDtypeStruct((n, d), BF16),
                   jax.ShapeDtypeStruct((n, LANES), jnp.int32),
                   jax.ShapeDtypeStruct((n, LANES), F32)],
        compiler_params=_cparams("parallel"),
        name="attn_out_router",
    )(o_lat, x, wuv, wo, g, wr)


def _moe_kernel(pt_ref, pe_ref, lo_ref, hi_ref, first_ref, np_ref, xs_ref, wg_ref, wu_ref, wd_ref,
                *refs, chunk):
    out_ref = refs[-1]
    i = pl.program_id(0)

    @pl.when(i < np_ref[0])
    def _():
        xs = xs_ref[...]
        acc = jnp.zeros(out_ref.shape, F32)
        for c in range(wg_ref.shape[2] // chunk):
            sl = slice(c * chunk, (c + 1) * chunk)
            gate = _dot(xs, wg_ref[0, :, sl])
            up = _dot(xs, wu_ref[0, :, sl])
            acc += _dot((gate * _sigmoid(gate) * up).astype(BF16), wd_ref[0, sl, :])
        res = acc.astype(out_ref.dtype)
        row = lax.broadcasted_iota(jnp.int32, out_ref.shape, 0)
        lo, hi = lo_ref[i], hi_ref[i]

        def keep_rows(base):
            return jnp.where(row >= lo, jnp.where(row < hi, res, base), base)

        @pl.when(first_ref[i] == 1)
        def _():
            out_ref[...] = keep_rows(jnp.zeros_like(res))

        @pl.when(first_ref[i] == 0)
        def _():
            out_ref[...] = keep_rows(out_ref[...])


def _moe(pairs, xs, wg, wu, wd, out_prev, t0, n_rows):
    d = xs.shape[1]
    tm = MOE_TILE
    max_pairs = pairs[0].shape[0]
    expert = lambda i, pt, pe, lo, hi, first, n: (pe[i], 0, 0)
    w_specs = [pl.BlockSpec((1,) + w.shape[1:], expert, pipeline_mode=pl.Buffered(1))
               for w in (wg, wu, wd)]
    args = [xs, wg, wu, wd]
    in_specs = [pl.BlockSpec((tm, d), lambda i, pt, pe, lo, hi, first, n: (pt[i], 0))] + w_specs
    aliases = {}
    if out_prev is not None:
        in_specs.append(pl.BlockSpec(memory_space=pl.ANY))
        args.append(out_prev)
        aliases = {len(pairs) + len(args) - 1: 0}
    grid_spec = pltpu.PrefetchScalarGridSpec(
        num_scalar_prefetch=len(pairs), grid=(max_pairs,), in_specs=in_specs,
        out_specs=pl.BlockSpec((tm, d), lambda i, pt, pe, lo, hi, first, n: (t0 + pt[i], 0)))
    return pl.pallas_call(
        functools.partial(_moe_kernel, chunk=MOE_CHUNK),
        grid_spec=grid_spec,
        out_shape=jax.ShapeDtypeStruct((n_rows, d), BF16),
        input_output_aliases=aliases,
        compiler_params=_cparams("arbitrary"),
        name="moe_experts",
    )(*pairs, *args)


def _final_kernel(x_ref, y1_ref, y2_ref, wts_ref, g_ref, outp_ref, outs_ref, *, npb):
    w = wts_ref[...]
    x4 = x_ref[...] + (w[:, 0:1] * y1_ref[...].astype(F32) + w[:, 1:2] * y2_ref[...].astype(F32))
    _store_split(npb, outp_ref, outs_ref, _rms(x4, g_ref[...]))


def _final(x, y1, y2, wts, g, n_p):
    n, d = x.shape
    tm = TOKEN_TILE
    npb = n_p // tm
    row = lambda w: pl.BlockSpec((tm, w), lambda i: (i, 0))
    return pl.pallas_call(
        functools.partial(_final_kernel, npb=npb), grid=(n // tm,),
        in_specs=[row(d), row(d), row(d), row(LANES), _full(g.shape)],
        out_specs=_split_specs(tm, d, npb),
        out_shape=[jax.ShapeDtypeStruct((n_p, d), F32), jax.ShapeDtypeStruct((n - n_p, d), F32)],
        compiler_params=_cparams("arbitrary"),
        name="combine_final",
    )(x, y1, y2, wts, g)


def _rot_cols(w, half):
    return jnp.concatenate([-w[..., half:], w[..., :half]], axis=-1)


def _route(ids, n_experts):
    n2 = ids.shape[0]
    shift = (n2 - 1).bit_length()
    keys = (ids << shift) | jnp.arange(n2, dtype=jnp.int32)
    order = jnp.sort(keys) & ((1 << shift) - 1)
    onehot = (ids[:, None] == jnp.arange(n_experts, dtype=jnp.int32)[None, :]).astype(jnp.int32)
    rank = jnp.take_along_axis(jnp.cumsum(onehot, axis=0) - onehot, ids[:, None], axis=1)[:, 0]
    ends = jnp.cumsum(jnp.sum(onehot, axis=0))
    starts = ends - jnp.sum(onehot, axis=0)
    return order, starts[ids] + rank, starts, ends


def _tile_expert_pairs(starts, ends, t0, n_tiles, tile, max_pairs):
    n_experts = starts.shape[0]
    base = (t0 + jnp.arange(n_tiles, dtype=jnp.int32))[:, None] * tile
    lo = jnp.maximum(starts[None, :] - base, 0).reshape(-1)
    hi = jnp.minimum(ends[None, :] - base, tile).reshape(-1)
    hit = hi > lo
    n = jnp.sum(hit).astype(jnp.int32)
    idx = jnp.nonzero(hit, size=max_pairs, fill_value=0)[0].astype(jnp.int32)
    idx = jnp.where(jnp.arange(max_pairs) < n, idx, idx[n - 1])
    p_tile = idx // n_experts
    first = jnp.concatenate([jnp.ones((1,), jnp.int32),
                             (p_tile[1:] != p_tile[:-1]).astype(jnp.int32)])
    return p_tile, idx % n_experts, lo[idx], hi[idx], first, n.reshape(1)


def kernel(x_prompt, x_sample, state_C, state_n, state_m, cache_ckv, cache_krope, page_table, g_norm_a, w_in_a, b_gate_a, g_head_a, w_out_a, g_kv, w_dkv, g_ckv, w_uk, w_uv, g_norm_b, w_dq, g_q, w_uq, w_o_b, g_ffn_d, w_gate_d, w_up_d, w_down_d, g_ffn_m, w_router, w_gate_m, w_up_m, w_down_m, g_final):
    B, T, D = x_prompt.shape
    DB, TS, _ = x_sample.shape
    H, DV = g_head_a.shape[1:]
    DK = state_C.shape[3]
    kv_lora, n_bheads, nope_dim = w_uk.shape
    rope_dim = cache_krope.shape[2]
    page = cache_ckv.shape[1]
    past_len = page_table.shape[1] * page
    n_experts = w_router.shape[2]
    assert state_C.shape[0] == 1 and g_norm_b.shape[0] == 1 and g_ffn_d.shape[0] == 1
    assert TS <= SAMPLE_PAD and 2 * rope_dim == LANES and 2 * H <= LANES
    TP = SAMPLE_PAD
    n_p, n_s = B * T, DB * TP
    n = n_p + n_s
    assert n_p % TOKEN_TILE == 0 and n_s % TOKEN_TILE == 0 and T % ATTN_TILE == 0

    x_p = x_prompt.reshape(n_p, D)
    x_s = jnp.pad(x_sample, ((0, 0), (0, TP - TS), (0, 0))).reshape(n_s, D)

    w_in = w_in_a[0].astype(BF16)
    hq, hv = H * DK, H * DV
    wq, wk, wv, wo = (w_in[:, :hq], w_in[:, hq:2 * hq], w_in[:, 2 * hq:2 * hq + hv],
                      w_in[:, 2 * hq + hv:2 * hq + 2 * hv])
    wgate = jnp.pad(w_in[:, 2 * hq + 2 * hv:], ((0, 0), (0, LANES - 2 * H)))
    bgate = jnp.pad(b_gate_a[0], (0, LANES - 2 * H)).reshape(1, LANES)
    q, k, v, o, gates = _in_proj(x_p, x_s, g_norm_a, wq, wk, wv, wo, wgate, bgate,
                                 n_heads=H, q_scale=DK ** -0.5)
    hg, c_p, n_pr, m_p = _mlstm(q, k, v, o, gates, g_head_a[0], row0=0, B=B, T=T,
                                L=MLSTM_CHUNK, t_valid=MLSTM_CHUNK)
    hg, c_s, n_sm, m_s = _mlstm(q, k, v, o, gates, g_head_a[0], row0=n_p, B=DB, T=TP, L=TP,
                                t_valid=TS, state=(state_C[0], state_n[0], state_m[0]), hg_prev=hg)

    f = w_gate_d.shape[2]
    f_pad = -(-f // FFN_CHUNK) * FFN_CHUNK
    nch = f_pad // FFN_CHUNK
    col_chunks = lambda w: jnp.pad(w.astype(BF16), ((0, 0), (0, f_pad - f))).reshape(
        D, nch, FFN_CHUNK).transpose(1, 0, 2)
    wd_d = jnp.pad(w_down_d[0].astype(BF16), ((0, f_pad - f), (0, 0))).reshape(nch, FFN_CHUNK, D)
    x2 = _ffn(x_p, x_s, hg, w_out_a[0].astype(BF16), g_ffn_d, col_chunks(w_gate_d[0]),
              col_chunks(w_up_d[0]), wd_d)

    half = rope_dim // 2
    inv = ROPE_THETA ** (-jnp.arange(half, dtype=F32) / half)

    def tables(pos):
        ang = pos.astype(F32)[:, None] * inv[None, :]
        return (jnp.tile(jnp.cos(ang), (1, LANES // half)), jnp.tile(jnp.sin(ang), (1, LANES // half)))

    cos_p, sin_p = tables(jnp.arange(T, dtype=jnp.int32))
    cos_s, sin_s = tables(past_len + jnp.arange(TP, dtype=jnp.int32))
    reps = ATTN_TILE // TP
    cos_tab = jnp.concatenate([cos_p, jnp.tile(cos_s, (reps, 1))], axis=0)
    sin_tab = jnp.concatenate([sin_p, jnp.tile(sin_s, (reps, 1))], axis=0)

    w_kr = w_dkv[:, kv_lora:]
    wdkv = jnp.concatenate([w_dkv[:, :kv_lora], w_kr, _rot_cols(w_kr, half)], axis=1).astype(BF16)
    wuq = w_uq[0].reshape(-1, n_bheads, nope_dim + rope_dim)
    w_nope = wuq[:, :, :nope_dim].reshape(-1, n_bheads * nope_dim).astype(BF16)
    w_rope = wuq[:, :, nope_dim:]
    w_r = w_rope.reshape(-1, n_bheads * rope_dim).astype(BF16)
    w_rr = _rot_cols(w_rope, half).reshape(-1, n_bheads * rope_dim).astype(BF16)
    w_ukt = jnp.transpose(w_uk, (1, 2, 0)).astype(BF16)
    lat_weights = [g_kv.reshape(1, D), wdkv, g_ckv.reshape(1, kv_lora), g_norm_b,
                   w_dq[0].astype(BF16), g_q, w_nope, w_r, w_rr, w_ukt]
    ckv_p, ckv_s, kr_p, kr_s, kcat, kvt, qcat = _latq(
        x2, cos_tab, sin_tab, n_p // ATTN_TILE, T // ATTN_TILE, lat_weights, n_heads=n_bheads,
        kv_lora=kv_lora, rope_dim=rope_dim, nope_dim=nope_dim)

    scale = (nope_dim + rope_dim) ** -0.5
    f_m = w_gate_m.shape[3]
    o_lat, (wg_m, wu_m, wd_m) = _attn_prompt(
        qcat, kcat, kvt, n,
        [w_gate_m[0].reshape(n_experts * D, f_m), w_up_m[0].reshape(n_experts * D, f_m),
         w_down_m[0].reshape(n_experts * f_m, D)],
        B=B, T=T, kv_lora=kv_lora, scale=scale)

    kw = kv_lora + 2 * rope_dim
    q_s = qcat[n_p // ATTN_TILE:].reshape(-1, n_bheads, ATTN_TILE // TP, TP, kw)[:, :, :, :TS]
    q_s = q_s.transpose(0, 2, 1, 3, 4).reshape(DB, n_bheads * TS, kw)
    q_s = jnp.concatenate([q_s[..., :kv_lora],
                           q_s[..., kv_lora:kv_lora + rope_dim] + q_s[..., kv_lora + rope_dim:]], axis=-1)
    k_new = jnp.pad(kcat[n_p:].reshape(DB, TP, kw), ((0, 0), (0, LANES - TP), (0, 0)))
    o_s = _attn_sample(page_table, q_s, k_new, cache_ckv, jnp.swapaxes(cache_krope, 1, 2),
                       t_valid=TS, scale=scale)
    o_s = o_s.reshape(DB, n_bheads, TS, kv_lora).transpose(0, 2, 1, 3)
    o_s = jnp.pad(o_s, ((0, 0), (0, TP - TS), (0, 0), (0, 0))).reshape(n_s, n_bheads * kv_lora)
    o_lat = lax.dynamic_update_slice(o_lat, o_s.astype(BF16), (n_p, 0))

    w_uvh = jnp.transpose(w_uv, (1, 0, 2)).astype(BF16)
    w_rt = jnp.pad(w_router[0], ((0, 0), (0, LANES - n_experts))).astype(BF16)
    x3, xn_m, ids, wts = _attn_out(o_lat, x2, w_uvh, w_o_b[0].astype(BF16), g_ffn_m, w_rt,
                                   n_experts=n_experts)

    top_k = 2
    n2 = n * top_k
    assert n2 % MOE_TILE == 0
    order, pos, starts, ends = _route(ids[:, :top_k].reshape(-1), n_experts)
    row_token = order // top_k
    n_tiles = n2 // MOE_TILE
    parts = math.gcd(n_tiles, MOE_PARTS)
    part_tiles = n_tiles // parts
    weights = (wg_m.reshape(n_experts, D, f_m), wu_m.reshape(n_experts, D, f_m),
               wd_m.reshape(n_experts, f_m, D))
    ys = None
    for p in range(parts):
        t0 = p * part_tiles
        pairs = _tile_expert_pairs(starts, ends, t0, part_tiles, MOE_TILE,
                                   part_tiles + n_experts - 1)
        rows = lax.dynamic_slice_in_dim(row_token, t0 * MOE_TILE, part_tiles * MOE_TILE)
        ys = _moe(pairs, xn_m[rows], *weights, ys, t0, n2)
    pos = pos.reshape(n, top_k)
    y_p, y_s = _final(x3, ys[pos[:, 0]], ys[pos[:, 1]], wts, g_final.reshape(1, D), n_p)

    prompt = lambda a: a.reshape(B, T, a.shape[1])
    sample = lambda a: a.reshape(DB, TP, a.shape[1])[:, :TS]
    return (prompt(y_p), sample(y_s), c_p[None], n_pr[None], m_p[None], prompt(ckv_p), prompt(kr_p),
            c_s[None], n_sm[None], m_s[None], sample(ckv_s), sample(kr_s))
```

```python
import functools
import math

import jax
import jax.numpy as jnp
from jax import lax
from jax.experimental import pallas as pl
from jax.experimental.pallas import tpu as pltpu

F32 = jnp.float32
BF16 = jnp.bfloat16

EPS = 1e-6
GATE_CAP = 15.0
ROPE_THETA = 10000.0
NEG = -1e30
LANES = 128
VMEM_LIMIT_BYTES = 56 * 2**20

SAMPLE_PAD = 8
TOKEN_TILE = 512
ATTN_TILE = 512
ATTN_KEY_TILE = 256
MLSTM_CHUNK = 512
SHORT_SEQS_PER_STEP = 1
FFN_CHUNK = 1408
MOE_TILE = 512
MOE_CHUNK = 512
MOE_PARTS = 4
PAGES_PER_STEP = 64


def _cparams(*sem, flags=None):
    return pltpu.CompilerParams(dimension_semantics=sem, vmem_limit_bytes=VMEM_LIMIT_BYTES,
                                flags=flags)


def _dot(a, b):
    return jnp.dot(a, b, preferred_element_type=F32)


def _dot_nt(a, b):
    return lax.dot_general(a, b, (((1,), (1,)), ((), ())), preferred_element_type=F32)


def _rms(x, g):
    return x * lax.rsqrt(jnp.mean(x * x, axis=-1, keepdims=True) + EPS) * g


def _sigmoid(x):
    return 1.0 / (1.0 + jnp.exp(-x))


def _split3(x):
    hi = x.astype(BF16)
    r1 = x - hi.astype(F32)
    mid = r1.astype(BF16)
    lo = (r1 - mid.astype(F32)).astype(BF16)
    return hi, mid, lo


def _split_specs(tm, w, npb):
    return [pl.BlockSpec((tm, w), lambda i: (jnp.minimum(i, npb - 1), 0)),
            pl.BlockSpec((tm, w), lambda i: (jnp.maximum(i - npb, 0), 0))]


def _load_split(npb, p_ref, s_ref):
    return jnp.where(pl.program_id(0) < npb, p_ref[...], s_ref[...])


def _store_split(npb, p_ref, s_ref, val):
    i = pl.program_id(0)

    @pl.when(i < npb)
    def _():
        p_ref[...] = val

    @pl.when(i >= npb)
    def _():
        s_ref[...] = val


def _full(shape):
    nd = len(shape)
    return pl.BlockSpec(shape, lambda *_: (0,) * nd)


def _in_proj_kernel(xp_ref, xs_ref, g_ref, wq_ref, wk_ref, wv_ref, wo_ref, wg_ref, b_ref,
                    q_ref, k_ref, v_ref, o_ref, gate_ref, *, n_heads, q_scale, npb):
    xn = _rms(_load_split(npb, xp_ref, xs_ref), g_ref[...]).astype(BF16)
    q_ref[...] = _dot(xn, wq_ref[...]) * q_scale
    k_ref[...] = _dot(xn, wk_ref[...])
    v_ref[...] = _dot(xn, wv_ref[...])
    o_ref[...] = _dot(xn, wo_ref[...])
    g = _dot(xn, wg_ref[...]) + b_ref[...]
    g = GATE_CAP * jnp.tanh(g / GATE_CAP)
    logf = jnp.minimum(g, 0.0) - jnp.log1p(jnp.exp(-jnp.abs(g)))
    lane = lax.broadcasted_iota(jnp.int32, g.shape, 1)
    gate_ref[...] = jnp.where(lane < n_heads, g, jnp.where(lane < 2 * n_heads, logf, 0.0))


def _in_proj(xp, xs, g, wq, wk, wv, wo, wg, b, *, n_heads, q_scale):
    d = xp.shape[1]
    n = xp.shape[0] + xs.shape[0]
    tm = TOKEN_TILE
    npb = xp.shape[0] // tm
    row = lambda w: pl.BlockSpec((tm, w), lambda i: (i, 0))
    outs = [(wq.shape[1], F32), (wk.shape[1], F32), (wv.shape[1], F32), (wo.shape[1], F32), (LANES, F32)]
    return pl.pallas_call(
        functools.partial(_in_proj_kernel, n_heads=n_heads, q_scale=q_scale, npb=npb),
        grid=(n // tm,),
        in_specs=_split_specs(tm, d, npb) + [
            _full(g.shape), _full(wq.shape), _full(wk.shape), _full(wv.shape),
            _full(wo.shape), _full(wg.shape), _full(b.shape)],
        out_specs=[row(w) for w, _ in outs],
        out_shape=[jax.ShapeDtypeStruct((n, w), dt) for w, dt in outs],
        compiler_params=_cparams("parallel"),
        name="in_proj",
    )(xp, xs, g, wq, wk, wv, wo, wg, b)


def _mlstm_kernel(*refs, L, H, DK, DV, nb, t_valid, has_state, mm_dtype):
    if has_state:
        (q_ref, k_ref, v_ref, o_ref, gate_ref, gh_ref, c0_ref, n0_ref, m0_ref, _,
         hg_ref, cout_ref, nout_ref, mout_ref, caug_ref, m_scr) = refs
    else:
        (q_ref, k_ref, v_ref, o_ref, gate_ref, gh_ref,
         hg_ref, cout_ref, nout_ref, mout_ref, caug_ref, m_scr) = refs
    c = pl.program_id(1)
    last = pl.num_programs(1) - 1

    rk = lax.broadcasted_iota(jnp.int32, (DK, DK), 0)
    ck = lax.broadcasted_iota(jnp.int32, (DK, DK), 1)
    eye_k = rk == ck

    @pl.when(c == 0)
    def _():
        if has_state:
            for sh in range(nb * H):
                s, h = divmod(sh, H)
                caug_ref[sh, :, :DV] = c0_ref[s, h]
                n_row = n0_ref[s, h:h + 1, :]
                n_col = jnp.sum(jnp.where(eye_k, jnp.broadcast_to(n_row, (DK, DK)), 0.0),
                                axis=1, keepdims=True)
                caug_ref[sh, :, DV:] = jnp.broadcast_to(n_col, (DK, DV))
                m_scr[sh:sh + 1, :] = jnp.broadcast_to(m0_ref[s, :, h:h + 1], (1, LANES))
        else:
            caug_ref[...] = jnp.zeros_like(caug_ref)
            m_scr[...] = jnp.zeros_like(m_scr)

    ri = lax.broadcasted_iota(jnp.int32, (L, L), 0)
    ci = lax.broadcasted_iota(jnp.int32, (L, L), 1)
    causal = ci <= ri
    eye = ci == ri
    ones_v = jnp.ones((L, DV), mm_dtype)
    use_mxu_cumsum = L % LANES == 0

    def head(s, h, gates, cums):
        rows = slice(s * L, (s + 1) * L)
        sh = s * H + h
        if use_mxu_cumsum:
            gates_t, cum, cum_t = cums
            a_col = cum[:, H + h:H + h + 1]
            b_row = gates_t[h:h + 1, :] - cum_t[H + h:H + h + 1, :]
        else:
            li_col = gates[:, h:h + 1]
            lf_col = gates[:, H + h:H + h + 1]
            lf_b = jnp.broadcast_to(lf_col, (L, L))
            lf_row = jnp.sum(jnp.where(eye, lf_b, 0.0), axis=0, keepdims=True)
            a_col = jnp.sum(jnp.where(causal, jnp.broadcast_to(lf_row, (L, L)), 0.0),
                            axis=1, keepdims=True)
            a_row = jnp.sum(jnp.where(ci >= ri, lf_b, 0.0), axis=0, keepdims=True)
            li_row = jnp.sum(jnp.where(eye, jnp.broadcast_to(li_col, (L, L)), 0.0),
                             axis=0, keepdims=True)
            b_row = li_row - a_row
        d = jnp.where(causal, a_col + b_row, NEG)
        m_prev = m_scr[sh:sh + 1, 0:1]
        inter = a_col + m_prev
        m_t = jnp.maximum(inter, jnp.max(d, axis=1, keepdims=True))

        qh = q_ref[rows, h * DK:(h + 1) * DK].astype(mm_dtype)
        kh = k_ref[rows, h * DK:(h + 1) * DK].astype(mm_dtype)
        vaug = jnp.concatenate([v_ref[rows, h * DV:(h + 1) * DV].astype(mm_dtype), ones_v], axis=1)
        caug = caug_ref[sh]

        w = jnp.exp(d - m_t) * _dot_nt(qh, kh)
        e_inter = jnp.exp(inter - m_t)
        num = _dot(w.astype(mm_dtype), vaug) + e_inter * _dot(qh, caug.astype(mm_dtype))
        hh = num[:, :DV] / jnp.maximum(jnp.abs(num[:, DV:]), jnp.exp(-m_t))
        hn = _rms(hh, gh_ref[h:h + 1, :])
        hg_ref[rows, h * DV:(h + 1) * DV] = _sigmoid(o_ref[rows, h * DV:(h + 1) * DV]) * hn

        m_new = m_t[L - 1:L, :]
        a_last = a_col[L - 1:L, :]
        e_end = jnp.exp(a_last + b_row - m_new)
        e_carry = jnp.exp(a_last + m_prev - m_new)
        k_t = _dot_nt(eye_k.astype(mm_dtype), kh)
        caug_new = e_carry * caug + _dot((k_t * e_end).astype(mm_dtype), vaug)
        caug_ref[sh] = caug_new
        m_scr[sh:sh + 1, :] = jnp.broadcast_to(m_new, (1, LANES))

        @pl.when(c == last)
        def _():
            cout_ref[s, h] = caug_new[:, :DV]
            nout_ref[s, h:h + 1, :] = jnp.sum(jnp.where(eye_k, caug_new[:, DV:DV + DK], 0.0),
                                              axis=0, keepdims=True)
            mout_ref[s, :, h:h + 1] = m_new

    for s in range(nb):
        gates = gate_ref[s * L:(s + 1) * L, :]
        if t_valid < L:
            t_id = lax.broadcasted_iota(jnp.int32, gates.shape, 0)
            lane = lax.broadcasted_iota(jnp.int32, gates.shape, 1)
            gates = jnp.where(t_id < t_valid, gates, jnp.where(lane < H, NEG, 0.0))
        cums = None
        if use_mxu_cumsum:
            gates_t = gates.T
            cum = sum(_dot(causal.astype(BF16), p) for p in _split3(gates))
            cum_t = sum(_dot(p, (ri <= ci).astype(BF16)) for p in _split3(gates_t))
            cums = (gates_t, cum, cum_t)
        for h in range(H):
            head(s, h, gates, cums)


def _mlstm(q, k, v, o, gates, g_head, *, row0, B, T, L, t_valid, state=None, hg_prev=None):
    n = q.shape[0]
    H, DV = g_head.shape
    DK = q.shape[1] // H
    nc = T // L
    nb = SHORT_SEQS_PER_STEP if nc == 1 else 1
    assert B % nb == 0 and row0 % (nb * L) == 0
    blk0 = row0 // (nb * L)
    row = lambda w: pl.BlockSpec((nb * L, w), lambda b, c: (blk0 + b * nc + c, 0))
    c_spec = pl.BlockSpec((nb, H, DK, DV), lambda b, c: (b, 0, 0, 0))
    n_spec = pl.BlockSpec((nb, H, DK), lambda b, c: (b, 0, 0))
    m_spec = pl.BlockSpec((nb, 1, H), lambda b, c: (b, 0, 0))
    in_specs = [row(H * DK), row(H * DK), row(H * DV), row(H * DV), row(LANES), _full(g_head.shape)]
    args = [q, k, v, o, gates, g_head]
    aliases = {}
    if state is not None:
        c0, n0, m0 = state
        in_specs += [c_spec, n_spec, m_spec, pl.BlockSpec(memory_space=pl.ANY)]
        args += [c0, n0, m0.reshape(B, 1, H), hg_prev]
        aliases = {len(args) - 1: 0}
    out_shape = [jax.ShapeDtypeStruct((n, H * DV), F32),
                 jax.ShapeDtypeStruct((B, H, DK, DV), F32),
                 jax.ShapeDtypeStruct((B, H, DK), F32),
                 jax.ShapeDtypeStruct((B, 1, H), F32)]
    out_specs = [row(H * DV), c_spec, n_spec, m_spec]
    kern = functools.partial(_mlstm_kernel, L=L, H=H, DK=DK, DV=DV, nb=nb, t_valid=t_valid,
                             has_state=state is not None,
                             mm_dtype=BF16 if L % 16 == 0 else F32)
    hg, c_out, n_out, m_out = pl.pallas_call(
        kern, grid=(B // nb, nc), in_specs=in_specs, out_specs=out_specs, out_shape=out_shape,
        scratch_shapes=[pltpu.VMEM((nb * H, DK, 2 * DV), F32), pltpu.VMEM((nb * H, LANES), F32)],
        input_output_aliases=aliases,
        compiler_params=_cparams("parallel", "arbitrary"),
        name="mlstm_sample" if state is not None else "mlstm_prompt",
    )(*args)
    return hg, c_out, n_out, m_out.reshape(B, H)


def _ffn_kernel(xp_ref, xs_ref, hg_ref, wout_ref, g_ref, wg_ref, wu_ref, wd_ref, out_ref, acc_ref,
                *, npb):
    x1 = _load_split(npb, xp_ref, xs_ref) + _dot(hg_ref[...].astype(BF16), wout_ref[...])
    xn = _rms(x1, g_ref[...]).astype(BF16)
    acc_ref[...] = jnp.zeros_like(acc_ref)

    def body(c, carry):
        gate = _dot(xn, wg_ref[c])
        up = _dot(xn, wu_ref[c])
        hmid = (gate * _sigmoid(gate) * up).astype(BF16)
        acc_ref[...] += _dot(hmid, wd_ref[c])
        return carry

    lax.fori_loop(0, wg_ref.shape[0], body, 0)
    out_ref[...] = x1 + acc_ref[...]


def _ffn(xp, xs, hg, wout, g, wg, wu, wd):
    n, d = hg.shape
    tm = TOKEN_TILE
    npb = xp.shape[0] // tm
    row = pl.BlockSpec((tm, d), lambda i: (i, 0))
    return pl.pallas_call(
        functools.partial(_ffn_kernel, npb=npb), grid=(n // tm,),
        in_specs=_split_specs(tm, d, npb) + [
            row, _full(wout.shape), _full(g.shape), _full(wg.shape), _full(wu.shape),
            _full(wd.shape)],
        out_specs=row, out_shape=jax.ShapeDtypeStruct((n, d), F32),
        scratch_shapes=[pltpu.VMEM((tm, d), F32)],
        compiler_params=_cparams("parallel"),
        name="outproj_ffn",
    )(xp, xs, hg, wout, g, wg, wu, wd)


def _latq_kernel(x_ref, cos_ref, sin_ref, gkv_ref, wdkv_ref, gckv_ref, gnb_ref, wdq_ref, gq_ref,
                 wn_ref, wr_ref, wrr_ref, wuk_ref,
                 ckvp_ref, ckvs_ref, krp_ref, krs_ref, kcat_ref, kvt_ref, q_ref,
                 *, n_heads, kv_lora, rope_dim, nope_dim, npb):
    x = x_ref[...]
    xs = x * lax.rsqrt(jnp.mean(x * x, axis=-1, keepdims=True) + EPS)
    cos = cos_ref[...]
    sin = sin_ref[...]
    lane = lax.broadcasted_iota(jnp.int32, cos.shape, 1)
    lo = lane < rope_dim

    lat = _dot((xs * gkv_ref[...]).astype(BF16), wdkv_ref[...])
    ckv = _rms(lat[:, :kv_lora], gckv_ref[...])
    _store_split(npb, ckvp_ref, ckvs_ref, ckv)
    t = lat[:, kv_lora:] * jnp.where(lo, cos, sin)
    kr2 = t + pltpu.roll(t, rope_dim, axis=1)
    _store_split(npb, krp_ref, krs_ref, kr2[:, :rope_dim])
    kcat_ref[...] = jnp.concatenate([ckv, kr2], axis=1).astype(BF16)
    for r in range(kvt_ref.shape[0]):
        kvt_ref[r] = ckv[r * ATTN_KEY_TILE:(r + 1) * ATTN_KEY_TILE].T.astype(BF16)

    cq = _dot((xs * gnb_ref[...]).astype(BF16), wdq_ref[...])
    cqn = _rms(cq, gq_ref[...]).astype(BF16)
    qn = _dot(cqn, wn_ref[...]).astype(BF16)
    reps = n_heads * rope_dim // LANES
    cos_h = jnp.concatenate([cos] * reps, axis=1)
    sin_h = jnp.concatenate([sin] * reps, axis=1)
    qr = _dot(cqn, wr_ref[...]) * cos_h + _dot(cqn, wrr_ref[...]) * sin_h
    for h in range(n_heads):
        ql = _dot(qn[:, h * nope_dim:(h + 1) * nope_dim], wuk_ref[h])
        pair = qr[:, (h // 2) * LANES:(h // 2 + 1) * LANES]
        slot = jnp.where(lo if h % 2 == 0 else jnp.logical_not(lo), pair, 0.0)
        q_ref[0, h] = jnp.concatenate([ql, slot], axis=1).astype(BF16)


def _latq(x, cos_tab, sin_tab, n_prompt_blocks, blocks_per_seq, weights, *, n_heads, kv_lora,
          rope_dim, nope_dim):
    n, d = x.shape
    tm = ATTN_TILE
    kw = kv_lora + 2 * rope_dim

    def tab_map(i):
        return (jnp.where(i < n_prompt_blocks, i % blocks_per_seq, blocks_per_seq), 0)

    tab = pl.BlockSpec((tm, LANES), tab_map)
    row = lambda w: pl.BlockSpec((tm, w), lambda i: (i, 0))
    npb = n_prompt_blocks
    n_p, n_s = npb * tm, n - npb * tm
    return pl.pallas_call(
        functools.partial(_latq_kernel, n_heads=n_heads, kv_lora=kv_lora, rope_dim=rope_dim,
                          nope_dim=nope_dim, npb=npb),
        grid=(n // tm,),
        in_specs=[row(d), tab, tab] + [_full(w.shape) for w in weights],
        out_specs=_split_specs(tm, kv_lora, npb) + _split_specs(tm, rope_dim, npb) + [
            row(kw),
            pl.BlockSpec((tm // ATTN_KEY_TILE, kv_lora, ATTN_KEY_TILE), lambda i: (i, 0, 0)),
            pl.BlockSpec((1, n_heads, tm, kw), lambda i: (i, 0, 0, 0))],
        out_shape=[jax.ShapeDtypeStruct((n_p, kv_lora), F32),
                   jax.ShapeDtypeStruct((n_s, kv_lora), F32),
                   jax.ShapeDtypeStruct((n_p, rope_dim), F32),
                   jax.ShapeDtypeStruct((n_s, rope_dim), F32),
                   jax.ShapeDtypeStruct((n, kw), BF16),
                   jax.ShapeDtypeStruct((n // ATTN_KEY_TILE, kv_lora, ATTN_KEY_TILE), BF16),
                   jax.ShapeDtypeStruct((n // tm, n_heads, tm, kw), BF16)],
        compiler_params=_cparams("arbitrary"),
        name="latent_q",
    )(x, cos_tab, sin_tab, *weights)


def _attn_prompt_kernel(q_ref, k_ref, kt_ref, *refs, tq, tk, kv_lora, scale, n_cast):
    cast_in = refs[:n_cast]
    o_ref = refs[n_cast]
    cast_out = refs[n_cast + 1:2 * n_cast + 1]
    m_scr, l_scr, acc_scr, sa_scr, sb_scr = refs[2 * n_cast + 1:]
    for src, dst in zip(cast_in, cast_out):
        dst[...] = src[...].astype(dst.dtype)

    qi = pl.program_id(1)
    n_heads = q_ref.shape[1]
    m_scr[...] = jnp.full_like(m_scr, NEG)
    l_scr[...] = jnp.zeros_like(l_scr)
    acc_scr[...] = jnp.zeros_like(acc_scr)

    def scores(j, s_ref):
        kc = k_ref[pl.ds(pl.multiple_of(j * tk, tk), tk), :]
        for h in range(n_heads):
            s_ref[h] = _dot_nt(kc, q_ref[0, h])

    def consume(j, s_all, diag_offset):
        kt = kt_ref[j]
        if diag_offset is not None:
            key = lax.broadcasted_iota(jnp.int32, (tk, tq), 0) + diag_offset
            qry = lax.broadcasted_iota(jnp.int32, (tk, tq), 1)
            keep = key <= qry
        for h in range(n_heads):
            st = s_all[h] * scale
            if diag_offset is not None:
                st = jnp.where(keep, st, NEG)
            m_prev = m_scr[h]
            m_new = jnp.maximum(m_prev, jnp.max(st, axis=0, keepdims=True))
            p = jnp.exp(st - m_new)
            alpha = jnp.exp(m_prev - m_new)
            l_scr[h] = alpha * l_scr[h] + jnp.sum(p, axis=0, keepdims=True)
            acc_scr[h] = alpha * acc_scr[h] + _dot(kt, p.astype(BF16))
            m_scr[h] = m_new

    assert tq == 2 * tk

    def body(i, carry):
        j = 2 * i
        scores(j + 1, sb_scr)
        consume(j, sa_scr, None)
        scores(j + 2, sa_scr)
        consume(j + 1, sb_scr, None)
        return carry

    scores(0, sa_scr)
    lax.fori_loop(0, qi, body, 0)
    n_full = 2 * qi
    scores(n_full + 1, sb_scr)
    consume(n_full, sa_scr, 0)
    consume(n_full + 1, sb_scr, tk)
    for h in range(n_heads):
        o_ref[:, h * kv_lora:(h + 1) * kv_lora] = (acc_scr[h] / l_scr[h]).T.astype(BF16)


def _attn_prompt(q, kcat, kvt, n_rows, to_cast, *, B, T, kv_lora, scale):
    tq, tk = ATTN_TILE, ATTN_KEY_TILE
    _, n_heads, _, kw = q.shape
    nq = T // tq
    steps = B * nq
    cast_specs = []
    for w in to_cast:
        rows = w.shape[0] // steps
        assert rows * steps == w.shape[0] and rows % 16 == 0
        cast_specs.append(pl.BlockSpec((rows, w.shape[1]), lambda b, i: (b * nq + i, 0)))
    out = pl.pallas_call(
        functools.partial(_attn_prompt_kernel, tq=tq, tk=tk, kv_lora=kv_lora, scale=scale,
                          n_cast=len(to_cast)),
        grid=(B, nq),
        in_specs=[pl.BlockSpec((1, n_heads, tq, kw), lambda b, i: (b * nq + i, 0, 0, 0)),
                  pl.BlockSpec((T, kw), lambda b, i: (b, 0)),
                  pl.BlockSpec((T // tk, kv_lora, tk), lambda b, i: (b, 0, 0))] + cast_specs,
        out_specs=[pl.BlockSpec((tq, n_heads * kv_lora), lambda b, i: (b * nq + i, 0))] + cast_specs,
        out_shape=[jax.ShapeDtypeStruct((n_rows, n_heads * kv_lora), BF16)]
        + [jax.ShapeDtypeStruct(w.shape, BF16) for w in to_cast],
        scratch_shapes=[pltpu.VMEM((n_heads, 1, tq), F32), pltpu.VMEM((n_heads, 1, tq), F32),
                        pltpu.VMEM((n_heads, kv_lora, tq), F32),
                        pltpu.VMEM((n_heads, tk, tq), F32), pltpu.VMEM((n_heads, tk, tq), F32)],
        compiler_params=_cparams("parallel", "arbitrary"),
        name="attn_prompt",
    )(q, kcat, kvt, *to_cast)
    return out[0], out[1:]


def _attn_sample_kernel(pt_ref, q_ref, knew_ref, *refs, n_pages, t_valid, kv_lora, rope_dim, scale,
                        single_step):
    ck_refs = refs[:n_pages]
    kr_refs = refs[n_pages:2 * n_pages]
    o_ref, m_scr, l_scr, acc_scr = refs[2 * n_pages:]
    g = pl.program_id(1)
    q = q_ref[0]
    ql = q[:, :kv_lora]
    qr = q[:, kv_lora:]

    def new_token_scores():
        kn = knew_ref[0]
        ckn = kn[:, :kv_lora]
        sn = (_dot_nt(ql, ckn) + _dot_nt(qr, kn[:, kv_lora:kv_lora + rope_dim])) * scale
        t = lax.broadcasted_iota(jnp.int32, sn.shape, 0) % t_valid
        j = lax.broadcasted_iota(jnp.int32, sn.shape, 1)
        return jnp.where(j <= t, sn, NEG), ckn

    if single_step:
        cks = [r[0].astype(BF16) for r in ck_refs]
        sn, ckn = new_token_scores()
        s = jnp.concatenate(
            [(_dot_nt(ql, ck) + _dot(qr, kr[0].astype(BF16))) * scale
             for ck, kr in zip(cks, kr_refs)] + [sn], axis=1)
        p = jnp.exp(s - jnp.max(s, axis=1, keepdims=True))
        denom = jnp.sum(p, axis=1, keepdims=True)
        p = p.astype(BF16)
        values = cks + [ckn]
        page = cks[0].shape[0]
        pv = _dot(p[:, :page], values[0])
        for i in range(1, len(values)):
            pv += _dot(p[:, i * page:(i + 1) * page], values[i])
        o_ref[0] = pv / denom
        return

    @pl.when(g == 0)
    def _():
        m_scr[...] = jnp.full_like(m_scr, NEG)
        l_scr[...] = jnp.zeros_like(l_scr)
        acc_scr[...] = jnp.zeros_like(acc_scr)

    def update(s, values):
        m_prev = m_scr[...]
        m_new = jnp.maximum(m_prev, jnp.max(s, axis=1, keepdims=True))
        p = jnp.exp(s - m_new)
        alpha = jnp.exp(m_prev - m_new)
        l_scr[...] = alpha * l_scr[...] + jnp.sum(p, axis=1, keepdims=True)
        p = p.astype(BF16)
        pv = _dot(p[:, :values[0].shape[0]], values[0])
        for i in range(1, len(values)):
            rows = values[i].shape[0]
            pv += _dot(p[:, i * rows:(i + 1) * rows], values[i])
        acc_scr[...] = alpha * acc_scr[...] + pv
        m_scr[...] = m_new

    cks = [r[0].astype(BF16) for r in ck_refs]
    s = jnp.concatenate(
        [_dot_nt(ql, ck) + _dot(qr, kr[0].astype(BF16)) for ck, kr in zip(cks, kr_refs)],
        axis=1) * scale
    update(s, cks)

    @pl.when(g == pl.num_programs(1) - 1)
    def _():
        sn, ckn = new_token_scores()
        update(sn, [ckn])
        o_ref[0] = acc_scr[...] / l_scr[...]


def _attn_sample(page_table, q, knew, cache_ckv, cache_krope_t, *, t_valid, scale):
    nb, n_pages = page_table.shape
    _, rows, qw = q.shape
    _, page, kv_lora = cache_ckv.shape
    rope_dim = cache_krope_t.shape[1]
    pg = min(PAGES_PER_STEP, n_pages)
    assert n_pages % pg == 0
    ck_specs = [pl.BlockSpec((1, page, kv_lora),
                             functools.partial(lambda b, g, pt, i: (pt[b, g * pg + i], 0, 0), i=i))
                for i in range(pg)]
    kr_specs = [pl.BlockSpec((1, rope_dim, page),
                             functools.partial(lambda b, g, pt, i: (pt[b, g * pg + i], 0, 0), i=i))
                for i in range(pg)]
    grid_spec = pltpu.PrefetchScalarGridSpec(
        num_scalar_prefetch=1, grid=(nb, n_pages // pg),
        in_specs=[pl.BlockSpec((1, rows, qw), lambda b, g, pt: (b, 0, 0)),
                  pl.BlockSpec((1,) + knew.shape[1:], lambda b, g, pt: (b, 0, 0))]
        + ck_specs + kr_specs,
        out_specs=pl.BlockSpec((1, rows, kv_lora), lambda b, g, pt: (b, 0, 0)),
        scratch_shapes=[pltpu.VMEM((rows, 1), F32), pltpu.VMEM((rows, 1), F32),
                        pltpu.VMEM((rows, kv_lora), F32)])
    return pl.pallas_call(
        functools.partial(_attn_sample_kernel, n_pages=pg, t_valid=t_valid, kv_lora=kv_lora,
                          rope_dim=rope_dim, scale=scale, single_step=pg == n_pages),
        grid_spec=grid_spec,
        out_shape=jax.ShapeDtypeStruct((nb, rows, kv_lora), F32),
        compiler_params=_cparams("parallel", "arbitrary"),
        name="attn_sample",
    )(page_table, q, knew, *([cache_ckv] * pg), *([cache_krope_t] * pg))


def _attn_out_kernel(ol_ref, x_ref, wuv_ref, wo_ref, g_ref, wr_ref,
                     x3_ref, xn_ref, ids_ref, wts_ref, *, n_experts):
    n_heads, kv_lora, _ = wuv_ref.shape
    o = jnp.concatenate(
        [_dot(ol_ref[:, h * kv_lora:(h + 1) * kv_lora], wuv_ref[h]) for h in range(n_heads)],
        axis=1).astype(BF16)
    x3 = x_ref[...] + _dot(o, wo_ref[...])
    x3_ref[...] = x3
    xn = _rms(x3, g_ref[...]).astype(BF16)
    xn_ref[...] = xn
    logits = _dot(xn, wr_ref[...])
    lane = lax.broadcasted_iota(jnp.int32, logits.shape, 1)
    lane_f = lane.astype(F32)
    lg = jnp.where(lane < n_experts, logits, -jnp.inf)
    v1 = jnp.max(lg, axis=1, keepdims=True)
    i1 = jnp.min(jnp.where(lg == v1, lane_f, float(LANES)), axis=1, keepdims=True)
    lg2 = jnp.where(lane_f == i1, -jnp.inf, lg)
    v2 = jnp.max(lg2, axis=1, keepdims=True)
    i2 = jnp.min(jnp.where(lg2 == v2, lane_f, float(LANES)), axis=1, keepdims=True)
    e = jnp.exp(v2 - v1)
    w1 = 1.0 / (1.0 + e)
    w2 = e / (1.0 + e)
    ids_ref[...] = jnp.where(lane == 0, i1, jnp.where(lane == 1, i2, 0.0)).astype(jnp.int32)
    wts_ref[...] = jnp.where(lane == 0, w1, jnp.where(lane == 1, w2, 0.0))


def _attn_out(o_lat, x, wuv, wo, g, wr, *, n_experts):
    n, d = x.shape
    tm = TOKEN_TILE
    row = lambda w: pl.BlockSpec((tm, w), lambda i: (i, 0))
    return pl.pallas_call(
        functools.partial(_attn_out_kernel, n_experts=n_experts),
        grid=(n // tm,),
        in_specs=[row(o_lat.shape[1]), row(d), _full(wuv.shape), _full(wo.shape), _full(g.shape),
                  _full(wr.shape)],
        out_specs=[row(d), row(d), row(LANES), row(LANES)],
        out_shape=[jax.ShapeDtypeStruct((n, d), F32), jax.ShapeDtypeStruct((n, d), BF16),
                   jax.ShapeDtypeStruct((n, LANES), jnp.int32),
                   jax.ShapeDtypeStruct((n, LANES), F32)],
        compiler_params=_cparams("parallel"),
        name="attn_out_router",
    )(o_lat, x, wuv, wo, g, wr)


def _moe_kernel(pt_ref, pe_ref, lo_ref, hi_ref, first_ref, np_ref, xs_ref, wg_ref, wu_ref, wd_ref,
                *refs, chunk):
    out_ref = refs[-1]
    i = pl.program_id(0)

    @pl.when(i < np_ref[0])
    def _():
        xs = xs_ref[...]
        acc = jnp.zeros(out_ref.shape, F32)
        for c in range(wg_ref.shape[2] // chunk):
            sl = slice(c * chunk, (c + 1) * chunk)
            gate = _dot(xs, wg_ref[0, :, sl])
            up = _dot(xs, wu_ref[0, :, sl])
            acc += _dot((gate * _sigmoid(gate) * up).astype(BF16), wd_ref[0, sl, :])
        res = acc.astype(out_ref.dtype)
        row = lax.broadcasted_iota(jnp.int32, out_ref.shape, 0)
        lo, hi = lo_ref[i], hi_ref[i]

        def keep_rows(base):
            return jnp.where(row >= lo, jnp.where(row < hi, res, base), base)

        @pl.when(first_ref[i] == 1)
        def _():
            out_ref[...] = keep_rows(jnp.zeros_like(res))

        @pl.when(first_ref[i] == 0)
        def _():
            out_ref[...] = keep_rows(out_ref[...])


def _moe(pairs, xs, wg, wu, wd, out_prev, t0, n_rows):
    d = xs.shape[1]
    tm = MOE_TILE
    max_pairs = pairs[0].shape[0]
    expert = lambda i, pt, pe, lo, hi, first, n: (pe[i], 0, 0)
    w_specs = [pl.BlockSpec((1,) + w.shape[1:], expert, pipeline_mode=pl.Buffered(1))
               for w in (wg, wu, wd)]
    args = [xs, wg, wu, wd]
    in_specs = [pl.BlockSpec((tm, d), lambda i, pt, pe, lo, hi, first, n: (pt[i], 0))] + w_specs
    aliases = {}
    if out_prev is not None:
        in_specs.append(pl.BlockSpec(memory_space=pl.ANY))
        args.append(out_prev)
        aliases = {len(pairs) + len(args) - 1: 0}
    grid_spec = pltpu.PrefetchScalarGridSpec(
        num_scalar_prefetch=len(pairs), grid=(max_pairs,), in_specs=in_specs,
        out_specs=pl.BlockSpec((tm, d), lambda i, pt, pe, lo, hi, first, n: (t0 + pt[i], 0)))
    return pl.pallas_call(
        functools.partial(_moe_kernel, chunk=MOE_CHUNK),
        grid_spec=grid_spec,
        out_shape=jax.ShapeDtypeStruct((n_rows, d), BF16),
        input_output_aliases=aliases,
        compiler_params=_cparams("arbitrary"),
        name="moe_experts",
    )(*pairs, *args)


def _final_kernel(x_ref, y1_ref, y2_ref, wts_ref, g_ref, outp_ref, outs_ref, *, npb):
    w = wts_ref[...]
    x4 = x_ref[...] + (w[:, 0:1] * y1_ref[...].astype(F32) + w[:, 1:2] * y2_ref[...].astype(F32))
    _store_split(npb, outp_ref, outs_ref, _rms(x4, g_ref[...]))


def _final(x, y1, y2, wts, g, n_p):
    n, d = x.shape
    tm = TOKEN_TILE
    npb = n_p // tm
    row = lambda w: pl.BlockSpec((tm, w), lambda i: (i, 0))
    return pl.pallas_call(
        functools.partial(_final_kernel, npb=npb), grid=(n // tm,),
        in_specs=[row(d), row(d), row(d), row(LANES), _full(g.shape)],
        out_specs=_split_specs(tm, d, npb),
        out_shape=[jax.ShapeDtypeStruct((n_p, d), F32), jax.ShapeDtypeStruct((n - n_p, d), F32)],
        compiler_params=_cparams("arbitrary"),
        name="combine_final",
    )(x, y1, y2, wts, g)


def _rot_cols(w, half):
    return jnp.concatenate([-w[..., half:], w[..., :half]], axis=-1)


def _route(ids, n_experts):
    n2 = ids.shape[0]
    shift = (n2 - 1).bit_length()
    keys = (ids << shift) | jnp.arange(n2, dtype=jnp.int32)
    order = jnp.sort(keys) & ((1 << shift) - 1)
    onehot = (ids[:, None] == jnp.arange(n_experts, dtype=jnp.int32)[None, :]).astype(jnp.int32)
    rank = jnp.take_along_axis(jnp.cumsum(onehot, axis=0) - onehot, ids[:, None], axis=1)[:, 0]
    ends = jnp.cumsum(jnp.sum(onehot, axis=0))
    starts = ends - jnp.sum(onehot, axis=0)
    return order, starts[ids] + rank, starts, ends


def _tile_expert_pairs(starts, ends, t0, n_tiles, tile, max_pairs):
    n_experts = starts.shape[0]
    base = (t0 + jnp.arange(n_tiles, dtype=jnp.int32))[:, None] * tile
    lo = jnp.maximum(starts[None, :] - base, 0).reshape(-1)
    hi = jnp.minimum(ends[None, :] - base, tile).reshape(-1)
    hit = hi > lo
    n = jnp.sum(hit).astype(jnp.int32)
    idx = jnp.nonzero(hit, size=max_pairs, fill_value=0)[0].astype(jnp.int32)
    idx = jnp.where(jnp.arange(max_pairs) < n, idx, idx[n - 1])
    p_tile = idx // n_experts
    first = jnp.concatenate([jnp.ones((1,), jnp.int32),
                             (p_tile[1:] != p_tile[:-1]).astype(jnp.int32)])
    return p_tile, idx % n_experts, lo[idx], hi[idx], first, n.reshape(1)


def kernel(x_prompt, x_sample, state_C, state_n, state_m, cache_ckv, cache_krope, page_table, g_norm_a, w_in_a, b_gate_a, g_head_a, w_out_a, g_kv, w_dkv, g_ckv, w_uk, w_uv, g_norm_b, w_dq, g_q, w_uq, w_o_b, g_ffn_d, w_gate_d, w_up_d, w_down_d, g_ffn_m, w_router, w_gate_m, w_up_m, w_down_m, g_final):
    B, T, D = x_prompt.shape
    DB, TS, _ = x_sample.shape
    H, DV = g_head_a.shape[1:]
    DK = state_C.shape[3]
    kv_lora, n_bheads, nope_dim = w_uk.shape
    rope_dim = cache_krope.shape[2]
    page = cache_ckv.shape[1]
    past_len = page_table.shape[1] * page
    n_experts = w_router.shape[2]
    assert state_C.shape[0] == 1 and g_norm_b.shape[0] == 1 and g_ffn_d.shape[0] == 1
    assert TS <= SAMPLE_PAD and 2 * rope_dim == LANES and 2 * H <= LANES
    TP = SAMPLE_PAD
    n_p, n_s = B * T, DB * TP
    n = n_p + n_s
    assert n_p % TOKEN_TILE == 0 and n_s % TOKEN_TILE == 0 and T % ATTN_TILE == 0

    x_p = x_prompt.reshape(n_p, D)
    x_s = jnp.pad(x_sample, ((0, 0), (0, TP - TS), (0, 0))).reshape(n_s, D)

    w_in = w_in_a[0].astype(BF16)
    hq, hv = H * DK, H * DV
    wq, wk, wv, wo = (w_in[:, :hq], w_in[:, hq:2 * hq], w_in[:, 2 * hq:2 * hq + hv],
                      w_in[:, 2 * hq + hv:2 * hq + 2 * hv])
    wgate = jnp.pad(w_in[:, 2 * hq + 2 * hv:], ((0, 0), (0, LANES - 2 * H)))
    bgate = jnp.pad(b_gate_a[0], (0, LANES - 2 * H)).reshape(1, LANES)
    q, k, v, o, gates = _in_proj(x_p, x_s, g_norm_a, wq, wk, wv, wo, wgate, bgate,
                                 n_heads=H, q_scale=DK ** -0.5)
    hg, c_p, n_pr, m_p = _mlstm(q, k, v, o, gates, g_head_a[0], row0=0, B=B, T=T,
                                L=MLSTM_CHUNK, t_valid=MLSTM_CHUNK)
    hg, c_s, n_sm, m_s = _mlstm(q, k, v, o, gates, g_head_a[0], row0=n_p, B=DB, T=TP, L=TP,
                                t_valid=TS, state=(state_C[0], state_n[0], state_m[0]), hg_prev=hg)

    f = w_gate_d.shape[2]
    f_pad = -(-f // FFN_CHUNK) * FFN_CHUNK
    nch = f_pad // FFN_CHUNK
    col_chunks = lambda w: jnp.pad(w.astype(BF16), ((0, 0), (0, f_pad - f))).reshape(
        D, nch, FFN_CHUNK).transpose(1, 0, 2)
    wd_d = jnp.pad(w_down_d[0].astype(BF16), ((0, f_pad - f), (0, 0))).reshape(nch, FFN_CHUNK, D)
    x2 = _ffn(x_p, x_s, hg, w_out_a[0].astype(BF16), g_ffn_d, col_chunks(w_gate_d[0]),
              col_chunks(w_up_d[0]), wd_d)

    half = rope_dim // 2
    inv = ROPE_THETA ** (-jnp.arange(half, dtype=F32) / half)

    def tables(pos):
        ang = pos.astype(F32)[:, None] * inv[None, :]
        return (jnp.tile(jnp.cos(ang), (1, LANES // half)), jnp.tile(jnp.sin(ang), (1, LANES // half)))

    cos_p, sin_p = tables(jnp.arange(T, dtype=jnp.int32))
    cos_s, sin_s = tables(past_len + jnp.arange(TP, dtype=jnp.int32))
    reps = ATTN_TILE // TP
    cos_tab = jnp.concatenate([cos_p, jnp.tile(cos_s, (reps, 1))], axis=0)
    sin_tab = jnp.concatenate([sin_p, jnp.tile(sin_s, (reps, 1))], axis=0)

    w_kr = w_dkv[:, kv_lora:]
    wdkv = jnp.concatenate([w_dkv[:, :kv_lora], w_kr, _rot_cols(w_kr, half)], axis=1).astype(BF16)
    wuq = w_uq[0].reshape(-1, n_bheads, nope_dim + rope_dim)
    w_nope = wuq[:, :, :nope_dim].reshape(-1, n_bheads * nope_dim).astype(BF16)
    w_rope = wuq[:, :, nope_dim:]
    w_r = w_rope.reshape(-1, n_bheads * rope_dim).astype(BF16)
    w_rr = _rot_cols(w_rope, half).reshape(-1, n_bheads * rope_dim).astype(BF16)
    w_ukt = jnp.transpose(w_uk, (1, 2, 0)).astype(BF16)
    lat_weights = [g_kv.reshape(1, D), wdkv, g_ckv.reshape(1, kv_lora), g_norm_b,
                   w_dq[0].astype(BF16), g_q, w_nope, w_r, w_rr, w_ukt]
    ckv_p, ckv_s, kr_p, kr_s, kcat, kvt, qcat = _latq(
        x2, cos_tab, sin_tab, n_p // ATTN_TILE, T // ATTN_TILE, lat_weights, n_heads=n_bheads,
        kv_lora=kv_lora, rope_dim=rope_dim, nope_dim=nope_dim)

    scale = (nope_dim + rope_dim) ** -0.5
    f_m = w_gate_m.shape[3]
    o_lat, (wg_m, wu_m, wd_m) = _attn_prompt(
        qcat, kcat, kvt, n,
        [w_gate_m[0].reshape(n_experts * D, f_m), w_up_m[0].reshape(n_experts * D, f_m),
         w_down_m[0].reshape(n_experts * f_m, D)],
        B=B, T=T, kv_lora=kv_lora, scale=scale)

    kw = kv_lora + 2 * rope_dim
    q_s = qcat[n_p // ATTN_TILE:].reshape(-1, n_bheads, ATTN_TILE // TP, TP, kw)[:, :, :, :TS]
    q_s = q_s.transpose(0, 2, 1, 3, 4).reshape(DB, n_bheads * TS, kw)
    q_s = jnp.concatenate([q_s[..., :kv_lora],
                           q_s[..., kv_lora:kv_lora + rope_dim] + q_s[..., kv_lora + rope_dim:]], axis=-1)
    k_new = jnp.pad(kcat[n_p:].reshape(DB, TP, kw), ((0, 0), (0, LANES - TP), (0, 0)))
    o_s = _attn_sample(page_table, q_s, k_new, cache_ckv, jnp.swapaxes(cache_krope, 1, 2),
                       t_valid=TS, scale=scale)
    o_s = o_s.reshape(DB, n_bheads, TS, kv_lora).transpose(0, 2, 1, 3)
    o_s = jnp.pad(o_s, ((0, 0), (0, TP - TS), (0, 0), (0, 0))).reshape(n_s, n_bheads * kv_lora)
    o_lat = lax.dynamic_update_slice(o_lat, o_s.astype(BF16), (n_p, 0))

    w_uvh = jnp.transpose(w_uv, (1, 0, 2)).astype(BF16)
    w_rt = jnp.pad(w_router[0], ((0, 0), (0, LANES - n_experts))).astype(BF16)
    x3, xn_m, ids, wts = _attn_out(o_lat, x2, w_uvh, w_o_b[0].astype(BF16), g_ffn_m, w_rt,
                                   n_experts=n_experts)

    top_k = 2
    n2 = n * top_k
    assert n2 % MOE_TILE == 0
    order, pos, starts, ends = _route(ids[:, :top_k].reshape(-1), n_experts)
    row_token = order // top_k
    n_tiles = n2 // MOE_TILE
    parts = math.gcd(n_tiles, MOE_PARTS)
    part_tiles = n_tiles // parts
    weights = (wg_m.reshape(n_experts, D, f_m), wu_m.reshape(n_experts, D, f_m),
               wd_m.reshape(n_experts, f_m, D))
    ys = None
    for p in range(parts):
        t0 = p * part_tiles
        pairs = _tile_expert_pairs(starts, ends, t0, part_tiles, MOE_TILE,
                                   part_tiles + n_experts - 1)
        rows = lax.dynamic_slice_in_dim(row_token, t0 * MOE_TILE, part_tiles * MOE_TILE)
        ys = _moe(pairs, xn_m[rows], *weights, ys, t0, n2)
    pos = pos.reshape(n, top_k)
    y_p, y_s = _final(x3, ys[pos[:, 0]], ys[pos[:, 1]], wts, g_final.reshape(1, D), n_p)

    prompt = lambda a: a.reshape(B, T, a.shape[1])
    sample = lambda a: a.reshape(DB, TP, a.shape[1])[:, :TS]
    return (prompt(y_p), sample(y_s), c_p[None], n_pr[None], m_p[None], prompt(ckv_p), prompt(kr_p),
            c_s[None], n_sm[None], m_s[None], sample(ckv_s), sample(kr_s))
```

```python
import functools
import math

import jax
import jax.numpy as jnp
from jax import lax
from jax.experimental import pallas as pl
from jax.experimental.pallas import tpu as pltpu

F32 = jnp.float32
BF16 = jnp.bfloat16

EPS = 1e-6
GATE_CAP = 15.0
ROPE_THETA = 10000.0
NEG = -1e30
LANES = 128
VMEM_LIMIT_BYTES = 56 * 2**20

SAMPLE_PAD = 8
TOKEN_TILE = 512
ATTN_TILE = 512
ATTN_KEY_TILE = 256
MLSTM_CHUNK = 512
SHORT_SEQS_PER_STEP = 1
FFN_CHUNK = 1408
MOE_TILE = 512
MOE_CHUNK = 512
MOE_PARTS = 4
PAGES_PER_STEP = 64


def _cparams(*sem, flags=None):
    return pltpu.CompilerParams(dimension_semantics=sem, vmem_limit_bytes=VMEM_LIMIT_BYTES,
                                flags=flags)


def _dot(a, b):
    return jnp.dot(a, b, preferred_element_type=F32)


def _dot_nt(a, b):
    return lax.dot_general(a, b, (((1,), (1,)), ((), ())), preferred_element_type=F32)


def _rms(x, g):
    return x * lax.rsqrt(jnp.mean(x * x, axis=-1, keepdims=True) + EPS) * g


def _sigmoid(x):
    return 1.0 / (1.0 + jnp.exp(-x))


def _split3(x):
    hi = x.astype(BF16)
    r1 = x - hi.astype(F32)
    mid = r1.astype(BF16)
    lo = (r1 - mid.astype(F32)).astype(BF16)
    return hi, mid, lo


def _split_specs(tm, w, npb):
    return [pl.BlockSpec((tm, w), lambda i: (jnp.minimum(i, npb - 1), 0)),
            pl.BlockSpec((tm, w), lambda i: (jnp.maximum(i - npb, 0), 0))]


def _load_split(npb, p_ref, s_ref):
    return jnp.where(pl.program_id(0) < npb, p_ref[...], s_ref[...])


def _store_split(npb, p_ref, s_ref, val):
    i = pl.program_id(0)

    @pl.when(i < npb)
    def _():
        p_ref[...] = val

    @pl.when(i >= npb)
    def _():
        s_ref[...] = val


def _full(shape):
    nd = len(shape)
    return pl.BlockSpec(shape, lambda *_: (0,) * nd)


def _in_proj_kernel(xp_ref, xs_ref, g_ref, wq_ref, wk_ref, wv_ref, wo_ref, wg_ref, b_ref,
                    q_ref, k_ref, v_ref, o_ref, gate_ref, *, n_heads, q_scale, npb):
    xn = _rms(_load_split(npb, xp_ref, xs_ref), g_ref[...]).astype(BF16)
    q_ref[...] = _dot(xn, wq_ref[...]) * q_scale
    k_ref[...] = _dot(xn, wk_ref[...])
    v_ref[...] = _dot(xn, wv_ref[...])
    o_ref[...] = _dot(xn, wo_ref[...])
    g = _dot(xn, wg_ref[...]) + b_ref[...]
    g = GATE_CAP * jnp.tanh(g / GATE_CAP)
    logf = jnp.minimum(g, 0.0) - jnp.log1p(jnp.exp(-jnp.abs(g)))
    lane = lax.broadcasted_iota(jnp.int32, g.shape, 1)
    gate_ref[...] = jnp.where(lane < n_heads, g, jnp.where(lane < 2 * n_heads, logf, 0.0))


def _in_proj(xp, xs, g, wq, wk, wv, wo, wg, b, *, n_heads, q_scale):
    d = xp.shape[1]
    n = xp.shape[0] + xs.shape[0]
    tm = TOKEN_TILE
    npb = xp.shape[0] // tm
    row = lambda w: pl.BlockSpec((tm, w), lambda i: (i, 0))
    outs = [(wq.shape[1], F32), (wk.shape[1], F32), (wv.shape[1], F32), (wo.shape[1], F32), (LANES, F32)]
    return pl.pallas_call(
        functools.partial(_in_proj_kernel, n_heads=n_heads, q_scale=q_scale, npb=npb),
        grid=(n // tm,),
        in_specs=_split_specs(tm, d, npb) + [
            _full(g.shape), _full(wq.shape), _full(wk.shape), _full(wv.shape),
            _full(wo.shape), _full(wg.shape), _full(b.shape)],
        out_specs=[row(w) for w, _ in outs],
        out_shape=[jax.ShapeDtypeStruct((n, w), dt) for w, dt in outs],
        compiler_params=_cparams("parallel"),
        name="in_proj",
    )(xp, xs, g, wq, wk, wv, wo, wg, b)


def _mlstm_kernel(*refs, L, H, DK, DV, nb, t_valid, has_state, mm_dtype):
    if has_state:
        (q_ref, k_ref, v_ref, o_ref, gate_ref, gh_ref, c0_ref, n0_ref, m0_ref, _,
         hg_ref, cout_ref, nout_ref, mout_ref, caug_ref, m_scr) = refs
    else:
        (q_ref, k_ref, v_ref, o_ref, gate_ref, gh_ref,
         hg_ref, cout_ref, nout_ref, mout_ref, caug_ref, m_scr) = refs
    c = pl.program_id(1)
    last = pl.num_programs(1) - 1

    rk = lax.broadcasted_iota(jnp.int32, (DK, DK), 0)
    ck = lax.broadcasted_iota(jnp.int32, (DK, DK), 1)
    eye_k = rk == ck

    @pl.when(c == 0)
    def _():
        if has_state:
            for sh in range(nb * H):
                s, h = divmod(sh, H)
                caug_ref[sh, :, :DV] = c0_ref[s, h]
                n_row = n0_ref[s, h:h + 1, :]
                n_col = jnp.sum(jnp.where(eye_k, jnp.broadcast_to(n_row, (DK, DK)), 0.0),
                                axis=1, keepdims=True)
                caug_ref[sh, :, DV:] = jnp.broadcast_to(n_col, (DK, DV))
                m_scr[sh:sh + 1, :] = jnp.broadcast_to(m0_ref[s, :, h:h + 1], (1, LANES))
        else:
            caug_ref[...] = jnp.zeros_like(caug_ref)
            m_scr[...] = jnp.zeros_like(m_scr)

    ri = lax.broadcasted_iota(jnp.int32, (L, L), 0)
    ci = lax.broadcasted_iota(jnp.int32, (L, L), 1)
    causal = ci <= ri
    eye = ci == ri
    ones_v = jnp.ones((L, DV), mm_dtype)
    use_mxu_cumsum = L % LANES == 0

    def head(s, h, gates, cums):
        rows = slice(s * L, (s + 1) * L)
        sh = s * H + h
        if use_mxu_cumsum:
            gates_t, cum, cum_t = cums
            a_col = cum[:, H + h:H + h + 1]
            b_row = gates_t[h:h + 1, :] - cum_t[H + h:H + h + 1, :]
        else:
            li_col = gates[:, h:h + 1]
            lf_col = gates[:, H + h:H + h + 1]
            lf_b = jnp.broadcast_to(lf_col, (L, L))
            lf_row = jnp.sum(jnp.where(eye, lf_b, 0.0), axis=0, keepdims=True)
            a_col = jnp.sum(jnp.where(causal, jnp.broadcast_to(lf_row, (L, L)), 0.0),
                            axis=1, keepdims=True)
            a_row = jnp.sum(jnp.where(ci >= ri, lf_b, 0.0), axis=0, keepdims=True)
            li_row = jnp.sum(jnp.where(eye, jnp.broadcast_to(li_col, (L, L)), 0.0),
                             axis=0, keepdims=True)
            b_row = li_row - a_row
        d = jnp.where(causal, a_col + b_row, NEG)
        m_prev = m_scr[sh:sh + 1, 0:1]
        inter = a_col + m_prev
        m_t = jnp.maximum(inter, jnp.max(d, axis=1, keepdims=True))

        qh = q_ref[rows, h * DK:(h + 1) * DK].astype(mm_dtype)
        kh = k_ref[rows, h * DK:(h + 1) * DK].astype(mm_dtype)
        vaug = jnp.concatenate([v_ref[rows, h * DV:(h + 1) * DV].astype(mm_dtype), ones_v], axis=1)
        caug = caug_ref[sh]

        w = jnp.exp(d - m_t) * _dot_nt(qh, kh)
        e_inter = jnp.exp(inter - m_t)
        num = _dot(w.astype(mm_dtype), vaug) + e_inter * _dot(qh, caug.astype(mm_dtype))
        hh = num[:, :DV] / jnp.maximum(jnp.abs(num[:, DV:]), jnp.exp(-m_t))
        hn = _rms(hh, gh_ref[h:h + 1, :])
        hg_ref[rows, h * DV:(h + 1) * DV] = _sigmoid(o_ref[rows, h * DV:(h + 1) * DV]) * hn

        m_new = m_t[L - 1:L, :]
        a_last = a_col[L - 1:L, :]
        e_end = jnp.exp(a_last + b_row - m_new)
        e_carry = jnp.exp(a_last + m_prev - m_new)
        k_t = _dot_nt(eye_k.astype(mm_dtype), kh)
        caug_new = e_carry * caug + _dot((k_t * e_end).astype(mm_dtype), vaug)
        caug_ref[sh] = caug_new
        m_scr[sh:sh + 1, :] = jnp.broadcast_to(m_new, (1, LANES))

        @pl.when(c == last)
        def _():
            cout_ref[s, h] = caug_new[:, :DV]
            nout_ref[s, h:h + 1, :] = jnp.sum(jnp.where(eye_k, caug_new[:, DV:DV + DK], 0.0),
                                              axis=0, keepdims=True)
            mout_ref[s, :, h:h + 1] = m_new

    for s in range(nb):
        gates = gate_ref[s * L:(s + 1) * L, :]
        if t_valid < L:
            t_id = lax.broadcasted_iota(jnp.int32, gates.shape, 0)
            lane = lax.broadcasted_iota(jnp.int32, gates.shape, 1)
            gates = jnp.where(t_id < t_valid, gates, jnp.where(lane < H, NEG, 0.0))
        cums = None
        if use_mxu_cumsum:
            gates_t = gates.T
            cum = sum(_dot(causal.astype(BF16), p) for p in _split3(gates))
            cum_t = sum(_dot(p, (ri <= ci).astype(BF16)) for p in _split3(gates_t))
            cums = (gates_t, cum, cum_t)
        for h in range(H):
            head(s, h, gates, cums)


def _mlstm(q, k, v, o, gates, g_head, *, row0, B, T, L, t_valid, state=None, hg_prev=None):
    n = q.shape[0]
    H, DV = g_head.shape
    DK = q.shape[1] // H
    nc = T // L
    nb = SHORT_SEQS_PER_STEP if nc == 1 else 1
    assert B % nb == 0 and row0 % (nb * L) == 0
    blk0 = row0 // (nb * L)
    row = lambda w: pl.BlockSpec((nb * L, w), lambda b, c: (blk0 + b * nc + c, 0))
    c_spec = pl.BlockSpec((nb, H, DK, DV), lambda b, c: (b, 0, 0, 0))
    n_spec = pl.BlockSpec((nb, H, DK), lambda b, c: (b, 0, 0))
    m_spec = pl.BlockSpec((nb, 1, H), lambda b, c: (b, 0, 0))
    in_specs = [row(H * DK), row(H * DK), row(H * DV), row(H * DV), row(LANES), _full(g_head.shape)]
    args = [q, k, v, o, gates, g_head]
    aliases = {}
    if state is not None:
        c0, n0, m0 = state
        in_specs += [c_spec, n_spec, m_spec, pl.BlockSpec(memory_space=pl.ANY)]
        args += [c0, n0, m0.reshape(B, 1, H), hg_prev]
        aliases = {len(args) - 1: 0}
    out_shape = [jax.ShapeDtypeStruct((n, H * DV), F32),
                 jax.ShapeDtypeStruct((B, H, DK, DV), F32),
                 jax.ShapeDtypeStruct((B, H, DK), F32),
                 jax.ShapeDtypeStruct((B, 1, H), F32)]
    out_specs = [row(H * DV), c_spec, n_spec, m_spec]
    kern = functools.partial(_mlstm_kernel, L=L, H=H, DK=DK, DV=DV, nb=nb, t_valid=t_valid,
                             has_state=state is not None,
                             mm_dtype=BF16 if L % 16 == 0 else F32)
    hg, c_out, n_out, m_out = pl.pallas_call(
        kern, grid=(B // nb, nc), in_specs=in_specs, out_specs=out_specs, out_shape=out_shape,
        scratch_shapes=[pltpu.VMEM((nb * H, DK, 2 * DV), F32), pltpu.VMEM((nb * H, LANES), F32)],
        input_output_aliases=aliases,
        compiler_params=_cparams("parallel", "arbitrary"),
        name="mlstm_sample" if state is not None else "mlstm_prompt",
    )(*args)
    return hg, c_out, n_out, m_out.reshape(B, H)


def _ffn_kernel(xp_ref, xs_ref, hg_ref, wout_ref, g_ref, wg_ref, wu_ref, wd_ref, out_ref, acc_ref,
                *, npb):
    x1 = _load_split(npb, xp_ref, xs_ref) + _dot(hg_ref[...].astype(BF16), wout_ref[...])
    xn = _rms(x1, g_ref[...]).astype(BF16)
    acc_ref[...] = jnp.zeros_like(acc_ref)

    def body(c, carry):
        gate = _dot(xn, wg_ref[c])
        up = _dot(xn, wu_ref[c])
        hmid = (gate * _sigmoid(gate) * up).astype(BF16)
        acc_ref[...] += _dot(hmid, wd_ref[c])
        return carry

    lax.fori_loop(0, wg_ref.shape[0], body, 0)
    out_ref[...] = x1 + acc_ref[...]


def _ffn(xp, xs, hg, wout, g, wg, wu, wd):
    n, d = hg.shape
    tm = TOKEN_TILE
    npb = xp.shape[0] // tm
    row = pl.BlockSpec((tm, d), lambda i: (i, 0))
    return pl.pallas_call(
        functools.partial(_ffn_kernel, npb=npb), grid=(n // tm,),
        in_specs=_split_specs(tm, d, npb) + [
            row, _full(wout.shape), _full(g.shape), _full(wg.shape), _full(wu.shape),
            _full(wd.shape)],
        out_specs=row, out_shape=jax.ShapeDtypeStruct((n, d), F32),
        scratch_shapes=[pltpu.VMEM((tm, d), F32)],
        compiler_params=_cparams("parallel"),
        name="outproj_ffn",
    )(xp, xs, hg, wout, g, wg, wu, wd)


def _latq_kernel(x_ref, cos_ref, sin_ref, gkv_ref, wdkv_ref, gckv_ref, gnb_ref, wdq_ref, gq_ref,
                 wn_ref, wr_ref, wrr_ref, wuk_ref,
                 ckvp_ref, ckvs_ref, krp_ref, krs_ref, kcat_ref, kvt_ref, q_ref,
                 *, n_heads, kv_lora, rope_dim, nope_dim, npb):
    x = x_ref[...]
    xs = x * lax.rsqrt(jnp.mean(x * x, axis=-1, keepdims=True) + EPS)
    cos = cos_ref[...]
    sin = sin_ref[...]
    lane = lax.broadcasted_iota(jnp.int32, cos.shape, 1)
    lo = lane < rope_dim

    lat = _dot((xs * gkv_ref[...]).astype(BF16), wdkv_ref[...])
    ckv = _rms(lat[:, :kv_lora], gckv_ref[...])
    _store_split(npb, ckvp_ref, ckvs_ref, ckv)
    t = lat[:, kv_lora:] * jnp.where(lo, cos, sin)
    kr2 = t + pltpu.roll(t, rope_dim, axis=1)
    _store_split(npb, krp_ref, krs_ref, kr2[:, :rope_dim])
    kcat_ref[...] = jnp.concatenate([ckv, kr2], axis=1).astype(BF16)
    for r in range(kvt_ref.shape[0]):
        kvt_ref[r] = ckv[r * ATTN_KEY_TILE:(r + 1) * ATTN_KEY_TILE].T.astype(BF16)

    cq = _dot((xs * gnb_ref[...]).astype(BF16), wdq_ref[...])
    cqn = _rms(cq, gq_ref[...]).astype(BF16)
    qn = _dot(cqn, wn_ref[...]).astype(BF16)
    reps = n_heads * rope_dim // LANES
    cos_h = jnp.concatenate([cos] * reps, axis=1)
    sin_h = jnp.concatenate([sin] * reps, axis=1)
    qr = _dot(cqn, wr_ref[...]) * cos_h + _dot(cqn, wrr_ref[...]) * sin_h
    for h in range(n_heads):
        ql = _dot(qn[:, h * nope_dim:(h + 1) * nope_dim], wuk_ref[h])
        pair = qr[:, (h // 2) * LANES:(h // 2 + 1) * LANES]
        slot = jnp.where(lo if h % 2 == 0 else jnp.logical_not(lo), pair, 0.0)
        q_ref[0, h] = jnp.concatenate([ql, slot], axis=1).astype(BF16)


def _latq(x, cos_tab, sin_tab, n_prompt_blocks, blocks_per_seq, weights, *, n_heads, kv_lora,
          rope_dim, nope_dim):
    n, d = x.shape
    tm = ATTN_TILE
    kw = kv_lora + 2 * rope_dim

    def tab_map(i):
        return (jnp.where(i < n_prompt_blocks, i % blocks_per_seq, blocks_per_seq), 0)

    tab = pl.BlockSpec((tm, LANES), tab_map)
    row = lambda w: pl.BlockSpec((tm, w), lambda i: (i, 0))
    npb = n_prompt_blocks
    n_p, n_s = npb * tm, n - npb * tm
    return pl.pallas_call(
        functools.partial(_latq_kernel, n_heads=n_heads, kv_lora=kv_lora, rope_dim=rope_dim,
                          nope_dim=nope_dim, npb=npb),
        grid=(n // tm,),
        in_specs=[row(d), tab, tab] + [_full(w.shape) for w in weights],
        out_specs=_split_specs(tm, kv_lora, npb) + _split_specs(tm, rope_dim, npb) + [
            row(kw),
            pl.BlockSpec((tm // ATTN_KEY_TILE, kv_lora, ATTN_KEY_TILE), lambda i: (i, 0, 0)),
            pl.BlockSpec((1, n_heads, tm, kw), lambda i: (i, 0, 0, 0))],
        out_shape=[jax.ShapeDtypeStruct((n_p, kv_lora), F32),
                   jax.ShapeDtypeStruct((n_s, kv_lora), F32),
                   jax.ShapeDtypeStruct((n_p, rope_dim), F32),
                   jax.ShapeDtypeStruct((n_s, rope_dim), F32),
                   jax.ShapeDtypeStruct((n, kw), BF16),
                   jax.ShapeDtypeStruct((n // ATTN_KEY_TILE, kv_lora, ATTN_KEY_TILE), BF16),
                   jax.ShapeDtypeStruct((n // tm, n_heads, tm, kw), BF16)],
        compiler_params=_cparams("arbitrary"),
        name="latent_q",
    )(x, cos_tab, sin_tab, *weights)


def _attn_prompt_kernel(q_ref, k_ref, kt_ref, *refs, tq, tk, kv_lora, scale, n_cast):
    cast_in = refs[:n_cast]
    o_ref = refs[n_cast]
    cast_out = refs[n_cast + 1:2 * n_cast + 1]
    m_scr, l_scr, acc_scr, sa_scr, sb_scr = refs[2 * n_cast + 1:]
    for src, dst in zip(cast_in, cast_out):
        dst[...] = src[...].astype(dst.dtype)

    qi = pl.program_id(1)
    n_heads = q_ref.shape[1]
    m_scr[...] = jnp.full_like(m_scr, NEG)
    l_scr[...] = jnp.zeros_like(l_scr)
    acc_scr[...] = jnp.zeros_like(acc_scr)

    def scores(j, s_ref):
        kc = k_ref[pl.ds(pl.multiple_of(j * tk, tk), tk), :]
        for h in range(n_heads):
            s_ref[h] = _dot_nt(kc, q_ref[0, h])

    def consume(j, s_all, diag_offset):
        kt = kt_ref[j]
        if diag_offset is not None:
            key = lax.broadcasted_iota(jnp.int32, (tk, tq), 0) + diag_offset
            qry = lax.broadcasted_iota(jnp.int32, (tk, tq), 1)
            keep = key <= qry
        for h in range(n_heads):
            st = s_all[h] * scale
            if diag_offset is not None:
                st = jnp.where(keep, st, NEG)
            m_prev = m_scr[h]
            m_new = jnp.maximum(m_prev, jnp.max(st, axis=0, keepdims=True))
            p = jnp.exp(st - m_new)
            alpha = jnp.exp(m_prev - m_new)
            l_scr[h] = alpha * l_scr[h] + jnp.sum(p, axis=0, keepdims=True)
            acc_scr[h] = alpha * acc_scr[h] + _dot(kt, p.astype(BF16))
            m_scr[h] = m_new

    assert tq == 2 * tk

    def body(i, carry):
        j = 2 * i
        scores(j + 1, sb_scr)
        consume(j, sa_scr, None)
        scores(j + 2, sa_scr)
        consume(j + 1, sb_scr, None)
        return carry

    scores(0, sa_scr)
    lax.fori_loop(0, qi, body, 0)
    n_full = 2 * qi
    scores(n_full + 1, sb_scr)
    consume(n_full, sa_scr, 0)
    consume(n_full + 1, sb_scr, tk)
    for h in range(n_heads):
        o_ref[:, h * kv_lora:(h + 1) * kv_lora] = (acc_scr[h] / l_scr[h]).T.astype(BF16)


def _attn_prompt(q, kcat, kvt, n_rows, to_cast, *, B, T, kv_lora, scale):
    tq, tk = ATTN_TILE, ATTN_KEY_TILE
    _, n_heads, _, kw = q.shape
    nq = T // tq
    steps = B * nq
    cast_specs = []
    for w in to_cast:
        rows = w.shape[0] // steps
        assert rows * steps == w.shape[0] and rows % 16 == 0
        cast_specs.append(pl.BlockSpec((rows, w.shape[1]), lambda b, i: (b * nq + i, 0)))
    out = pl.pallas_call(
        functools.partial(_attn_prompt_kernel, tq=tq, tk=tk, kv_lora=kv_lora, scale=scale,
                          n_cast=len(to_cast)),
        grid=(B, nq),
        in_specs=[pl.BlockSpec((1, n_heads, tq, kw), lambda b, i: (b * nq + i, 0, 0, 0)),
                  pl.BlockSpec((T, kw), lambda b, i: (b, 0)),
                  pl.BlockSpec((T // tk, kv_lora, tk), lambda b, i: (b, 0, 0))] + cast_specs,
        out_specs=[pl.BlockSpec((tq, n_heads * kv_lora), lambda b, i: (b * nq + i, 0))] + cast_specs,
        out_shape=[jax.ShapeDtypeStruct((n_rows, n_heads * kv_lora), BF16)]
        + [jax.ShapeDtypeStruct(w.shape, BF16) for w in to_cast],
        scratch_shapes=[pltpu.VMEM((n_heads, 1, tq), F32), pltpu.VMEM((n_heads, 1, tq), F32),
                        pltpu.VMEM((n_heads, kv_lora, tq), F32),
                        pltpu.VMEM((n_heads, tk, tq), F32), pltpu.VMEM((n_heads, tk, tq), F32)],
        compiler_params=_cparams("parallel", "arbitrary"),
        name="attn_prompt",
    )(q, kcat, kvt, *to_cast)
    return out[0], out[1:]


def _attn_sample_kernel(pt_ref, q_ref, knew_ref, *refs, n_pages, t_valid, kv_lora, rope_dim, scale,
                        single_step):
    ck_refs = refs[:n_pages]
    kr_refs = refs[n_pages:2 * n_pages]
    o_ref, m_scr, l_scr, acc_scr = refs[2 * n_pages:]
    g = pl.program_id(1)
    q = q_ref[0]
    ql = q[:, :kv_lora]
    qr = q[:, kv_lora:]

    def new_token_scores():
        kn = knew_ref[0]
        ckn = kn[:, :kv_lora]
        sn = (_dot_nt(ql, ckn) + _dot_nt(qr, kn[:, kv_lora:kv_lora + rope_dim])) * scale
        t = lax.broadcasted_iota(jnp.int32, sn.shape, 0) % t_valid
        j = lax.broadcasted_iota(jnp.int32, sn.shape, 1)
        return jnp.where(j <= t, sn, NEG), ckn

    if single_step:
        cks = [r[0].astype(BF16) for r in ck_refs]
        sn, ckn = new_token_scores()
        s = jnp.concatenate(
            [(_dot_nt(ql, ck) + _dot(qr, kr[0].astype(BF16))) * scale
             for ck, kr in zip(cks, kr_refs)] + [sn], axis=1)
        p = jnp.exp(s - jnp.max(s, axis=1, keepdims=True))
        denom = jnp.sum(p, axis=1, keepdims=True)
        p = p.astype(BF16)
        values = cks + [ckn]
        page = cks[0].shape[0]
        pv = _dot(p[:, :page], values[0])
        for i in range(1, len(values)):
            pv += _dot(p[:, i * page:(i + 1) * page], values[i])
        o_ref[0] = pv / denom
        return

    @pl.when(g == 0)
    def _():
        m_scr[...] = jnp.full_like(m_scr, NEG)
        l_scr[...] = jnp.zeros_like(l_scr)
        acc_scr[...] = jnp.zeros_like(acc_scr)

    def update(s, values):
        m_prev = m_scr[...]
        m_new = jnp.maximum(m_prev, jnp.max(s, axis=1, keepdims=True))
        p = jnp.exp(s - m_new)
        alpha = jnp.exp(m_prev - m_new)
        l_scr[...] = alpha * l_scr[...] + jnp.sum(p, axis=1, keepdims=True)
        p = p.astype(BF16)
        pv = _dot(p[:, :values[0].shape[0]], values[0])
        for i in range(1, len(values)):
            rows = values[i].shape[0]
            pv += _dot(p[:, i * rows:(i + 1) * rows], values[i])
        acc_scr[...] = alpha * acc_scr[...] + pv
        m_scr[...] = m_new

    cks = [r[0].astype(BF16) for r in ck_refs]
    s = jnp.concatenate(
        [_dot_nt(ql, ck) + _dot(qr, kr[0].astype(BF16)) for ck, kr in zip(cks, kr_refs)],
        axis=1) * scale
    update(s, cks)

    @pl.when(g == pl.num_programs(1) - 1)
    def _():
        sn, ckn = new_token_scores()
        update(sn, [ckn])
        o_ref[0] = acc_scr[...] / l_scr[...]


def _attn_sample(page_table, q, knew, cache_ckv, cache_krope_t, *, t_valid, scale):
    nb, n_pages = page_table.shape
    _, rows, qw = q.shape
    _, page, kv_lora = cache_ckv.shape
    rope_dim = cache_krope_t.shape[1]
    pg = min(PAGES_PER_STEP, n_pages)
    assert n_pages % pg == 0
    ck_specs = [pl.BlockSpec((1, page, kv_lora),
                             functools.partial(lambda b, g, pt, i: (pt[b, g * pg + i], 0, 0), i=i))
                for i in range(pg)]
    kr_specs = [pl.BlockSpec((1, rope_dim, page),
                             functools.partial(lambda b, g, pt, i: (pt[b, g * pg + i], 0, 0), i=i))
                for i in range(pg)]
    grid_spec = pltpu.PrefetchScalarGridSpec(
        num_scalar_prefetch=1, grid=(nb, n_pages // pg),
        in_specs=[pl.BlockSpec((1, rows, qw), lambda b, g, pt: (b, 0, 0)),
                  pl.BlockSpec((1,) + knew.shape[1:], lambda b, g, pt: (b, 0, 0))]
        + ck_specs + kr_specs,
        out_specs=pl.BlockSpec((1, rows, kv_lora), lambda b, g, pt: (b, 0, 0)),
        scratch_shapes=[pltpu.VMEM((rows, 1), F32), pltpu.VMEM((rows, 1), F32),
                        pltpu.VMEM((rows, kv_lora), F32)])
    return pl.pallas_call(
        functools.partial(_attn_sample_kernel, n_pages=pg, t_valid=t_valid, kv_lora=kv_lora,
                          rope_dim=rope_dim, scale=scale, single_step=pg == n_pages),
        grid_spec=grid_spec,
        out_shape=jax.ShapeDtypeStruct((nb, rows, kv_lora), F32),
        compiler_params=_cparams("parallel", "arbitrary"),
        name="attn_sample",
    )(page_table, q, knew, *([cache_ckv] * pg), *([cache_krope_t] * pg))


def _attn_out_kernel(ol_ref, x_ref, wuv_ref, wo_ref, g_ref, wr_ref,
                     x3_ref, xn_ref, ids_ref, wts_ref, *, n_experts):
    n_heads, kv_lora, _ = wuv_ref.shape
    o = jnp.concatenate(
        [_dot(ol_ref[:, h * kv_lora:(h + 1) * kv_lora], wuv_ref[h]) for h in range(n_heads)],
        axis=1).astype(BF16)
    x3 = x_ref[...] + _dot(o, wo_ref[...])
    x3_ref[...] = x3
    xn = _rms(x3, g_ref[...]).astype(BF16)
    xn_ref[...] = xn
    logits = _dot(xn, wr_ref[...])
    lane = lax.broadcasted_iota(jnp.int32, logits.shape, 1)
    lane_f = lane.astype(F32)
    lg = jnp.where(lane < n_experts, logits, -jnp.inf)
    v1 = jnp.max(lg, axis=1, keepdims=True)
    i1 = jnp.min(jnp.where(lg == v1, lane_f, float(LANES)), axis=1, keepdims=True)
    lg2 = jnp.where(lane_f == i1, -jnp.inf, lg)
    v2 = jnp.max(lg2, axis=1, keepdims=True)
    i2 = jnp.min(jnp.where(lg2 == v2, lane_f, float(LANES)), axis=1, keepdims=True)
    e = jnp.exp(v2 - v1)
    w1 = 1.0 / (1.0 + e)
    w2 = e / (1.0 + e)
    ids_ref[...] = jnp.where(lane == 0, i1, jnp.where(lane == 1, i2, 0.0)).astype(jnp.int32)
    wts_ref[...] = jnp.where(lane == 0, w1, jnp.where(lane == 1, w2, 0.0))


def _attn_out(o_lat, x, wuv, wo, g, wr, *, n_experts):
    n, d = x.shape
    tm = TOKEN_TILE
    row = lambda w: pl.BlockSpec((tm, w), lambda i: (i, 0))
    return pl.pallas_call(
        functools.partial(_attn_out_kernel, n_experts=n_experts),
        grid=(n // tm,),
        in_specs=[row(o_lat.shape[1]), row(d), _full(wuv.shape), _full(wo.shape), _full(g.shape),
                  _full(wr.shape)],
        out_specs=[row(d), row(d), row(LANES), row(LANES)],
        out_shape=[jax.ShapeDtypeStruct((n, d), F32), jax.ShapeDtypeStruct((n, d), BF16),
                   jax.ShapeDtypeStruct((n, LANES), jnp.int32),
                   jax.ShapeDtypeStruct((n, LANES), F32)],
        compiler_params=_cparams("parallel"),
        name="attn_out_router",
    )(o_lat, x, wuv, wo, g, wr)


def _moe_kernel(pt_ref, pe_ref, lo_ref, hi_ref, first_ref, win_ref, xs_ref, wg_ref, wu_ref, wd_ref,
                *refs, chunk):
    out_ref = refs[-1]
    step = pl.program_id(0)
    i = win_ref[0] + step

    @pl.when(step < win_ref[1])
    def _():
        xs = xs_ref[...]
        acc = jnp.zeros(out_ref.shape, F32)
        for c in range(wg_ref.shape[2] // chunk):
            sl = slice(c * chunk, (c + 1) * chunk)
            gate = _dot(xs, wg_ref[0, :, sl])
            up = _dot(xs, wu_ref[0, :, sl])
            acc += _dot((gate * _sigmoid(gate) * up).astype(BF16), wd_ref[0, sl, :])
        res = acc.astype(out_ref.dtype)
        row = lax.broadcasted_iota(jnp.int32, out_ref.shape, 0)
        lo, hi = lo_ref[i], hi_ref[i]

        def keep_rows(base):
            return jnp.where(row >= lo, jnp.where(row < hi, res, base), base)

        @pl.when(first_ref[i] == 1)
        def _():
            out_ref[...] = keep_rows(jnp.zeros_like(res))

        @pl.when(first_ref[i] == 0)
        def _():
            out_ref[...] = keep_rows(out_ref[...])


def _moe(pairs, window, xs, wg, wu, wd, out_prev, t0, n_rows):
    d = xs.shape[1]
    tm = MOE_TILE
    n_experts = wg.shape[0]
    pair = lambda s, win: win[0] + jnp.minimum(s, win[1] - 1)
    expert = lambda s, pt, pe, lo, hi, first, win: (pe[pair(s, win)], 0, 0)
    w_specs = [pl.BlockSpec((1,) + w.shape[1:], expert, pipeline_mode=pl.Buffered(1))
               for w in (wg, wu, wd)]
    args = [xs, wg, wu, wd]
    in_specs = [pl.BlockSpec((tm, d), lambda s, pt, pe, lo, hi, first, win:
                             (pt[pair(s, win)] - t0, 0))] + w_specs
    aliases = {}
    if out_prev is not None:
        in_specs.append(pl.BlockSpec(memory_space=pl.ANY))
        args.append(out_prev)
        aliases = {len(pairs) + 1 + len(args) - 1: 0}
    grid_spec = pltpu.PrefetchScalarGridSpec(
        num_scalar_prefetch=len(pairs) + 1, grid=(xs.shape[0] // tm + n_experts - 1,),
        in_specs=in_specs,
        out_specs=pl.BlockSpec((tm, d), lambda s, pt, pe, lo, hi, first, win:
                               (pt[pair(s, win)], 0)))
    return pl.pallas_call(
        functools.partial(_moe_kernel, chunk=MOE_CHUNK),
        grid_spec=grid_spec,
        out_shape=jax.ShapeDtypeStruct((n_rows, d), BF16),
        input_output_aliases=aliases,
        compiler_params=_cparams("arbitrary"),
        name="moe_experts",
    )(*pairs, window, *args)


def _final_kernel(x_ref, y1_ref, y2_ref, wts_ref, g_ref, outp_ref, outs_ref, *, npb):
    w = wts_ref[...]
    x4 = x_ref[...] + (w[:, 0:1] * y1_ref[...].astype(F32) + w[:, 1:2] * y2_ref[...].astype(F32))
    _store_split(npb, outp_ref, outs_ref, _rms(x4, g_ref[...]))


def _final(x, y1, y2, wts, g, n_p):
    n, d = x.shape
    tm = TOKEN_TILE
    npb = n_p // tm
    row = lambda w: pl.BlockSpec((tm, w), lambda i: (i, 0))
    return pl.pallas_call(
        functools.partial(_final_kernel, npb=npb), grid=(n // tm,),
        in_specs=[row(d), row(d), row(d), row(LANES), _full(g.shape)],
        out_specs=_split_specs(tm, d, npb),
        out_shape=[jax.ShapeDtypeStruct((n_p, d), F32), jax.ShapeDtypeStruct((n - n_p, d), F32)],
        compiler_params=_cparams("arbitrary"),
        name="combine_final",
    )(x, y1, y2, wts, g)


def _rot_cols(w, half):
    return jnp.concatenate([-w[..., half:], w[..., :half]], axis=-1)


def _route(e1, e2, n_experts):
    n = e1.shape[0]
    ex = jnp.arange(n_experts, dtype=jnp.int32)[:, None]
    h1, h2 = e1[None, :] == ex, e2[None, :] == ex
    m = h1.astype(jnp.int32) + h2.astype(jnp.int32)
    counts = jnp.sum(m, axis=1)
    ends = jnp.cumsum(counts)
    starts = ends - counts
    row = starts[:, None] + jnp.cumsum(m, axis=1) - m
    pos1 = jnp.sum(jnp.where(h1, row, 0), axis=0)
    pos2 = jnp.sum(jnp.where(h2, row, 0), axis=0)
    shift = (2 * n - 1).bit_length()
    a1 = 2 * jnp.arange(n, dtype=jnp.int32)
    keys = jnp.concatenate([(e1 << shift) | a1, (e2 << shift) | (a1 + 1)])
    order = jnp.sort(keys) & ((1 << shift) - 1)
    return order, pos1, pos2, starts, ends


def _tile_expert_pairs(starts, ends, n_tiles, tile):
    n_experts = starts.shape[0]
    max_pairs = n_tiles + n_experts - 1
    base = jnp.arange(n_tiles, dtype=jnp.int32)[:, None] * tile
    lo = jnp.maximum(starts[None, :] - base, 0).reshape(-1)
    hi = jnp.minimum(ends[None, :] - base, tile).reshape(-1)
    hit = hi > lo
    n = jnp.sum(hit).astype(jnp.int32)
    idx = jnp.nonzero(hit, size=max_pairs, fill_value=0)[0].astype(jnp.int32)
    idx = jnp.where(jnp.arange(max_pairs) < n, idx, idx[n - 1])
    p_tile = idx // n_experts
    first = jnp.concatenate([jnp.ones((1,), jnp.int32),
                             (p_tile[1:] != p_tile[:-1]).astype(jnp.int32)])
    return (p_tile, idx % n_experts, lo[idx], hi[idx], first), n


def kernel(x_prompt, x_sample, state_C, state_n, state_m, cache_ckv, cache_krope, page_table, g_norm_a, w_in_a, b_gate_a, g_head_a, w_out_a, g_kv, w_dkv, g_ckv, w_uk, w_uv, g_norm_b, w_dq, g_q, w_uq, w_o_b, g_ffn_d, w_gate_d, w_up_d, w_down_d, g_ffn_m, w_router, w_gate_m, w_up_m, w_down_m, g_final):
    B, T, D = x_prompt.shape
    DB, TS, _ = x_sample.shape
    H, DV = g_head_a.shape[1:]
    DK = state_C.shape[3]
    kv_lora, n_bheads, nope_dim = w_uk.shape
    rope_dim = cache_krope.shape[2]
    page = cache_ckv.shape[1]
    past_len = page_table.shape[1] * page
    n_experts = w_router.shape[2]
    assert state_C.shape[0] == 1 and g_norm_b.shape[0] == 1 and g_ffn_d.shape[0] == 1
    assert TS <= SAMPLE_PAD and 2 * rope_dim == LANES and 2 * H <= LANES
    TP = SAMPLE_PAD
    n_p, n_s = B * T, DB * TP
    n = n_p + n_s
    assert n_p % TOKEN_TILE == 0 and n_s % TOKEN_TILE == 0 and T % ATTN_TILE == 0

    x_p = x_prompt.reshape(n_p, D)
    x_s = jnp.pad(x_sample, ((0, 0), (0, TP - TS), (0, 0))).reshape(n_s, D)

    w_in = w_in_a[0].astype(BF16)
    hq, hv = H * DK, H * DV
    wq, wk, wv, wo = (w_in[:, :hq], w_in[:, hq:2 * hq], w_in[:, 2 * hq:2 * hq + hv],
                      w_in[:, 2 * hq + hv:2 * hq + 2 * hv])
    wgate = jnp.pad(w_in[:, 2 * hq + 2 * hv:], ((0, 0), (0, LANES - 2 * H)))
    bgate = jnp.pad(b_gate_a[0], (0, LANES - 2 * H)).reshape(1, LANES)
    q, k, v, o, gates = _in_proj(x_p, x_s, g_norm_a, wq, wk, wv, wo, wgate, bgate,
                                 n_heads=H, q_scale=DK ** -0.5)
    hg, c_p, n_pr, m_p = _mlstm(q, k, v, o, gates, g_head_a[0], row0=0, B=B, T=T,
                                L=MLSTM_CHUNK, t_valid=MLSTM_CHUNK)
    hg, c_s, n_sm, m_s = _mlstm(q, k, v, o, gates, g_head_a[0], row0=n_p, B=DB, T=TP, L=TP,
                                t_valid=TS, state=(state_C[0], state_n[0], state_m[0]), hg_prev=hg)

    f = w_gate_d.shape[2]
    f_pad = -(-f // FFN_CHUNK) * FFN_CHUNK
    nch = f_pad // FFN_CHUNK
    col_chunks = lambda w: jnp.pad(w.astype(BF16), ((0, 0), (0, f_pad - f))).reshape(
        D, nch, FFN_CHUNK).transpose(1, 0, 2)
    wd_d = jnp.pad(w_down_d[0].astype(BF16), ((0, f_pad - f), (0, 0))).reshape(nch, FFN_CHUNK, D)
    x2 = _ffn(x_p, x_s, hg, w_out_a[0].astype(BF16), g_ffn_d, col_chunks(w_gate_d[0]),
              col_chunks(w_up_d[0]), wd_d)

    half = rope_dim // 2
    inv = ROPE_THETA ** (-jnp.arange(half, dtype=F32) / half)

    def tables(pos):
        ang = pos.astype(F32)[:, None] * inv[None, :]
        return (jnp.tile(jnp.cos(ang), (1, LANES // half)), jnp.tile(jnp.sin(ang), (1, LANES // half)))

    cos_p, sin_p = tables(jnp.arange(T, dtype=jnp.int32))
    cos_s, sin_s = tables(past_len + jnp.arange(TP, dtype=jnp.int32))
    reps = ATTN_TILE // TP
    cos_tab = jnp.concatenate([cos_p, jnp.tile(cos_s, (reps, 1))], axis=0)
    sin_tab = jnp.concatenate([sin_p, jnp.tile(sin_s, (reps, 1))], axis=0)

    w_kr = w_dkv[:, kv_lora:]
    wdkv = jnp.concatenate([w_dkv[:, :kv_lora], w_kr, _rot_cols(w_kr, half)], axis=1).astype(BF16)
    wuq = w_uq[0].reshape(-1, n_bheads, nope_dim + rope_dim)
    w_nope = wuq[:, :, :nope_dim].reshape(-1, n_bheads * nope_dim).astype(BF16)
    w_rope = wuq[:, :, nope_dim:]
    w_r = w_rope.reshape(-1, n_bheads * rope_dim).astype(BF16)
    w_rr = _rot_cols(w_rope, half).reshape(-1, n_bheads * rope_dim).astype(BF16)
    w_ukt = jnp.transpose(w_uk, (1, 2, 0)).astype(BF16)
    lat_weights = [g_kv.reshape(1, D), wdkv, g_ckv.reshape(1, kv_lora), g_norm_b,
                   w_dq[0].astype(BF16), g_q, w_nope, w_r, w_rr, w_ukt]
    ckv_p, ckv_s, kr_p, kr_s, kcat, kvt, qcat = _latq(
        x2, cos_tab, sin_tab, n_p // ATTN_TILE, T // ATTN_TILE, lat_weights, n_heads=n_bheads,
        kv_lora=kv_lora, rope_dim=rope_dim, nope_dim=nope_dim)

    scale = (nope_dim + rope_dim) ** -0.5
    f_m = w_gate_m.shape[3]
    o_lat, (wg_m, wu_m, wd_m) = _attn_prompt(
        qcat, kcat, kvt, n,
        [w_gate_m[0].reshape(n_experts * D, f_m), w_up_m[0].reshape(n_experts * D, f_m),
         w_down_m[0].reshape(n_experts * f_m, D)],
        B=B, T=T, kv_lora=kv_lora, scale=scale)

    kw = kv_lora + 2 * rope_dim
    q_s = qcat[n_p // ATTN_TILE:].reshape(-1, n_bheads, ATTN_TILE // TP, TP, kw)[:, :, :, :TS]
    q_s = q_s.transpose(0, 2, 1, 3, 4).reshape(DB, n_bheads * TS, kw)
    q_s = jnp.concatenate([q_s[..., :kv_lora],
                           q_s[..., kv_lora:kv_lora + rope_dim] + q_s[..., kv_lora + rope_dim:]], axis=-1)
    k_new = jnp.pad(kcat[n_p:].reshape(DB, TP, kw), ((0, 0), (0, LANES - TP), (0, 0)))
    o_s = _attn_sample(page_table, q_s, k_new, cache_ckv, jnp.swapaxes(cache_krope, 1, 2),
                       t_valid=TS, scale=scale)
    o_s = o_s.reshape(DB, n_bheads, TS, kv_lora).transpose(0, 2, 1, 3)
    o_s = jnp.pad(o_s, ((0, 0), (0, TP - TS), (0, 0), (0, 0))).reshape(n_s, n_bheads * kv_lora)
    o_lat = lax.dynamic_update_slice(o_lat, o_s.astype(BF16), (n_p, 0))

    w_uvh = jnp.transpose(w_uv, (1, 0, 2)).astype(BF16)
    w_rt = jnp.pad(w_router[0], ((0, 0), (0, LANES - n_experts))).astype(BF16)
    x3, xn_m, ids, wts = _attn_out(o_lat, x2, w_uvh, w_o_b[0].astype(BF16), g_ffn_m, w_rt,
                                   n_experts=n_experts)

    top_k = 2
    n2 = n * top_k
    assert n2 % MOE_TILE == 0
    order, pos1, pos2, starts, ends = _route(ids[:, 0], ids[:, 1], n_experts)
    row_token = order // top_k
    n_tiles = n2 // MOE_TILE
    parts = math.gcd(n_tiles, MOE_PARTS)
    part_tiles = n_tiles // parts
    weights = (wg_m.reshape(n_experts, D, f_m), wu_m.reshape(n_experts, D, f_m),
               wd_m.reshape(n_experts, f_m, D))
    pairs, n_pairs = _tile_expert_pairs(starts, ends, n_tiles, MOE_TILE)
    valid = jnp.arange(pairs[0].shape[0]) < n_pairs
    ys = None
    for p in range(parts):
        t0 = p * part_tiles
        before = jnp.sum(valid & (pairs[0] < t0))
        upto = jnp.sum(valid & (pairs[0] < t0 + part_tiles))
        window = jnp.stack([before, upto - before]).astype(jnp.int32)
        rows = row_token[t0 * MOE_TILE:(t0 + part_tiles) * MOE_TILE]
        ys = _moe(pairs, window, xn_m[rows], *weights, ys, t0, n2)
    y_p, y_s = _final(x3, ys[pos1], ys[pos2], wts, g_final.reshape(1, D), n_p)

    prompt = lambda a: a.reshape(B, T, a.shape[1])
    sample = lambda a: a.reshape(DB, TP, a.shape[1])[:, :TS]
    return (prompt(y_p), sample(y_s), c_p[None], n_pr[None], m_p[None], prompt(ckv_p), prompt(kr_p),
            c_s[None], n_sm[None], m_s[None], sample(ckv_s), sample(kr_s))
```

```python
import functools
import math

import jax
import jax.numpy as jnp
from jax import lax
from jax.experimental import pallas as pl
from jax.experimental.pallas import tpu as pltpu

F32 = jnp.float32
BF16 = jnp.bfloat16

EPS = 1e-6
GATE_CAP = 15.0
ROPE_THETA = 10000.0
NEG = -1e30
LANES = 128
VMEM_LIMIT_BYTES = 56 * 2**20

SAMPLE_PAD = 8
TOKEN_TILE = 512
ATTN_TILE = 512
ATTN_KEY_TILE = 256
MLSTM_CHUNK = 512
SHORT_SEQS_PER_STEP = 1
FFN_CHUNK = 1408
MOE_TILE = 512
MOE_CHUNK = 512
MOE_PARTS = 4
PAGES_PER_STEP = 64


def _cparams(*sem, flags=None):
    return pltpu.CompilerParams(dimension_semantics=sem, vmem_limit_bytes=VMEM_LIMIT_BYTES,
                                flags=flags)


def _dot(a, b):
    return jnp.dot(a, b, preferred_element_type=F32)


def _dot_nt(a, b):
    return lax.dot_general(a, b, (((1,), (1,)), ((), ())), preferred_element_type=F32)


def _rms(x, g):
    return x * lax.rsqrt(jnp.mean(x * x, axis=-1, keepdims=True) + EPS) * g


def _sigmoid(x):
    return 1.0 / (1.0 + jnp.exp(-x))


def _split3(x):
    hi = x.astype(BF16)
    r1 = x - hi.astype(F32)
    mid = r1.astype(BF16)
    lo = (r1 - mid.astype(F32)).astype(BF16)
    return hi, mid, lo


def _split_specs(tm, w, npb):
    return [pl.BlockSpec((tm, w), lambda i: (jnp.minimum(i, npb - 1), 0)),
            pl.BlockSpec((tm, w), lambda i: (jnp.maximum(i - npb, 0), 0))]


def _load_split(npb, p_ref, s_ref):
    return jnp.where(pl.program_id(0) < npb, p_ref[...], s_ref[...])


def _store_split(npb, p_ref, s_ref, val):
    i = pl.program_id(0)

    @pl.when(i < npb)
    def _():
        p_ref[...] = val

    @pl.when(i >= npb)
    def _():
        s_ref[...] = val


def _full(shape):
    nd = len(shape)
    return pl.BlockSpec(shape, lambda *_: (0,) * nd)


def _in_proj_kernel(xp_ref, xs_ref, g_ref, wq_ref, wk_ref, wv_ref, wo_ref, wg_ref, b_ref,
                    q_ref, k_ref, v_ref, o_ref, gate_ref, *, n_heads, q_scale, npb):
    xn = _rms(_load_split(npb, xp_ref, xs_ref), g_ref[...]).astype(BF16)
    q_ref[...] = _dot(xn, wq_ref[...]) * q_scale
    k_ref[...] = _dot(xn, wk_ref[...])
    v_ref[...] = _dot(xn, wv_ref[...])
    o_ref[...] = _dot(xn, wo_ref[...])
    g = _dot(xn, wg_ref[...]) + b_ref[...]
    g = GATE_CAP * jnp.tanh(g / GATE_CAP)
    logf = jnp.minimum(g, 0.0) - jnp.log1p(jnp.exp(-jnp.abs(g)))
    lane = lax.broadcasted_iota(jnp.int32, g.shape, 1)
    gate_ref[...] = jnp.where(lane < n_heads, g, jnp.where(lane < 2 * n_heads, logf, 0.0))


def _in_proj(xp, xs, g, wq, wk, wv, wo, wg, b, *, n_heads, q_scale):
    d = xp.shape[1]
    n = xp.shape[0] + xs.shape[0]
    tm = TOKEN_TILE
    npb = xp.shape[0] // tm
    row = lambda w: pl.BlockSpec((tm, w), lambda i: (i, 0))
    outs = [(wq.shape[1], F32), (wk.shape[1], F32), (wv.shape[1], F32), (wo.shape[1], F32), (LANES, F32)]
    return pl.pallas_call(
        functools.partial(_in_proj_kernel, n_heads=n_heads, q_scale=q_scale, npb=npb),
        grid=(n // tm,),
        in_specs=_split_specs(tm, d, npb) + [
            _full(g.shape), _full(wq.shape), _full(wk.shape), _full(wv.shape),
            _full(wo.shape), _full(wg.shape), _full(b.shape)],
        out_specs=[row(w) for w, _ in outs],
        out_shape=[jax.ShapeDtypeStruct((n, w), dt) for w, dt in outs],
        compiler_params=_cparams("parallel"),
        name="in_proj",
    )(xp, xs, g, wq, wk, wv, wo, wg, b)


def _mlstm_kernel(*refs, L, H, DK, DV, nb, t_valid, has_state, mm_dtype):
    if has_state:
        (q_ref, k_ref, v_ref, o_ref, gate_ref, gh_ref, c0_ref, n0_ref, m0_ref, _,
         hg_ref, cout_ref, nout_ref, mout_ref, caug_ref, m_scr) = refs
    else:
        (q_ref, k_ref, v_ref, o_ref, gate_ref, gh_ref,
         hg_ref, cout_ref, nout_ref, mout_ref, caug_ref, m_scr) = refs
    c = pl.program_id(1)
    last = pl.num_programs(1) - 1

    rk = lax.broadcasted_iota(jnp.int32, (DK, DK), 0)
    ck = lax.broadcasted_iota(jnp.int32, (DK, DK), 1)
    eye_k = rk == ck

    @pl.when(c == 0)
    def _():
        if has_state:
            for sh in range(nb * H):
                s, h = divmod(sh, H)
                caug_ref[sh, :, :DV] = c0_ref[s, h]
                n_row = n0_ref[s, h:h + 1, :]
                n_col = jnp.sum(jnp.where(eye_k, jnp.broadcast_to(n_row, (DK, DK)), 0.0),
                                axis=1, keepdims=True)
                caug_ref[sh, :, DV:] = jnp.broadcast_to(n_col, (DK, DV))
                m_scr[sh:sh + 1, :] = jnp.broadcast_to(m0_ref[s, :, h:h + 1], (1, LANES))
        else:
            caug_ref[...] = jnp.zeros_like(caug_ref)
            m_scr[...] = jnp.zeros_like(m_scr)

    ri = lax.broadcasted_iota(jnp.int32, (L, L), 0)
    ci = lax.broadcasted_iota(jnp.int32, (L, L), 1)
    causal = ci <= ri
    eye = ci == ri
    ones_v = jnp.ones((L, DV), mm_dtype)
    use_mxu_cumsum = L % LANES == 0

    def head(s, h, gates, cums):
        rows = slice(s * L, (s + 1) * L)
        sh = s * H + h
        if use_mxu_cumsum:
            gates_t, cum, cum_t = cums
            a_col = cum[:, H + h:H + h + 1]
            b_row = gates_t[h:h + 1, :] - cum_t[H + h:H + h + 1, :]
        else:
            li_col = gates[:, h:h + 1]
            lf_col = gates[:, H + h:H + h + 1]
            lf_b = jnp.broadcast_to(lf_col, (L, L))
            lf_row = jnp.sum(jnp.where(eye, lf_b, 0.0), axis=0, keepdims=True)
            a_col = jnp.sum(jnp.where(causal, jnp.broadcast_to(lf_row, (L, L)), 0.0),
                            axis=1, keepdims=True)
            a_row = jnp.sum(jnp.where(ci >= ri, lf_b, 0.0), axis=0, keepdims=True)
            li_row = jnp.sum(jnp.where(eye, jnp.broadcast_to(li_col, (L, L)), 0.0),
                             axis=0, keepdims=True)
            b_row = li_row - a_row
        d = jnp.where(causal, a_col + b_row, NEG)
        m_prev = m_scr[sh:sh + 1, 0:1]
        inter = a_col + m_prev
        m_t = jnp.maximum(inter, jnp.max(d, axis=1, keepdims=True))

        qh = q_ref[rows, h * DK:(h + 1) * DK].astype(mm_dtype)
        kh = k_ref[rows, h * DK:(h + 1) * DK].astype(mm_dtype)
        vaug = jnp.concatenate([v_ref[rows, h * DV:(h + 1) * DV].astype(mm_dtype), ones_v], axis=1)
        caug = caug_ref[sh]

        w = jnp.exp(d - m_t) * _dot_nt(qh, kh)
        e_inter = jnp.exp(inter - m_t)
        num = _dot(w.astype(mm_dtype), vaug) + e_inter * _dot(qh, caug.astype(mm_dtype))
        hh = num[:, :DV] / jnp.maximum(jnp.abs(num[:, DV:]), jnp.exp(-m_t))
        hn = _rms(hh, gh_ref[h:h + 1, :])
        hg_ref[rows, h * DV:(h + 1) * DV] = _sigmoid(o_ref[rows, h * DV:(h + 1) * DV]) * hn

        m_new = m_t[L - 1:L, :]
        a_last = a_col[L - 1:L, :]
        e_end = jnp.exp(a_last + b_row - m_new)
        e_carry = jnp.exp(a_last + m_prev - m_new)
        k_t = _dot_nt(eye_k.astype(mm_dtype), kh)
        caug_new = e_carry * caug + _dot((k_t * e_end).astype(mm_dtype), vaug)
        caug_ref[sh] = caug_new
        m_scr[sh:sh + 1, :] = jnp.broadcast_to(m_new, (1, LANES))

        @pl.when(c == last)
        def _():
            cout_ref[s, h] = caug_new[:, :DV]
            nout_ref[s, h:h + 1, :] = jnp.sum(jnp.where(eye_k, caug_new[:, DV:DV + DK], 0.0),
                                              axis=0, keepdims=True)
            mout_ref[s, :, h:h + 1] = m_new

    for s in range(nb):
        gates = gate_ref[s * L:(s + 1) * L, :]
        if t_valid < L:
            t_id = lax.broadcasted_iota(jnp.int32, gates.shape, 0)
            lane = lax.broadcasted_iota(jnp.int32, gates.shape, 1)
            gates = jnp.where(t_id < t_valid, gates, jnp.where(lane < H, NEG, 0.0))
        cums = None
        if use_mxu_cumsum:
            gates_t = gates.T
            cum = sum(_dot(causal.astype(BF16), p) for p in _split3(gates))
            cum_t = sum(_dot(p, (ri <= ci).astype(BF16)) for p in _split3(gates_t))
            cums = (gates_t, cum, cum_t)
        for h in range(H):
            head(s, h, gates, cums)


def _mlstm(q, k, v, o, gates, g_head, *, row0, B, T, L, t_valid, state=None, hg_prev=None):
    n = q.shape[0]
    H, DV = g_head.shape
    DK = q.shape[1] // H
    nc = T // L
    nb = SHORT_SEQS_PER_STEP if nc == 1 else 1
    assert B % nb == 0 and row0 % (nb * L) == 0
    blk0 = row0 // (nb * L)
    row = lambda w: pl.BlockSpec((nb * L, w), lambda b, c: (blk0 + b * nc + c, 0))
    c_spec = pl.BlockSpec((nb, H, DK, DV), lambda b, c: (b, 0, 0, 0))
    n_spec = pl.BlockSpec((nb, H, DK), lambda b, c: (b, 0, 0))
    m_spec = pl.BlockSpec((nb, 1, H), lambda b, c: (b, 0, 0))
    in_specs = [row(H * DK), row(H * DK), row(H * DV), row(H * DV), row(LANES), _full(g_head.shape)]
    args = [q, k, v, o, gates, g_head]
    aliases = {}
    if state is not None:
        c0, n0, m0 = state
        in_specs += [c_spec, n_spec, m_spec, pl.BlockSpec(memory_space=pl.ANY)]
        args += [c0, n0, m0.reshape(B, 1, H), hg_prev]
        aliases = {len(args) - 1: 0}
    out_shape = [jax.ShapeDtypeStruct((n, H * DV), F32),
                 jax.ShapeDtypeStruct((B, H, DK, DV), F32),
                 jax.ShapeDtypeStruct((B, H, DK), F32),
                 jax.ShapeDtypeStruct((B, 1, H), F32)]
    out_specs = [row(H * DV), c_spec, n_spec, m_spec]
    kern = functools.partial(_mlstm_kernel, L=L, H=H, DK=DK, DV=DV, nb=nb, t_valid=t_valid,
                             has_state=state is not None,
                             mm_dtype=BF16 if L % 16 == 0 else F32)
    hg, c_out, n_out, m_out = pl.pallas_call(
        kern, grid=(B // nb, nc), in_specs=in_specs, out_specs=out_specs, out_shape=out_shape,
        scratch_shapes=[pltpu.VMEM((nb * H, DK, 2 * DV), F32), pltpu.VMEM((nb * H, LANES), F32)],
        input_output_aliases=aliases,
        compiler_params=_cparams("parallel", "arbitrary"),
        name="mlstm_sample" if state is not None else "mlstm_prompt",
    )(*args)
    return hg, c_out, n_out, m_out.reshape(B, H)


def _ffn_kernel(xp_ref, xs_ref, hg_ref, wout_ref, g_ref, wg_ref, wu_ref, wd_ref, out_ref, acc_ref,
                *, npb):
    x1 = _load_split(npb, xp_ref, xs_ref) + _dot(hg_ref[...].astype(BF16), wout_ref[...])
    xn = _rms(x1, g_ref[...]).astype(BF16)
    acc_ref[...] = jnp.zeros_like(acc_ref)

    def body(c, carry):
        gate = _dot(xn, wg_ref[c])
        up = _dot(xn, wu_ref[c])
        hmid = (gate * _sigmoid(gate) * up).astype(BF16)
        acc_ref[...] += _dot(hmid, wd_ref[c])
        return carry

    lax.fori_loop(0, wg_ref.shape[0], body, 0)
    out_ref[...] = x1 + acc_ref[...]


def _ffn(xp, xs, hg, wout, g, wg, wu, wd):
    n, d = hg.shape
    tm = TOKEN_TILE
    npb = xp.shape[0] // tm
    row = pl.BlockSpec((tm, d), lambda i: (i, 0))
    return pl.pallas_call(
        functools.partial(_ffn_kernel, npb=npb), grid=(n // tm,),
        in_specs=_split_specs(tm, d, npb) + [
            row, _full(wout.shape), _full(g.shape), _full(wg.shape), _full(wu.shape),
            _full(wd.shape)],
        out_specs=row, out_shape=jax.ShapeDtypeStruct((n, d), F32),
        scratch_shapes=[pltpu.VMEM((tm, d), F32)],
        compiler_params=_cparams("parallel"),
        name="outproj_ffn",
    )(xp, xs, hg, wout, g, wg, wu, wd)


def _latq_kernel(x_ref, cos_ref, sin_ref, gkv_ref, wdkv_ref, gckv_ref, gnb_ref, wdq_ref, gq_ref,
                 wn_ref, wr_ref, wrr_ref, wuk_ref,
                 ckvp_ref, ckvs_ref, krp_ref, krs_ref, kcat_ref, kvt_ref, q_ref,
                 *, n_heads, kv_lora, rope_dim, nope_dim, npb):
    x = x_ref[...]
    xs = x * lax.rsqrt(jnp.mean(x * x, axis=-1, keepdims=True) + EPS)
    cos = cos_ref[...]
    sin = sin_ref[...]
    lane = lax.broadcasted_iota(jnp.int32, cos.shape, 1)
    lo = lane < rope_dim

    lat = _dot((xs * gkv_ref[...]).astype(BF16), wdkv_ref[...])
    ckv = _rms(lat[:, :kv_lora], gckv_ref[...])
    _store_split(npb, ckvp_ref, ckvs_ref, ckv)
    t = lat[:, kv_lora:] * jnp.where(lo, cos, sin)
    kr2 = t + pltpu.roll(t, rope_dim, axis=1)
    _store_split(npb, krp_ref, krs_ref, kr2[:, :rope_dim])
    kcat_ref[...] = jnp.concatenate([ckv, kr2], axis=1).astype(BF16)
    for r in range(kvt_ref.shape[0]):
        kvt_ref[r] = ckv[r * ATTN_KEY_TILE:(r + 1) * ATTN_KEY_TILE].T.astype(BF16)

    cq = _dot((xs * gnb_ref[...]).astype(BF16), wdq_ref[...])
    cqn = _rms(cq, gq_ref[...]).astype(BF16)
    qn = _dot(cqn, wn_ref[...]).astype(BF16)
    reps = n_heads * rope_dim // LANES
    cos_h = jnp.concatenate([cos] * reps, axis=1)
    sin_h = jnp.concatenate([sin] * reps, axis=1)
    qr = _dot(cqn, wr_ref[...]) * cos_h + _dot(cqn, wrr_ref[...]) * sin_h
    for h in range(n_heads):
        ql = _dot(qn[:, h * nope_dim:(h + 1) * nope_dim], wuk_ref[h])
        pair = qr[:, (h // 2) * LANES:(h // 2 + 1) * LANES]
        slot = jnp.where(lo if h % 2 == 0 else jnp.logical_not(lo), pair, 0.0)
        q_ref[0, h] = jnp.concatenate([ql, slot], axis=1).astype(BF16)


def _latq(x, cos_tab, sin_tab, n_prompt_blocks, blocks_per_seq, weights, *, n_heads, kv_lora,
          rope_dim, nope_dim):
    n, d = x.shape
    tm = ATTN_TILE
    kw = kv_lora + 2 * rope_dim

    def tab_map(i):
        return (jnp.where(i < n_prompt_blocks, i % blocks_per_seq, blocks_per_seq), 0)

    tab = pl.BlockSpec((tm, LANES), tab_map)
    row = lambda w: pl.BlockSpec((tm, w), lambda i: (i, 0))
    npb = n_prompt_blocks
    n_p, n_s = npb * tm, n - npb * tm
    return pl.pallas_call(
        functools.partial(_latq_kernel, n_heads=n_heads, kv_lora=kv_lora, rope_dim=rope_dim,
                          nope_dim=nope_dim, npb=npb),
        grid=(n // tm,),
        in_specs=[row(d), tab, tab] + [_full(w.shape) for w in weights],
        out_specs=_split_specs(tm, kv_lora, npb) + _split_specs(tm, rope_dim, npb) + [
            row(kw),
            pl.BlockSpec((tm // ATTN_KEY_TILE, kv_lora, ATTN_KEY_TILE), lambda i: (i, 0, 0)),
            pl.BlockSpec((1, n_heads, tm, kw), lambda i: (i, 0, 0, 0))],
        out_shape=[jax.ShapeDtypeStruct((n_p, kv_lora), F32),
                   jax.ShapeDtypeStruct((n_s, kv_lora), F32),
                   jax.ShapeDtypeStruct((n_p, rope_dim), F32),
                   jax.ShapeDtypeStruct((n_s, rope_dim), F32),
                   jax.ShapeDtypeStruct((n, kw), BF16),
                   jax.ShapeDtypeStruct((n // ATTN_KEY_TILE, kv_lora, ATTN_KEY_TILE), BF16),
                   jax.ShapeDtypeStruct((n // tm, n_heads, tm, kw), BF16)],
        compiler_params=_cparams("arbitrary"),
        name="latent_q",
    )(x, cos_tab, sin_tab, *weights)


def _attn_prompt_kernel(q_ref, k_ref, kt_ref, *refs, tq, tk, kv_lora, scale, n_cast):
    cast_in = refs[:n_cast]
    o_ref = refs[n_cast]
    cast_out = refs[n_cast + 1:2 * n_cast + 1]
    m_scr, l_scr, acc_scr, sa_scr, sb_scr = refs[2 * n_cast + 1:]
    for src, dst in zip(cast_in, cast_out):
        dst[...] = src[...].astype(dst.dtype)

    qi = pl.program_id(1)
    n_heads = q_ref.shape[1]
    m_scr[...] = jnp.full_like(m_scr, NEG)
    l_scr[...] = jnp.zeros_like(l_scr)
    acc_scr[...] = jnp.zeros_like(acc_scr)

    def scores(j, s_ref):
        kc = k_ref[pl.ds(pl.multiple_of(j * tk, tk), tk), :]
        for h in range(n_heads):
            s_ref[h] = _dot_nt(kc, q_ref[0, h])

    def consume(j, s_all, diag_offset):
        kt = kt_ref[j]
        if diag_offset is not None:
            key = lax.broadcasted_iota(jnp.int32, (tk, tq), 0) + diag_offset
            qry = lax.broadcasted_iota(jnp.int32, (tk, tq), 1)
            keep = key <= qry
        for h in range(n_heads):
            st = s_all[h] * scale
            if diag_offset is not None:
                st = jnp.where(keep, st, NEG)
            m_prev = m_scr[h]
            m_new = jnp.maximum(m_prev, jnp.max(st, axis=0, keepdims=True))
            p = jnp.exp(st - m_new)
            alpha = jnp.exp(m_prev - m_new)
            l_scr[h] = alpha * l_scr[h] + jnp.sum(p, axis=0, keepdims=True)
            acc_scr[h] = alpha * acc_scr[h] + _dot(kt, p.astype(BF16))
            m_scr[h] = m_new

    assert tq == 2 * tk

    def body(i, carry):
        j = 2 * i
        scores(j + 1, sb_scr)
        consume(j, sa_scr, None)
        scores(j + 2, sa_scr)
        consume(j + 1, sb_scr, None)
        return carry

    scores(0, sa_scr)
    lax.fori_loop(0, qi, body, 0)
    n_full = 2 * qi
    scores(n_full + 1, sb_scr)
    consume(n_full, sa_scr, 0)
    consume(n_full + 1, sb_scr, tk)
    for h in range(n_heads):
        o_ref[:, h * kv_lora:(h + 1) * kv_lora] = (acc_scr[h] / l_scr[h]).T.astype(BF16)


def _attn_prompt(q, kcat, kvt, n_rows, to_cast, *, B, T, kv_lora, scale):
    tq, tk = ATTN_TILE, ATTN_KEY_TILE
    _, n_heads, _, kw = q.shape
    nq = T // tq
    steps = B * nq
    cast_specs = []
    for w in to_cast:
        rows = w.shape[0] // steps
        assert rows * steps == w.shape[0] and rows % 16 == 0
        cast_specs.append(pl.BlockSpec((rows, w.shape[1]), lambda b, i: (b * nq + i, 0)))
    out = pl.pallas_call(
        functools.partial(_attn_prompt_kernel, tq=tq, tk=tk, kv_lora=kv_lora, scale=scale,
                          n_cast=len(to_cast)),
        grid=(B, nq),
        in_specs=[pl.BlockSpec((1, n_heads, tq, kw), lambda b, i: (b * nq + i, 0, 0, 0)),
                  pl.BlockSpec((T, kw), lambda b, i: (b, 0)),
                  pl.BlockSpec((T // tk, kv_lora, tk), lambda b, i: (b, 0, 0))] + cast_specs,
        out_specs=[pl.BlockSpec((tq, n_heads * kv_lora), lambda b, i: (b * nq + i, 0))] + cast_specs,
        out_shape=[jax.ShapeDtypeStruct((n_rows, n_heads * kv_lora), BF16)]
        + [jax.ShapeDtypeStruct(w.shape, BF16) for w in to_cast],
        scratch_shapes=[pltpu.VMEM((n_heads, 1, tq), F32), pltpu.VMEM((n_heads, 1, tq), F32),
                        pltpu.VMEM((n_heads, kv_lora, tq), F32),
                        pltpu.VMEM((n_heads, tk, tq), F32), pltpu.VMEM((n_heads, tk, tq), F32)],
        compiler_params=_cparams("parallel", "arbitrary"),
        name="attn_prompt",
    )(q, kcat, kvt, *to_cast)
    return out[0], out[1:]


def _attn_sample_kernel(pt_ref, q_ref, knew_ref, *refs, n_pages, t_valid, kv_lora, rope_dim, scale,
                        single_step):
    ck_refs = refs[:n_pages]
    kr_refs = refs[n_pages:2 * n_pages]
    o_ref, m_scr, l_scr, acc_scr = refs[2 * n_pages:]
    g = pl.program_id(1)
    q = q_ref[0]
    ql = q[:, :kv_lora]
    qr = q[:, kv_lora:]

    def new_token_scores():
        kn = knew_ref[0]
        ckn = kn[:, :kv_lora]
        sn = (_dot_nt(ql, ckn) + _dot_nt(qr, kn[:, kv_lora:kv_lora + rope_dim])) * scale
        t = lax.broadcasted_iota(jnp.int32, sn.shape, 0) % t_valid
        j = lax.broadcasted_iota(jnp.int32, sn.shape, 1)
        return jnp.where(j <= t, sn, NEG), ckn

    if single_step:
        cks = [r[0].astype(BF16) for r in ck_refs]
        sn, ckn = new_token_scores()
        s = jnp.concatenate(
            [(_dot_nt(ql, ck) + _dot(qr, kr[0].astype(BF16))) * scale
             for ck, kr in zip(cks, kr_refs)] + [sn], axis=1)
        p = jnp.exp(s - jnp.max(s, axis=1, keepdims=True))
        denom = jnp.sum(p, axis=1, keepdims=True)
        p = p.astype(BF16)
        values = cks + [ckn]
        page = cks[0].shape[0]
        pv = _dot(p[:, :page], values[0])
        for i in range(1, len(values)):
            pv += _dot(p[:, i * page:(i + 1) * page], values[i])
        o_ref[0] = pv / denom
        return

    @pl.when(g == 0)
    def _():
        m_scr[...] = jnp.full_like(m_scr, NEG)
        l_scr[...] = jnp.zeros_like(l_scr)
        acc_scr[...] = jnp.zeros_like(acc_scr)

    def update(s, values):
        m_prev = m_scr[...]
        m_new = jnp.maximum(m_prev, jnp.max(s, axis=1, keepdims=True))
        p = jnp.exp(s - m_new)
        alpha = jnp.exp(m_prev - m_new)
        l_scr[...] = alpha * l_scr[...] + jnp.sum(p, axis=1, keepdims=True)
        p = p.astype(BF16)
        pv = _dot(p[:, :values[0].shape[0]], values[0])
        for i in range(1, len(values)):
            rows = values[i].shape[0]
            pv += _dot(p[:, i * rows:(i + 1) * rows], values[i])
        acc_scr[...] = alpha * acc_scr[...] + pv
        m_scr[...] = m_new

    cks = [r[0].astype(BF16) for r in ck_refs]
    s = jnp.concatenate(
        [_dot_nt(ql, ck) + _dot(qr, kr[0].astype(BF16)) for ck, kr in zip(cks, kr_refs)],
        axis=1) * scale
    update(s, cks)

    @pl.when(g == pl.num_programs(1) - 1)
    def _():
        sn, ckn = new_token_scores()
        update(sn, [ckn])
        o_ref[0] = acc_scr[...] / l_scr[...]


def _attn_sample(page_table, q, knew, cache_ckv, cache_krope_t, *, t_valid, scale):
    nb, n_pages = page_table.shape
    _, rows, qw = q.shape
    _, page, kv_lora = cache_ckv.shape
    rope_dim = cache_krope_t.shape[1]
    pg = min(PAGES_PER_STEP, n_pages)
    assert n_pages % pg == 0
    ck_specs = [pl.BlockSpec((1, page, kv_lora),
                             functools.partial(lambda b, g, pt, i: (pt[b, g * pg + i], 0, 0), i=i))
                for i in range(pg)]
    kr_specs = [pl.BlockSpec((1, rope_dim, page),
                             functools.partial(lambda b, g, pt, i: (pt[b, g * pg + i], 0, 0), i=i))
                for i in range(pg)]
    grid_spec = pltpu.PrefetchScalarGridSpec(
        num_scalar_prefetch=1, grid=(nb, n_pages // pg),
        in_specs=[pl.BlockSpec((1, rows, qw), lambda b, g, pt: (b, 0, 0)),
                  pl.BlockSpec((1,) + knew.shape[1:], lambda b, g, pt: (b, 0, 0))]
        + ck_specs + kr_specs,
        out_specs=pl.BlockSpec((1, rows, kv_lora), lambda b, g, pt: (b, 0, 0)),
        scratch_shapes=[pltpu.VMEM((rows, 1), F32), pltpu.VMEM((rows, 1), F32),
                        pltpu.VMEM((rows, kv_lora), F32)])
    return pl.pallas_call(
        functools.partial(_attn_sample_kernel, n_pages=pg, t_valid=t_valid, kv_lora=kv_lora,
                          rope_dim=rope_dim, scale=scale, single_step=pg == n_pages),
        grid_spec=grid_spec,
        out_shape=jax.ShapeDtypeStruct((nb, rows, kv_lora), F32),
        compiler_params=_cparams("parallel", "arbitrary"),
        name="attn_sample",
    )(page_table, q, knew, *([cache_ckv] * pg), *([cache_krope_t] * pg))


def _attn_out_kernel(ol_ref, x_ref, wuv_ref, wo_ref, g_ref, wr_ref,
                     x3_ref, xn_ref, ids_ref, wts_ref, *, n_experts):
    n_heads, kv_lora, _ = wuv_ref.shape
    o = jnp.concatenate(
        [_dot(ol_ref[:, h * kv_lora:(h + 1) * kv_lora], wuv_ref[h]) for h in range(n_heads)],
        axis=1).astype(BF16)
    x3 = x_ref[...] + _dot(o, wo_ref[...])
    x3_ref[...] = x3
    xn = _rms(x3, g_ref[...]).astype(BF16)
    xn_ref[...] = xn
    logits = _dot(xn, wr_ref[...])
    lane = lax.broadcasted_iota(jnp.int32, logits.shape, 1)
    lane_f = lane.astype(F32)
    lg = jnp.where(lane < n_experts, logits, -jnp.inf)
    v1 = jnp.max(lg, axis=1, keepdims=True)
    i1 = jnp.min(jnp.where(lg == v1, lane_f, float(LANES)), axis=1, keepdims=True)
    lg2 = jnp.where(lane_f == i1, -jnp.inf, lg)
    v2 = jnp.max(lg2, axis=1, keepdims=True)
    i2 = jnp.min(jnp.where(lg2 == v2, lane_f, float(LANES)), axis=1, keepdims=True)
    e = jnp.exp(v2 - v1)
    w1 = 1.0 / (1.0 + e)
    w2 = e / (1.0 + e)
    ids_ref[...] = jnp.where(lane == 0, i1, jnp.where(lane == 1, i2, 0.0)).astype(jnp.int32)
    wts_ref[...] = jnp.where(lane == 0, w1, jnp.where(lane == 1, w2, 0.0))


def _attn_out(o_lat, x, wuv, wo, g, wr, *, n_experts):
    n, d = x.shape
    tm = TOKEN_TILE
    row = lambda w: pl.BlockSpec((tm, w), lambda i: (i, 0))
    return pl.pallas_call(
        functools.partial(_attn_out_kernel, n_experts=n_experts),
        grid=(n // tm,),
        in_specs=[row(o_lat.shape[1]), row(d), _full(wuv.shape), _full(wo.shape), _full(g.shape),
                  _full(wr.shape)],
        out_specs=[row(d), row(d), row(LANES), row(LANES)],
        out_shape=[jax.ShapeDtypeStruct((n, d), F32), jax.ShapeDtypeStruct((n, d), BF16),
                   jax.ShapeDtypeStruct((n, LANES), jnp.int32),
                   jax.ShapeDtypeStruct((n, LANES), F32)],
        compiler_params=_cparams("parallel"),
        name="attn_out_router",
    )(o_lat, x, wuv, wo, g, wr)


def _moe_kernel(pt_ref, pe_ref, lo_ref, hi_ref, first_ref, win_ref, xs_ref, wg_ref, wu_ref, wd_ref,
                *refs, chunk):
    out_ref = refs[-1]
    step = pl.program_id(0)
    i = win_ref[0] + step

    @pl.when(step < win_ref[1])
    def _():
        xs = xs_ref[...]
        acc = jnp.zeros(out_ref.shape, F32)
        for c in range(wg_ref.shape[2] // chunk):
            sl = slice(c * chunk, (c + 1) * chunk)
            gate = _dot(xs, wg_ref[0, :, sl])
            up = _dot(xs, wu_ref[0, :, sl])
            acc += _dot((gate * _sigmoid(gate) * up).astype(BF16), wd_ref[0, sl, :])
        res = acc.astype(out_ref.dtype)
        row = lax.broadcasted_iota(jnp.int32, out_ref.shape, 0)
        lo, hi = lo_ref[i], hi_ref[i]

        def keep_rows(base):
            return jnp.where(row >= lo, jnp.where(row < hi, res, base), base)

        @pl.when(first_ref[i] == 1)
        def _():
            out_ref[...] = keep_rows(jnp.zeros_like(res))

        @pl.when(first_ref[i] == 0)
        def _():
            out_ref[...] = keep_rows(out_ref[...])


def _moe(pairs, window, xs, wg, wu, wd, out_prev, t0, n_rows):
    d = xs.shape[1]
    tm = MOE_TILE
    n_experts = wg.shape[0]
    pair = lambda s, win: win[0] + jnp.minimum(s, win[1] - 1)
    expert = lambda s, pt, pe, lo, hi, first, win: (pe[pair(s, win)], 0, 0)
    w_specs = [pl.BlockSpec((1,) + w.shape[1:], expert, pipeline_mode=pl.Buffered(1))
               for w in (wg, wu, wd)]
    args = [xs, wg, wu, wd]
    in_specs = [pl.BlockSpec((tm, d), lambda s, pt, pe, lo, hi, first, win:
                             (pt[pair(s, win)] - t0, 0))] + w_specs
    aliases = {}
    if out_prev is not None:
        in_specs.append(pl.BlockSpec(memory_space=pl.ANY))
        args.append(out_prev)
        aliases = {len(pairs) + 1 + len(args) - 1: 0}
    grid_spec = pltpu.PrefetchScalarGridSpec(
        num_scalar_prefetch=len(pairs) + 1, grid=(xs.shape[0] // tm + n_experts - 1,),
        in_specs=in_specs,
        out_specs=pl.BlockSpec((tm, d), lambda s, pt, pe, lo, hi, first, win:
                               (pt[pair(s, win)], 0)))
    return pl.pallas_call(
        functools.partial(_moe_kernel, chunk=MOE_CHUNK),
        grid_spec=grid_spec,
        out_shape=jax.ShapeDtypeStruct((n_rows, d), BF16),
        input_output_aliases=aliases,
        compiler_params=_cparams("arbitrary"),
        name="moe_experts",
    )(*pairs, window, *args)


def _final_kernel(x_ref, y1_ref, y2_ref, wts_ref, g_ref, *refs, npb, t0, prompt_only):
    w = wts_ref[...]
    x4 = x_ref[...] + (w[:, 0:1] * y1_ref[...].astype(F32) + w[:, 1:2] * y2_ref[...].astype(F32))
    val = _rms(x4, g_ref[...])
    if prompt_only:
        refs[-1][...] = val
    else:
        _store_split(npb - t0, refs[-2], refs[-1], val)


def _final(x, y1, y2, wts, g, n_p, row0, prev_p):
    n, d = x.shape
    tm = TOKEN_TILE
    npb, t0, nt = n_p // tm, row0 // tm, y1.shape[0] // tm
    prompt_only = t0 + nt <= npb
    assert prompt_only or row0 + y1.shape[0] == n
    here = lambda w: pl.BlockSpec((tm, w), lambda i: (t0 + i, 0))
    local = pl.BlockSpec((tm, d), lambda i: (i, 0))
    in_specs = [here(d), local, local, here(LANES), _full(g.shape)]
    args = [x, y1, y2, wts, g]
    aliases = {}
    if prev_p is not None:
        in_specs.append(pl.BlockSpec(memory_space=pl.ANY))
        args.append(prev_p)
        aliases = {len(args) - 1: 0}
    p_shape = jax.ShapeDtypeStruct((n_p, d), F32)
    if prompt_only:
        out_specs, out_shape = here(d), p_shape
    else:
        out_specs = [pl.BlockSpec((tm, d), lambda i: (jnp.minimum(t0 + i, npb - 1), 0)),
                     pl.BlockSpec((tm, d), lambda i: (jnp.maximum(t0 + i - npb, 0), 0))]
        out_shape = [p_shape, jax.ShapeDtypeStruct((n - n_p, d), F32)]
    return pl.pallas_call(
        functools.partial(_final_kernel, npb=npb, t0=t0, prompt_only=prompt_only), grid=(nt,),
        in_specs=in_specs, out_specs=out_specs, out_shape=out_shape,
        input_output_aliases=aliases,
        compiler_params=_cparams("arbitrary"),
        name="combine_final",
    )(*args)


def _rot_cols(w, half):
    return jnp.concatenate([-w[..., half:], w[..., :half]], axis=-1)


def _route(e1, e2, n_experts):
    n = e1.shape[0]
    ex = jnp.arange(n_experts, dtype=jnp.int32)[:, None]
    h1, h2 = e1[None, :] == ex, e2[None, :] == ex
    m = h1.astype(jnp.int32) + h2.astype(jnp.int32)
    counts = jnp.sum(m, axis=1)
    ends = jnp.cumsum(counts)
    starts = ends - counts
    row = starts[:, None] + jnp.cumsum(m, axis=1) - m
    pos1 = jnp.sum(jnp.where(h1, row, 0), axis=0)
    pos2 = jnp.sum(jnp.where(h2, row, 0), axis=0)
    shift = (2 * n - 1).bit_length()
    a1 = 2 * jnp.arange(n, dtype=jnp.int32)
    keys = jnp.concatenate([(e1 << shift) | a1, (e2 << shift) | (a1 + 1)])
    order = jnp.sort(keys) & ((1 << shift) - 1)
    return order, pos1, pos2, starts, ends


def _tile_expert_pairs(starts, ends, n_tiles, tile):
    n_experts = starts.shape[0]
    max_pairs = n_tiles + n_experts - 1
    base = jnp.arange(n_tiles, dtype=jnp.int32)[:, None] * tile
    lo = jnp.maximum(starts[None, :] - base, 0).reshape(-1)
    hi = jnp.minimum(ends[None, :] - base, tile).reshape(-1)
    hit = hi > lo
    n = jnp.sum(hit).astype(jnp.int32)
    idx = jnp.nonzero(hit, size=max_pairs, fill_value=0)[0].astype(jnp.int32)
    idx = jnp.where(jnp.arange(max_pairs) < n, idx, idx[n - 1])
    p_tile = idx // n_experts
    first = jnp.concatenate([jnp.ones((1,), jnp.int32),
                             (p_tile[1:] != p_tile[:-1]).astype(jnp.int32)])
    return (p_tile, idx % n_experts, lo[idx], hi[idx], first), n


def kernel(x_prompt, x_sample, state_C, state_n, state_m, cache_ckv, cache_krope, page_table, g_norm_a, w_in_a, b_gate_a, g_head_a, w_out_a, g_kv, w_dkv, g_ckv, w_uk, w_uv, g_norm_b, w_dq, g_q, w_uq, w_o_b, g_ffn_d, w_gate_d, w_up_d, w_down_d, g_ffn_m, w_router, w_gate_m, w_up_m, w_down_m, g_final):
    B, T, D = x_prompt.shape
    DB, TS, _ = x_sample.shape
    H, DV = g_head_a.shape[1:]
    DK = state_C.shape[3]
    kv_lora, n_bheads, nope_dim = w_uk.shape
    rope_dim = cache_krope.shape[2]
    page = cache_ckv.shape[1]
    past_len = page_table.shape[1] * page
    n_experts = w_router.shape[2]
    assert state_C.shape[0] == 1 and g_norm_b.shape[0] == 1 and g_ffn_d.shape[0] == 1
    assert TS <= SAMPLE_PAD and 2 * rope_dim == LANES and 2 * H <= LANES
    TP = SAMPLE_PAD
    n_p, n_s = B * T, DB * TP
    n = n_p + n_s
    assert n_p % TOKEN_TILE == 0 and n_s % TOKEN_TILE == 0 and T % ATTN_TILE == 0

    x_p = x_prompt.reshape(n_p, D)
    x_s = jnp.pad(x_sample, ((0, 0), (0, TP - TS), (0, 0))).reshape(n_s, D)

    w_in = w_in_a[0].astype(BF16)
    hq, hv = H * DK, H * DV
    wq, wk, wv, wo = (w_in[:, :hq], w_in[:, hq:2 * hq], w_in[:, 2 * hq:2 * hq + hv],
                      w_in[:, 2 * hq + hv:2 * hq + 2 * hv])
    wgate = jnp.pad(w_in[:, 2 * hq + 2 * hv:], ((0, 0), (0, LANES - 2 * H)))
    bgate = jnp.pad(b_gate_a[0], (0, LANES - 2 * H)).reshape(1, LANES)
    q, k, v, o, gates = _in_proj(x_p, x_s, g_norm_a, wq, wk, wv, wo, wgate, bgate,
                                 n_heads=H, q_scale=DK ** -0.5)
    hg, c_p, n_pr, m_p = _mlstm(q, k, v, o, gates, g_head_a[0], row0=0, B=B, T=T,
                                L=MLSTM_CHUNK, t_valid=MLSTM_CHUNK)
    hg, c_s, n_sm, m_s = _mlstm(q, k, v, o, gates, g_head_a[0], row0=n_p, B=DB, T=TP, L=TP,
                                t_valid=TS, state=(state_C[0], state_n[0], state_m[0]), hg_prev=hg)

    f = w_gate_d.shape[2]
    f_pad = -(-f // FFN_CHUNK) * FFN_CHUNK
    nch = f_pad // FFN_CHUNK
    col_chunks = lambda w: jnp.pad(w.astype(BF16), ((0, 0), (0, f_pad - f))).reshape(
        D, nch, FFN_CHUNK).transpose(1, 0, 2)
    wd_d = jnp.pad(w_down_d[0].astype(BF16), ((0, f_pad - f), (0, 0))).reshape(nch, FFN_CHUNK, D)
    x2 = _ffn(x_p, x_s, hg, w_out_a[0].astype(BF16), g_ffn_d, col_chunks(w_gate_d[0]),
              col_chunks(w_up_d[0]), wd_d)

    half = rope_dim // 2
    inv = ROPE_THETA ** (-jnp.arange(half, dtype=F32) / half)

    def tables(pos):
        ang = pos.astype(F32)[:, None] * inv[None, :]
        return (jnp.tile(jnp.cos(ang), (1, LANES // half)), jnp.tile(jnp.sin(ang), (1, LANES // half)))

    cos_p, sin_p = tables(jnp.arange(T, dtype=jnp.int32))
    cos_s, sin_s = tables(past_len + jnp.arange(TP, dtype=jnp.int32))
    reps = ATTN_TILE // TP
    cos_tab = jnp.concatenate([cos_p, jnp.tile(cos_s, (reps, 1))], axis=0)
    sin_tab = jnp.concatenate([sin_p, jnp.tile(sin_s, (reps, 1))], axis=0)

    w_kr = w_dkv[:, kv_lora:]
    wdkv = jnp.concatenate([w_dkv[:, :kv_lora], w_kr, _rot_cols(w_kr, half)], axis=1).astype(BF16)
    wuq = w_uq[0].reshape(-1, n_bheads, nope_dim + rope_dim)
    w_nope = wuq[:, :, :nope_dim].reshape(-1, n_bheads * nope_dim).astype(BF16)
    w_rope = wuq[:, :, nope_dim:]
    w_r = w_rope.reshape(-1, n_bheads * rope_dim).astype(BF16)
    w_rr = _rot_cols(w_rope, half).reshape(-1, n_bheads * rope_dim).astype(BF16)
    w_ukt = jnp.transpose(w_uk, (1, 2, 0)).astype(BF16)
    lat_weights = [g_kv.reshape(1, D), wdkv, g_ckv.reshape(1, kv_lora), g_norm_b,
                   w_dq[0].astype(BF16), g_q, w_nope, w_r, w_rr, w_ukt]
    ckv_p, ckv_s, kr_p, kr_s, kcat, kvt, qcat = _latq(
        x2, cos_tab, sin_tab, n_p // ATTN_TILE, T // ATTN_TILE, lat_weights, n_heads=n_bheads,
        kv_lora=kv_lora, rope_dim=rope_dim, nope_dim=nope_dim)

    scale = (nope_dim + rope_dim) ** -0.5
    f_m = w_gate_m.shape[3]
    o_lat, (wg_m, wu_m, wd_m) = _attn_prompt(
        qcat, kcat, kvt, n,
        [w_gate_m[0].reshape(n_experts * D, f_m), w_up_m[0].reshape(n_experts * D, f_m),
         w_down_m[0].reshape(n_experts * f_m, D)],
        B=B, T=T, kv_lora=kv_lora, scale=scale)

    kw = kv_lora + 2 * rope_dim
    q_s = qcat[n_p // ATTN_TILE:].reshape(-1, n_bheads, ATTN_TILE // TP, TP, kw)[:, :, :, :TS]
    q_s = q_s.transpose(0, 2, 1, 3, 4).reshape(DB, n_bheads * TS, kw)
    q_s = jnp.concatenate([q_s[..., :kv_lora],
                           q_s[..., kv_lora:kv_lora + rope_dim] + q_s[..., kv_lora + rope_dim:]], axis=-1)
    k_new = jnp.pad(kcat[n_p:].reshape(DB, TP, kw), ((0, 0), (0, LANES - TP), (0, 0)))
    o_s = _attn_sample(page_table, q_s, k_new, cache_ckv, jnp.swapaxes(cache_krope, 1, 2),
                       t_valid=TS, scale=scale)
    o_s = o_s.reshape(DB, n_bheads, TS, kv_lora).transpose(0, 2, 1, 3)
    o_s = jnp.pad(o_s, ((0, 0), (0, TP - TS), (0, 0), (0, 0))).reshape(n_s, n_bheads * kv_lora)
    o_lat = lax.dynamic_update_slice(o_lat, o_s.astype(BF16), (n_p, 0))

    w_uvh = jnp.transpose(w_uv, (1, 0, 2)).astype(BF16)
    w_rt = jnp.pad(w_router[0], ((0, 0), (0, LANES - n_experts))).astype(BF16)
    x3, xn_m, ids, wts = _attn_out(o_lat, x2, w_uvh, w_o_b[0].astype(BF16), g_ffn_m, w_rt,
                                   n_experts=n_experts)

    top_k = 2
    n2 = n * top_k
    assert n2 % MOE_TILE == 0
    order, pos1, pos2, starts, ends = _route(ids[:, 0], ids[:, 1], n_experts)
    row_token = order // top_k
    n_tiles = n2 // MOE_TILE
    parts = math.gcd(n_tiles, MOE_PARTS)
    part_tiles = n_tiles // parts
    weights = (wg_m.reshape(n_experts, D, f_m), wu_m.reshape(n_experts, D, f_m),
               wd_m.reshape(n_experts, f_m, D))
    pairs, n_pairs = _tile_expert_pairs(starts, ends, n_tiles, MOE_TILE)
    valid = jnp.arange(pairs[0].shape[0]) < n_pairs
    ys = None
    for p in range(parts):
        t0 = p * part_tiles
        before = jnp.sum(valid & (pairs[0] < t0))
        upto = jnp.sum(valid & (pairs[0] < t0 + part_tiles))
        window = jnp.stack([before, upto - before]).astype(jnp.int32)
        rows = row_token[t0 * MOE_TILE:(t0 + part_tiles) * MOE_TILE]
        ys = _moe(pairs, window, xn_m[rows], *weights, ys, t0, n2)
    g_fin = g_final.reshape(1, D)
    half = n_p // TOKEN_TILE // 2 * TOKEN_TILE
    y_p = _final(x3, ys[pos1[:half]], ys[pos2[:half]], wts, g_fin, n_p, 0, None)
    y_p, y_s = _final(x3, ys[pos1[half:]], ys[pos2[half:]], wts, g_fin, n_p, half, y_p)

    prompt = lambda a: a.reshape(B, T, a.shape[1])
    sample = lambda a: a.reshape(DB, TP, a.shape[1])[:, :TS]
    return (prompt(y_p), sample(y_s), c_p[None], n_pr[None], m_p[None], prompt(ckv_p), prompt(kr_p),
            c_s[None], n_sm[None], m_s[None], sample(ckv_s), sample(kr_s))
```

```python
import functools
import math

import jax
import jax.numpy as jnp
from jax import lax
from jax.experimental import pallas as pl
from jax.experimental.pallas import tpu as pltpu

F32 = jnp.float32
BF16 = jnp.bfloat16

EPS = 1e-6
GATE_CAP = 15.0
ROPE_THETA = 10000.0
NEG = -1e30
LANES = 128
VMEM_LIMIT_BYTES = 56 * 2**20

SAMPLE_PAD = 8
TOKEN_TILE = 512
ATTN_TILE = 512
ATTN_KEY_TILE = 256
MLSTM_CHUNK = 512
SHORT_SEQS_PER_STEP = 1
FFN_CHUNK = 1408
MOE_TILE = 512
MOE_CHUNK = 512
MOE_PARTS = 4
PAGES_PER_STEP = 64


def _cparams(*sem, flags=None):
    return pltpu.CompilerParams(dimension_semantics=sem, vmem_limit_bytes=VMEM_LIMIT_BYTES,
                                flags=flags)


def _dot(a, b):
    return jnp.dot(a, b, preferred_element_type=F32)


def _dot_nt(a, b):
    return lax.dot_general(a, b, (((1,), (1,)), ((), ())), preferred_element_type=F32)


def _rms(x, g):
    return x * lax.rsqrt(jnp.mean(x * x, axis=-1, keepdims=True) + EPS) * g


def _sigmoid(x):
    return 1.0 / (1.0 + jnp.exp(-x))


def _split3(x):
    hi = x.astype(BF16)
    r1 = x - hi.astype(F32)
    mid = r1.astype(BF16)
    lo = (r1 - mid.astype(F32)).astype(BF16)
    return hi, mid, lo


def _split_specs(tm, w, npb):
    return [pl.BlockSpec((tm, w), lambda i: (jnp.minimum(i, npb - 1), 0)),
            pl.BlockSpec((tm, w), lambda i: (jnp.maximum(i - npb, 0), 0))]


def _load_split(npb, p_ref, s_ref):
    return jnp.where(pl.program_id(0) < npb, p_ref[...], s_ref[...])


def _store_split(npb, p_ref, s_ref, val):
    i = pl.program_id(0)

    @pl.when(i < npb)
    def _():
        p_ref[...] = val

    @pl.when(i >= npb)
    def _():
        s_ref[...] = val


def _full(shape):
    nd = len(shape)
    return pl.BlockSpec(shape, lambda *_: (0,) * nd)


def _in_proj_kernel(xp_ref, xs_ref, g_ref, wq_ref, wk_ref, wv_ref, wo_ref, wg_ref, b_ref,
                    q_ref, k_ref, v_ref, o_ref, gate_ref, *, n_heads, q_scale, npb):
    xn = _rms(_load_split(npb, xp_ref, xs_ref), g_ref[...]).astype(BF16)
    q_ref[...] = _dot(xn, wq_ref[...]) * q_scale
    k_ref[...] = _dot(xn, wk_ref[...])
    v_ref[...] = _dot(xn, wv_ref[...])
    o_ref[...] = _dot(xn, wo_ref[...])
    g = _dot(xn, wg_ref[...]) + b_ref[...]
    g = GATE_CAP * jnp.tanh(g / GATE_CAP)
    logf = jnp.minimum(g, 0.0) - jnp.log1p(jnp.exp(-jnp.abs(g)))
    lane = lax.broadcasted_iota(jnp.int32, g.shape, 1)
    gate_ref[...] = jnp.where(lane < n_heads, g, jnp.where(lane < 2 * n_heads, logf, 0.0))


def _in_proj(xp, xs, g, wq, wk, wv, wo, wg, b, *, n_heads, q_scale):
    d = xp.shape[1]
    n = xp.shape[0] + xs.shape[0]
    tm = TOKEN_TILE
    npb = xp.shape[0] // tm
    row = lambda w: pl.BlockSpec((tm, w), lambda i: (i, 0))
    outs = [(wq.shape[1], F32), (wk.shape[1], F32), (wv.shape[1], F32), (wo.shape[1], F32), (LANES, F32)]
    return pl.pallas_call(
        functools.partial(_in_proj_kernel, n_heads=n_heads, q_scale=q_scale, npb=npb),
        grid=(n // tm,),
        in_specs=_split_specs(tm, d, npb) + [
            _full(g.shape), _full(wq.shape), _full(wk.shape), _full(wv.shape),
            _full(wo.shape), _full(wg.shape), _full(b.shape)],
        out_specs=[row(w) for w, _ in outs],
        out_shape=[jax.ShapeDtypeStruct((n, w), dt) for w, dt in outs],
        compiler_params=_cparams("parallel"),
        name="in_proj",
    )(xp, xs, g, wq, wk, wv, wo, wg, b)


def _mlstm_kernel(*refs, L, H, DK, DV, nb, t_valid, has_state, mm_dtype):
    if has_state:
        (q_ref, k_ref, v_ref, o_ref, gate_ref, gh_ref, c0_ref, n0_ref, m0_ref, _,
         hg_ref, cout_ref, nout_ref, mout_ref, caug_ref, m_scr) = refs
    else:
        (q_ref, k_ref, v_ref, o_ref, gate_ref, gh_ref,
         hg_ref, cout_ref, nout_ref, mout_ref, caug_ref, m_scr) = refs
    c = pl.program_id(1)
    last = pl.num_programs(1) - 1

    rk = lax.broadcasted_iota(jnp.int32, (DK, DK), 0)
    ck = lax.broadcasted_iota(jnp.int32, (DK, DK), 1)
    eye_k = rk == ck

    @pl.when(c == 0)
    def _():
        if has_state:
            for sh in range(nb * H):
                s, h = divmod(sh, H)
                caug_ref[sh, :, :DV] = c0_ref[s, h]
                n_row = n0_ref[s, h:h + 1, :]
                n_col = jnp.sum(jnp.where(eye_k, jnp.broadcast_to(n_row, (DK, DK)), 0.0),
                                axis=1, keepdims=True)
                caug_ref[sh, :, DV:] = jnp.broadcast_to(n_col, (DK, DV))
                m_scr[sh:sh + 1, :] = jnp.broadcast_to(m0_ref[s, :, h:h + 1], (1, LANES))
        else:
            caug_ref[...] = jnp.zeros_like(caug_ref)
            m_scr[...] = jnp.zeros_like(m_scr)

    ri = lax.broadcasted_iota(jnp.int32, (L, L), 0)
    ci = lax.broadcasted_iota(jnp.int32, (L, L), 1)
    causal = ci <= ri
    eye = ci == ri
    ones_v = jnp.ones((L, DV), mm_dtype)
    use_mxu_cumsum = L % LANES == 0

    def head(s, h, gates, cums):
        rows = slice(s * L, (s + 1) * L)
        sh = s * H + h
        if use_mxu_cumsum:
            gates_t, cum, cum_t = cums
            a_col = cum[:, H + h:H + h + 1]
            b_row = gates_t[h:h + 1, :] - cum_t[H + h:H + h + 1, :]
        else:
            li_col = gates[:, h:h + 1]
            lf_col = gates[:, H + h:H + h + 1]
            lf_b = jnp.broadcast_to(lf_col, (L, L))
            lf_row = jnp.sum(jnp.where(eye, lf_b, 0.0), axis=0, keepdims=True)
            a_col = jnp.sum(jnp.where(causal, jnp.broadcast_to(lf_row, (L, L)), 0.0),
                            axis=1, keepdims=True)
            a_row = jnp.sum(jnp.where(ci >= ri, lf_b, 0.0), axis=0, keepdims=True)
            li_row = jnp.sum(jnp.where(eye, jnp.broadcast_to(li_col, (L, L)), 0.0),
                             axis=0, keepdims=True)
            b_row = li_row - a_row
        d = jnp.where(causal, a_col + b_row, NEG)
        m_prev = m_scr[sh:sh + 1, 0:1]
        inter = a_col + m_prev
        m_t = jnp.maximum(inter, jnp.max(d, axis=1, keepdims=True))

        qh = q_ref[rows, h * DK:(h + 1) * DK].astype(mm_dtype)
        kh = k_ref[rows, h * DK:(h + 1) * DK].astype(mm_dtype)
        vaug = jnp.concatenate([v_ref[rows, h * DV:(h + 1) * DV].astype(mm_dtype), ones_v], axis=1)
        caug = caug_ref[sh]

        w = jnp.exp(d - m_t) * _dot_nt(qh, kh)
        e_inter = jnp.exp(inter - m_t)
        num = _dot(w.astype(mm_dtype), vaug) + e_inter * _dot(qh, caug.astype(mm_dtype))
        hh = num[:, :DV] / jnp.maximum(jnp.abs(num[:, DV:]), jnp.exp(-m_t))
        hn = _rms(hh, gh_ref[h:h + 1, :])
        hg_ref[rows, h * DV:(h + 1) * DV] = _sigmoid(o_ref[rows, h * DV:(h + 1) * DV]) * hn

        m_new = m_t[L - 1:L, :]
        a_last = a_col[L - 1:L, :]
        e_end = jnp.exp(a_last + b_row - m_new)
        e_carry = jnp.exp(a_last + m_prev - m_new)
        k_t = _dot_nt(eye_k.astype(mm_dtype), kh)
        caug_new = e_carry * caug + _dot((k_t * e_end).astype(mm_dtype), vaug)
        caug_ref[sh] = caug_new
        m_scr[sh:sh + 1, :] = jnp.broadcast_to(m_new, (1, LANES))

        @pl.when(c == last)
        def _():
            cout_ref[s, h] = caug_new[:, :DV]
            nout_ref[s, h:h + 1, :] = jnp.sum(jnp.where(eye_k, caug_new[:, DV:DV + DK], 0.0),
                                              axis=0, keepdims=True)
            mout_ref[s, :, h:h + 1] = m_new

    for s in range(nb):
        gates = gate_ref[s * L:(s + 1) * L, :]
        if t_valid < L:
            t_id = lax.broadcasted_iota(jnp.int32, gates.shape, 0)
            lane = lax.broadcasted_iota(jnp.int32, gates.shape, 1)
            gates = jnp.where(t_id < t_valid, gates, jnp.where(lane < H, NEG, 0.0))
        cums = None
        if use_mxu_cumsum:
            gates_t = gates.T
            cum = sum(_dot(causal.astype(BF16), p) for p in _split3(gates))
            cum_t = sum(_dot(p, (ri <= ci).astype(BF16)) for p in _split3(gates_t))
            cums = (gates_t, cum, cum_t)
        for h in range(H):
            head(s, h, gates, cums)


def _mlstm(q, k, v, o, gates, g_head, *, row0, B, T, L, t_valid, state=None, hg_prev=None):
    n = q.shape[0]
    H, DV = g_head.shape
    DK = q.shape[1] // H
    nc = T // L
    nb = SHORT_SEQS_PER_STEP if nc == 1 else 1
    assert B % nb == 0 and row0 % (nb * L) == 0
    blk0 = row0 // (nb * L)
    row = lambda w: pl.BlockSpec((nb * L, w), lambda b, c: (blk0 + b * nc + c, 0))
    c_spec = pl.BlockSpec((nb, H, DK, DV), lambda b, c: (b, 0, 0, 0))
    n_spec = pl.BlockSpec((nb, H, DK), lambda b, c: (b, 0, 0))
    m_spec = pl.BlockSpec((nb, 1, H), lambda b, c: (b, 0, 0))
    in_specs = [row(H * DK), row(H * DK), row(H * DV), row(H * DV), row(LANES), _full(g_head.shape)]
    args = [q, k, v, o, gates, g_head]
    aliases = {}
    if state is not None:
        c0, n0, m0 = state
        in_specs += [c_spec, n_spec, m_spec, pl.BlockSpec(memory_space=pl.ANY)]
        args += [c0, n0, m0.reshape(B, 1, H), hg_prev]
        aliases = {len(args) - 1: 0}
    out_shape = [jax.ShapeDtypeStruct((n, H * DV), F32),
                 jax.ShapeDtypeStruct((B, H, DK, DV), F32),
                 jax.ShapeDtypeStruct((B, H, DK), F32),
                 jax.ShapeDtypeStruct((B, 1, H), F32)]
    out_specs = [row(H * DV), c_spec, n_spec, m_spec]
    kern = functools.partial(_mlstm_kernel, L=L, H=H, DK=DK, DV=DV, nb=nb, t_valid=t_valid,
                             has_state=state is not None,
                             mm_dtype=BF16 if L % 16 == 0 else F32)
    hg, c_out, n_out, m_out = pl.pallas_call(
        kern, grid=(B // nb, nc), in_specs=in_specs, out_specs=out_specs, out_shape=out_shape,
        scratch_shapes=[pltpu.VMEM((nb * H, DK, 2 * DV), F32), pltpu.VMEM((nb * H, LANES), F32)],
        input_output_aliases=aliases,
        compiler_params=_cparams("parallel", "arbitrary"),
        name="mlstm_sample" if state is not None else "mlstm_prompt",
    )(*args)
    return hg, c_out, n_out, m_out.reshape(B, H)


def _ffn_kernel(xp_ref, xs_ref, hg_ref, wout_ref, g_ref, wg_ref, wu_ref, wd_ref, out_ref, acc_ref,
                *, npb):
    x1 = _load_split(npb, xp_ref, xs_ref) + _dot(hg_ref[...].astype(BF16), wout_ref[...])
    xn = _rms(x1, g_ref[...]).astype(BF16)
    acc_ref[...] = jnp.zeros_like(acc_ref)

    def body(c, carry):
        gate = _dot(xn, wg_ref[c])
        up = _dot(xn, wu_ref[c])
        hmid = (gate * _sigmoid(gate) * up).astype(BF16)
        acc_ref[...] += _dot(hmid, wd_ref[c])
        return carry

    lax.fori_loop(0, wg_ref.shape[0], body, 0)
    out_ref[...] = x1 + acc_ref[...]


def _ffn(xp, xs, hg, wout, g, wg, wu, wd):
    n, d = hg.shape
    tm = TOKEN_TILE
    npb = xp.shape[0] // tm
    row = pl.BlockSpec((tm, d), lambda i: (i, 0))
    return pl.pallas_call(
        functools.partial(_ffn_kernel, npb=npb), grid=(n // tm,),
        in_specs=_split_specs(tm, d, npb) + [
            row, _full(wout.shape), _full(g.shape), _full(wg.shape), _full(wu.shape),
            _full(wd.shape)],
        out_specs=row, out_shape=jax.ShapeDtypeStruct((n, d), F32),
        scratch_shapes=[pltpu.VMEM((tm, d), F32)],
        compiler_params=_cparams("parallel"),
        name="outproj_ffn",
    )(xp, xs, hg, wout, g, wg, wu, wd)


def _latq_kernel(x_ref, cos_ref, sin_ref, gkv_ref, wdkv_ref, gckv_ref, gnb_ref, wdq_ref, gq_ref,
                 wn_ref, wr_ref, wrr_ref, wuk_ref,
                 ckvp_ref, ckvs_ref, krp_ref, krs_ref, kcat_ref, kvt_ref, q_ref,
                 *, n_heads, kv_lora, rope_dim, nope_dim, npb):
    x = x_ref[...]
    xs = x * lax.rsqrt(jnp.mean(x * x, axis=-1, keepdims=True) + EPS)
    cos = cos_ref[...]
    sin = sin_ref[...]
    lane = lax.broadcasted_iota(jnp.int32, cos.shape, 1)
    lo = lane < rope_dim

    lat = _dot((xs * gkv_ref[...]).astype(BF16), wdkv_ref[...])
    ckv = _rms(lat[:, :kv_lora], gckv_ref[...])
    _store_split(npb, ckvp_ref, ckvs_ref, ckv)
    t = lat[:, kv_lora:] * jnp.where(lo, cos, sin)
    kr2 = t + pltpu.roll(t, rope_dim, axis=1)
    _store_split(npb, krp_ref, krs_ref, kr2[:, :rope_dim])
    kcat_ref[...] = jnp.concatenate([ckv, kr2], axis=1).astype(BF16)
    for r in range(kvt_ref.shape[0]):
        kvt_ref[r] = ckv[r * ATTN_KEY_TILE:(r + 1) * ATTN_KEY_TILE].T.astype(BF16)

    cq = _dot((xs * gnb_ref[...]).astype(BF16), wdq_ref[...])
    cqn = _rms(cq, gq_ref[...]).astype(BF16)
    qn = _dot(cqn, wn_ref[...]).astype(BF16)
    reps = n_heads * rope_dim // LANES
    cos_h = jnp.concatenate([cos] * reps, axis=1)
    sin_h = jnp.concatenate([sin] * reps, axis=1)
    qr = _dot(cqn, wr_ref[...]) * cos_h + _dot(cqn, wrr_ref[...]) * sin_h
    for h in range(n_heads):
        ql = _dot(qn[:, h * nope_dim:(h + 1) * nope_dim], wuk_ref[h])
        pair = qr[:, (h // 2) * LANES:(h // 2 + 1) * LANES]
        slot = jnp.where(lo if h % 2 == 0 else jnp.logical_not(lo), pair, 0.0)
        q_ref[0, h] = jnp.concatenate([ql, slot], axis=1).astype(BF16)


def _latq(x, cos_tab, sin_tab, n_prompt_blocks, blocks_per_seq, weights, *, n_heads, kv_lora,
          rope_dim, nope_dim):
    n, d = x.shape
    tm = ATTN_TILE
    kw = kv_lora + 2 * rope_dim

    def tab_map(i):
        return (jnp.where(i < n_prompt_blocks, i % blocks_per_seq, blocks_per_seq), 0)

    tab = pl.BlockSpec((tm, LANES), tab_map)
    row = lambda w: pl.BlockSpec((tm, w), lambda i: (i, 0))
    npb = n_prompt_blocks
    n_p, n_s = npb * tm, n - npb * tm
    return pl.pallas_call(
        functools.partial(_latq_kernel, n_heads=n_heads, kv_lora=kv_lora, rope_dim=rope_dim,
                          nope_dim=nope_dim, npb=npb),
        grid=(n // tm,),
        in_specs=[row(d), tab, tab] + [_full(w.shape) for w in weights],
        out_specs=_split_specs(tm, kv_lora, npb) + _split_specs(tm, rope_dim, npb) + [
            row(kw),
            pl.BlockSpec((tm // ATTN_KEY_TILE, kv_lora, ATTN_KEY_TILE), lambda i: (i, 0, 0)),
            pl.BlockSpec((1, n_heads, tm, kw), lambda i: (i, 0, 0, 0))],
        out_shape=[jax.ShapeDtypeStruct((n_p, kv_lora), F32),
                   jax.ShapeDtypeStruct((n_s, kv_lora), F32),
                   jax.ShapeDtypeStruct((n_p, rope_dim), F32),
                   jax.ShapeDtypeStruct((n_s, rope_dim), F32),
                   jax.ShapeDtypeStruct((n, kw), BF16),
                   jax.ShapeDtypeStruct((n // ATTN_KEY_TILE, kv_lora, ATTN_KEY_TILE), BF16),
                   jax.ShapeDtypeStruct((n // tm, n_heads, tm, kw), BF16)],
        compiler_params=_cparams("arbitrary"),
        name="latent_q",
    )(x, cos_tab, sin_tab, *weights)


def _attn_prompt_kernel(q_ref, k_ref, kt_ref, *refs, tq, tk, kv_lora, scale, n_cast):
    cast_in = refs[:n_cast]
    o_ref = refs[n_cast]
    cast_out = refs[n_cast + 1:2 * n_cast + 1]
    m_scr, l_scr, acc_scr, sa_scr, sb_scr = refs[2 * n_cast + 1:]
    for src, dst in zip(cast_in, cast_out):
        dst[...] = src[...].astype(dst.dtype)

    qi = pl.program_id(1)
    n_heads = q_ref.shape[1]
    m_scr[...] = jnp.full_like(m_scr, NEG)
    l_scr[...] = jnp.zeros_like(l_scr)
    acc_scr[...] = jnp.zeros_like(acc_scr)

    def scores(j, s_ref):
        kc = k_ref[pl.ds(pl.multiple_of(j * tk, tk), tk), :]
        for h in range(n_heads):
            s_ref[h] = _dot_nt(kc, q_ref[0, h])

    def consume(j, s_all, diag_offset):
        kt = kt_ref[j]
        if diag_offset is not None:
            key = lax.broadcasted_iota(jnp.int32, (tk, tq), 0) + diag_offset
            qry = lax.broadcasted_iota(jnp.int32, (tk, tq), 1)
            keep = key <= qry
        for h in range(n_heads):
            st = s_all[h] * scale
            if diag_offset is not None:
                st = jnp.where(keep, st, NEG)
            m_prev = m_scr[h]
            m_new = jnp.maximum(m_prev, jnp.max(st, axis=0, keepdims=True))
            p = jnp.exp(st - m_new)
            alpha = jnp.exp(m_prev - m_new)
            l_scr[h] = alpha * l_scr[h] + jnp.sum(p, axis=0, keepdims=True)
            acc_scr[h] = alpha * acc_scr[h] + _dot(kt, p.astype(BF16))
            m_scr[h] = m_new

    assert tq == 2 * tk

    def body(i, carry):
        j = 2 * i
        scores(j + 1, sb_scr)
        consume(j, sa_scr, None)
        scores(j + 2, sa_scr)
        consume(j + 1, sb_scr, None)
        return carry

    scores(0, sa_scr)
    lax.fori_loop(0, qi, body, 0)
    n_full = 2 * qi
    scores(n_full + 1, sb_scr)
    consume(n_full, sa_scr, 0)
    consume(n_full + 1, sb_scr, tk)
    for h in range(n_heads):
        o_ref[:, h * kv_lora:(h + 1) * kv_lora] = (acc_scr[h] / l_scr[h]).T.astype(BF16)


def _attn_prompt(q, kcat, kvt, n_rows, to_cast, *, B, T, kv_lora, scale):
    tq, tk = ATTN_TILE, ATTN_KEY_TILE
    _, n_heads, _, kw = q.shape
    nq = T // tq
    steps = B * nq
    cast_specs = []
    for w in to_cast:
        rows = w.shape[0] // steps
        assert rows * steps == w.shape[0] and rows % 16 == 0
        cast_specs.append(pl.BlockSpec((rows, w.shape[1]), lambda b, i: (b * nq + i, 0)))
    out = pl.pallas_call(
        functools.partial(_attn_prompt_kernel, tq=tq, tk=tk, kv_lora=kv_lora, scale=scale,
                          n_cast=len(to_cast)),
        grid=(B, nq),
        in_specs=[pl.BlockSpec((1, n_heads, tq, kw), lambda b, i: (b * nq + i, 0, 0, 0)),
                  pl.BlockSpec((T, kw), lambda b, i: (b, 0)),
                  pl.BlockSpec((T // tk, kv_lora, tk), lambda b, i: (b, 0, 0))] + cast_specs,
        out_specs=[pl.BlockSpec((tq, n_heads * kv_lora), lambda b, i: (b * nq + i, 0))] + cast_specs,
        out_shape=[jax.ShapeDtypeStruct((n_rows, n_heads * kv_lora), BF16)]
        + [jax.ShapeDtypeStruct(w.shape, BF16) for w in to_cast],
        scratch_shapes=[pltpu.VMEM((n_heads, 1, tq), F32), pltpu.VMEM((n_heads, 1, tq), F32),
                        pltpu.VMEM((n_heads, kv_lora, tq), F32),
                        pltpu.VMEM((n_heads, tk, tq), F32), pltpu.VMEM((n_heads, tk, tq), F32)],
        compiler_params=_cparams("parallel", "arbitrary"),
        name="attn_prompt",
    )(q, kcat, kvt, *to_cast)
    return out[0], out[1:]


def _attn_sample_kernel(pt_ref, q_ref, knew_ref, *refs, n_pages, t_valid, kv_lora, rope_dim, scale,
                        single_step):
    ck_refs = refs[:n_pages]
    kr_refs = refs[n_pages:2 * n_pages]
    o_ref, m_scr, l_scr, acc_scr = refs[2 * n_pages:]
    g = pl.program_id(1)
    q = q_ref[0]
    ql = q[:, :kv_lora]
    qr = q[:, kv_lora:]

    def new_token_scores():
        kn = knew_ref[0]
        ckn = kn[:, :kv_lora]
        sn = (_dot_nt(ql, ckn) + _dot_nt(qr, kn[:, kv_lora:kv_lora + rope_dim])) * scale
        t = lax.broadcasted_iota(jnp.int32, sn.shape, 0) % t_valid
        j = lax.broadcasted_iota(jnp.int32, sn.shape, 1)
        return jnp.where(j <= t, sn, NEG), ckn

    if single_step:
        cks = [r[0].astype(BF16) for r in ck_refs]
        sn, ckn = new_token_scores()
        s = jnp.concatenate(
            [(_dot_nt(ql, ck) + _dot(qr, kr[0].astype(BF16))) * scale
             for ck, kr in zip(cks, kr_refs)] + [sn], axis=1)
        p = jnp.exp(s - jnp.max(s, axis=1, keepdims=True))
        denom = jnp.sum(p, axis=1, keepdims=True)
        p = p.astype(BF16)
        values = cks + [ckn]
        page = cks[0].shape[0]
        pv = _dot(p[:, :page], values[0])
        for i in range(1, len(values)):
            pv += _dot(p[:, i * page:(i + 1) * page], values[i])
        o_ref[0] = pv / denom
        return

    @pl.when(g == 0)
    def _():
        m_scr[...] = jnp.full_like(m_scr, NEG)
        l_scr[...] = jnp.zeros_like(l_scr)
        acc_scr[...] = jnp.zeros_like(acc_scr)

    def update(s, values):
        m_prev = m_scr[...]
        m_new = jnp.maximum(m_prev, jnp.max(s, axis=1, keepdims=True))
        p = jnp.exp(s - m_new)
        alpha = jnp.exp(m_prev - m_new)
        l_scr[...] = alpha * l_scr[...] + jnp.sum(p, axis=1, keepdims=True)
        p = p.astype(BF16)
        pv = _dot(p[:, :values[0].shape[0]], values[0])
        for i in range(1, len(values)):
            rows = values[i].shape[0]
            pv += _dot(p[:, i * rows:(i + 1) * rows], values[i])
        acc_scr[...] = alpha * acc_scr[...] + pv
        m_scr[...] = m_new

    cks = [r[0].astype(BF16) for r in ck_refs]
    s = jnp.concatenate(
        [_dot_nt(ql, ck) + _dot(qr, kr[0].astype(BF16)) for ck, kr in zip(cks, kr_refs)],
        axis=1) * scale
    update(s, cks)

    @pl.when(g == pl.num_programs(1) - 1)
    def _():
        sn, ckn = new_token_scores()
        update(sn, [ckn])
        o_ref[0] = acc_scr[...] / l_scr[...]


def _attn_sample(page_table, q, knew, cache_ckv, cache_krope_t, *, t_valid, scale):
    nb, n_pages = page_table.shape
    _, rows, qw = q.shape
    _, page, kv_lora = cache_ckv.shape
    rope_dim = cache_krope_t.shape[1]
    pg = min(PAGES_PER_STEP, n_pages)
    assert n_pages % pg == 0
    ck_specs = [pl.BlockSpec((1, page, kv_lora),
                             functools.partial(lambda b, g, pt, i: (pt[b, g * pg + i], 0, 0), i=i))
                for i in range(pg)]
    kr_specs = [pl.BlockSpec((1, rope_dim, page),
                             functools.partial(lambda b, g, pt, i: (pt[b, g * pg + i], 0, 0), i=i))
                for i in range(pg)]
    grid_spec = pltpu.PrefetchScalarGridSpec(
        num_scalar_prefetch=1, grid=(nb, n_pages // pg),
        in_specs=[pl.BlockSpec((1, rows, qw), lambda b, g, pt: (b, 0, 0)),
                  pl.BlockSpec((1,) + knew.shape[1:], lambda b, g, pt: (b, 0, 0))]
        + ck_specs + kr_specs,
        out_specs=pl.BlockSpec((1, rows, kv_lora), lambda b, g, pt: (b, 0, 0)),
        scratch_shapes=[pltpu.VMEM((rows, 1), F32), pltpu.VMEM((rows, 1), F32),
                        pltpu.VMEM((rows, kv_lora), F32)])
    return pl.pallas_call(
        functools.partial(_attn_sample_kernel, n_pages=pg, t_valid=t_valid, kv_lora=kv_lora,
                          rope_dim=rope_dim, scale=scale, single_step=pg == n_pages),
        grid_spec=grid_spec,
        out_shape=jax.ShapeDtypeStruct((nb, rows, kv_lora), F32),
        compiler_params=_cparams("parallel", "arbitrary"),
        name="attn_sample",
    )(page_table, q, knew, *([cache_ckv] * pg), *([cache_krope_t] * pg))


def _attn_out_kernel(ol_ref, x_ref, wuv_ref, wo_ref, g_ref, wr_ref,
                     x3_ref, xn_ref, ids_ref, wts_ref, *, n_experts):
    n_heads, kv_lora, _ = wuv_ref.shape
    o = jnp.concatenate(
        [_dot(ol_ref[:, h * kv_lora:(h + 1) * kv_lora], wuv_ref[h]) for h in range(n_heads)],
        axis=1).astype(BF16)
    x3 = x_ref[...] + _dot(o, wo_ref[...])
    x3_ref[...] = x3
    xn = _rms(x3, g_ref[...]).astype(BF16)
    xn_ref[...] = xn
    logits = _dot(xn, wr_ref[...])
    lane = lax.broadcasted_iota(jnp.int32, logits.shape, 1)
    lane_f = lane.astype(F32)
    lg = jnp.where(lane < n_experts, logits, -jnp.inf)
    v1 = jnp.max(lg, axis=1, keepdims=True)
    i1 = jnp.min(jnp.where(lg == v1, lane_f, float(LANES)), axis=1, keepdims=True)
    lg2 = jnp.where(lane_f == i1, -jnp.inf, lg)
    v2 = jnp.max(lg2, axis=1, keepdims=True)
    i2 = jnp.min(jnp.where(lg2 == v2, lane_f, float(LANES)), axis=1, keepdims=True)
    e = jnp.exp(v2 - v1)
    w1 = 1.0 / (1.0 + e)
    w2 = e / (1.0 + e)
    ids_ref[...] = jnp.where(lane == 0, i1, jnp.where(lane == 1, i2, 0.0)).astype(jnp.int32)
    wts_ref[...] = jnp.where(lane == 0, w1, jnp.where(lane == 1, w2, 0.0))


def _attn_out(o_lat, x, wuv, wo, g, wr, *, n_experts):
    n, d = x.shape
    tm = TOKEN_TILE
    row = lambda w: pl.BlockSpec((tm, w), lambda i: (i, 0))
    return pl.pallas_call(
        functools.partial(_attn_out_kernel, n_experts=n_experts),
        grid=(n // tm,),
        in_specs=[row(o_lat.shape[1]), row(d), _full(wuv.shape), _full(wo.shape), _full(g.shape),
                  _full(wr.shape)],
        out_specs=[row(d), row(d), row(LANES), row(LANES)],
        out_shape=[jax.ShapeDtypeStruct((n, d), F32), jax.ShapeDtypeStruct((n, d), BF16),
                   jax.ShapeDtypeStruct((n, LANES), jnp.int32),
                   jax.ShapeDtypeStruct((n, LANES), F32)],
        compiler_params=_cparams("parallel"),
        name="attn_out_router",
    )(o_lat, x, wuv, wo, g, wr)


def _moe_kernel(pt_ref, pe_ref, lo_ref, hi_ref, first_ref, win_ref, xs_ref, wg_ref, wu_ref, wd_ref,
                *refs, chunk):
    out_ref = refs[-1]
    step = pl.program_id(0)
    i = win_ref[0] + step

    @pl.when(step < win_ref[1])
    def _():
        xs = xs_ref[...]
        acc = jnp.zeros(out_ref.shape, F32)
        for c in range(wg_ref.shape[2] // chunk):
            sl = slice(c * chunk, (c + 1) * chunk)
            gate = _dot(xs, wg_ref[0, :, sl])
            up = _dot(xs, wu_ref[0, :, sl])
            acc += _dot((gate * _sigmoid(gate) * up).astype(BF16), wd_ref[0, sl, :])
        res = acc.astype(out_ref.dtype)
        row = lax.broadcasted_iota(jnp.int32, out_ref.shape, 0)
        lo, hi = lo_ref[i], hi_ref[i]

        def keep_rows(base):
            return jnp.where(row >= lo, jnp.where(row < hi, res, base), base)

        @pl.when(first_ref[i] == 1)
        def _():
            out_ref[...] = keep_rows(jnp.zeros_like(res))

        @pl.when(first_ref[i] == 0)
        def _():
            out_ref[...] = keep_rows(out_ref[...])


def _moe(pairs, window, xs, wg, wu, wd, out_prev, t0, n_rows):
    d = xs.shape[1]
    tm = MOE_TILE
    n_experts = wg.shape[0]
    pair = lambda s, win: win[0] + jnp.minimum(s, win[1] - 1)
    expert = lambda s, pt, pe, lo, hi, first, win: (pe[pair(s, win)], 0, 0)
    w_specs = [pl.BlockSpec((1,) + w.shape[1:], expert) for w in (wg, wu, wd)]
    args = [xs, wg, wu, wd]
    in_specs = [pl.BlockSpec((tm, d), lambda s, pt, pe, lo, hi, first, win:
                             (pt[pair(s, win)] - t0, 0))] + w_specs
    aliases = {}
    if out_prev is not None:
        in_specs.append(pl.BlockSpec(memory_space=pl.ANY))
        args.append(out_prev)
        aliases = {len(pairs) + 1 + len(args) - 1: 0}
    grid_spec = pltpu.PrefetchScalarGridSpec(
        num_scalar_prefetch=len(pairs) + 1, grid=(xs.shape[0] // tm + n_experts - 1,),
        in_specs=in_specs,
        out_specs=pl.BlockSpec((tm, d), lambda s, pt, pe, lo, hi, first, win:
                               (pt[pair(s, win)], 0)))
    return pl.pallas_call(
        functools.partial(_moe_kernel, chunk=MOE_CHUNK),
        grid_spec=grid_spec,
        out_shape=jax.ShapeDtypeStruct((n_rows, d), BF16),
        input_output_aliases=aliases,
        compiler_params=_cparams("arbitrary"),
        name="moe_experts",
    )(*pairs, window, *args)


def _final_kernel(x_ref, y1_ref, y2_ref, wts_ref, g_ref, outp_ref, outs_ref, *, npb):
    w = wts_ref[...]
    x4 = x_ref[...] + (w[:, 0:1] * y1_ref[...].astype(F32) + w[:, 1:2] * y2_ref[...].astype(F32))
    _store_split(npb, outp_ref, outs_ref, _rms(x4, g_ref[...]))


def _final(x, y1, y2, wts, g, n_p):
    n, d = x.shape
    tm = TOKEN_TILE
    npb = n_p // tm
    row = lambda w: pl.BlockSpec((tm, w), lambda i: (i, 0))
    return pl.pallas_call(
        functools.partial(_final_kernel, npb=npb), grid=(n // tm,),
        in_specs=[row(d), row(d), row(d), row(LANES), _full(g.shape)],
        out_specs=_split_specs(tm, d, npb),
        out_shape=[jax.ShapeDtypeStruct((n_p, d), F32), jax.ShapeDtypeStruct((n - n_p, d), F32)],
        compiler_params=_cparams("arbitrary"),
        name="combine_final",
    )(x, y1, y2, wts, g)


def _rot_cols(w, half):
    return jnp.concatenate([-w[..., half:], w[..., :half]], axis=-1)


def _route(e1, e2, n_experts):
    n = e1.shape[0]
    ex = jnp.arange(n_experts, dtype=jnp.int32)[:, None]
    h1, h2 = e1[None, :] == ex, e2[None, :] == ex
    m = h1.astype(jnp.int32) + h2.astype(jnp.int32)
    counts = jnp.sum(m, axis=1)
    ends = jnp.cumsum(counts)
    starts = ends - counts
    row = starts[:, None] + jnp.cumsum(m, axis=1) - m
    pos1 = jnp.sum(jnp.where(h1, row, 0), axis=0)
    pos2 = jnp.sum(jnp.where(h2, row, 0), axis=0)
    shift = (2 * n - 1).bit_length()
    a1 = 2 * jnp.arange(n, dtype=jnp.int32)
    keys = jnp.concatenate([(e1 << shift) | a1, (e2 << shift) | (a1 + 1)])
    order = jnp.sort(keys) & ((1 << shift) - 1)
    return order, pos1, pos2, starts, ends


def _tile_expert_pairs(starts, ends, n_tiles, tile):
    n_experts = starts.shape[0]
    max_pairs = n_tiles + n_experts - 1
    base = jnp.arange(n_tiles, dtype=jnp.int32)[:, None] * tile
    lo = jnp.maximum(starts[None, :] - base, 0).reshape(-1)
    hi = jnp.minimum(ends[None, :] - base, tile).reshape(-1)
    hit = hi > lo
    n = jnp.sum(hit).astype(jnp.int32)
    idx = jnp.nonzero(hit, size=max_pairs, fill_value=0)[0].astype(jnp.int32)
    idx = jnp.where(jnp.arange(max_pairs) < n, idx, idx[n - 1])
    p_tile = idx // n_experts
    first = jnp.concatenate([jnp.ones((1,), jnp.int32),
                             (p_tile[1:] != p_tile[:-1]).astype(jnp.int32)])
    return (p_tile, idx % n_experts, lo[idx], hi[idx], first), n


def kernel(x_prompt, x_sample, state_C, state_n, state_m, cache_ckv, cache_krope, page_table, g_norm_a, w_in_a, b_gate_a, g_head_a, w_out_a, g_kv, w_dkv, g_ckv, w_uk, w_uv, g_norm_b, w_dq, g_q, w_uq, w_o_b, g_ffn_d, w_gate_d, w_up_d, w_down_d, g_ffn_m, w_router, w_gate_m, w_up_m, w_down_m, g_final):
    B, T, D = x_prompt.shape
    DB, TS, _ = x_sample.shape
    H, DV = g_head_a.shape[1:]
    DK = state_C.shape[3]
    kv_lora, n_bheads, nope_dim = w_uk.shape
    rope_dim = cache_krope.shape[2]
    page = cache_ckv.shape[1]
    past_len = page_table.shape[1] * page
    n_experts = w_router.shape[2]
    assert state_C.shape[0] == 1 and g_norm_b.shape[0] == 1 and g_ffn_d.shape[0] == 1
    assert TS <= SAMPLE_PAD and 2 * rope_dim == LANES and 2 * H <= LANES
    TP = SAMPLE_PAD
    n_p, n_s = B * T, DB * TP
    n = n_p + n_s
    assert n_p % TOKEN_TILE == 0 and n_s % TOKEN_TILE == 0 and T % ATTN_TILE == 0

    x_p = x_prompt.reshape(n_p, D)
    x_s = jnp.pad(x_sample, ((0, 0), (0, TP - TS), (0, 0))).reshape(n_s, D)

    w_in = w_in_a[0].astype(BF16)
    hq, hv = H * DK, H * DV
    wq, wk, wv, wo = (w_in[:, :hq], w_in[:, hq:2 * hq], w_in[:, 2 * hq:2 * hq + hv],
                      w_in[:, 2 * hq + hv:2 * hq + 2 * hv])
    wgate = jnp.pad(w_in[:, 2 * hq + 2 * hv:], ((0, 0), (0, LANES - 2 * H)))
    bgate = jnp.pad(b_gate_a[0], (0, LANES - 2 * H)).reshape(1, LANES)
    q, k, v, o, gates = _in_proj(x_p, x_s, g_norm_a, wq, wk, wv, wo, wgate, bgate,
                                 n_heads=H, q_scale=DK ** -0.5)
    hg, c_p, n_pr, m_p = _mlstm(q, k, v, o, gates, g_head_a[0], row0=0, B=B, T=T,
                                L=MLSTM_CHUNK, t_valid=MLSTM_CHUNK)
    hg, c_s, n_sm, m_s = _mlstm(q, k, v, o, gates, g_head_a[0], row0=n_p, B=DB, T=TP, L=TP,
                                t_valid=TS, state=(state_C[0], state_n[0], state_m[0]), hg_prev=hg)

    f = w_gate_d.shape[2]
    f_pad = -(-f // FFN_CHUNK) * FFN_CHUNK
    nch = f_pad // FFN_CHUNK
    col_chunks = lambda w: jnp.pad(w.astype(BF16), ((0, 0), (0, f_pad - f))).reshape(
        D, nch, FFN_CHUNK).transpose(1, 0, 2)
    wd_d = jnp.pad(w_down_d[0].astype(BF16), ((0, f_pad - f), (0, 0))).reshape(nch, FFN_CHUNK, D)
    x2 = _ffn(x_p, x_s, hg, w_out_a[0].astype(BF16), g_ffn_d, col_chunks(w_gate_d[0]),
              col_chunks(w_up_d[0]), wd_d)

    half = rope_dim // 2
    inv = ROPE_THETA ** (-jnp.arange(half, dtype=F32) / half)

    def tables(pos):
        ang = pos.astype(F32)[:, None] * inv[None, :]
        return (jnp.tile(jnp.cos(ang), (1, LANES // half)), jnp.tile(jnp.sin(ang), (1, LANES // half)))

    cos_p, sin_p = tables(jnp.arange(T, dtype=jnp.int32))
    cos_s, sin_s = tables(past_len + jnp.arange(TP, dtype=jnp.int32))
    reps = ATTN_TILE // TP
    cos_tab = jnp.concatenate([cos_p, jnp.tile(cos_s, (reps, 1))], axis=0)
    sin_tab = jnp.concatenate([sin_p, jnp.tile(sin_s, (reps, 1))], axis=0)

    w_kr = w_dkv[:, kv_lora:]
    wdkv = jnp.concatenate([w_dkv[:, :kv_lora], w_kr, _rot_cols(w_kr, half)], axis=1).astype(BF16)
    wuq = w_uq[0].reshape(-1, n_bheads, nope_dim + rope_dim)
    w_nope = wuq[:, :, :nope_dim].reshape(-1, n_bheads * nope_dim).astype(BF16)
    w_rope = wuq[:, :, nope_dim:]
    w_r = w_rope.reshape(-1, n_bheads * rope_dim).astype(BF16)
    w_rr = _rot_cols(w_rope, half).reshape(-1, n_bheads * rope_dim).astype(BF16)
    w_ukt = jnp.transpose(w_uk, (1, 2, 0)).astype(BF16)
    lat_weights = [g_kv.reshape(1, D), wdkv, g_ckv.reshape(1, kv_lora), g_norm_b,
                   w_dq[0].astype(BF16), g_q, w_nope, w_r, w_rr, w_ukt]
    ckv_p, ckv_s, kr_p, kr_s, kcat, kvt, qcat = _latq(
        x2, cos_tab, sin_tab, n_p // ATTN_TILE, T // ATTN_TILE, lat_weights, n_heads=n_bheads,
        kv_lora=kv_lora, rope_dim=rope_dim, nope_dim=nope_dim)

    scale = (nope_dim + rope_dim) ** -0.5
    f_m = w_gate_m.shape[3]
    o_lat, (wg_m, wu_m, wd_m) = _attn_prompt(
        qcat, kcat, kvt, n,
        [w_gate_m[0].reshape(n_experts * D, f_m), w_up_m[0].reshape(n_experts * D, f_m),
         w_down_m[0].reshape(n_experts * f_m, D)],
        B=B, T=T, kv_lora=kv_lora, scale=scale)

    kw = kv_lora + 2 * rope_dim
    q_s = qcat[n_p // ATTN_TILE:].reshape(-1, n_bheads, ATTN_TILE // TP, TP, kw)[:, :, :, :TS]
    q_s = q_s.transpose(0, 2, 1, 3, 4).reshape(DB, n_bheads * TS, kw)
    q_s = jnp.concatenate([q_s[..., :kv_lora],
                           q_s[..., kv_lora:kv_lora + rope_dim] + q_s[..., kv_lora + rope_dim:]], axis=-1)
    k_new = jnp.pad(kcat[n_p:].reshape(DB, TP, kw), ((0, 0), (0, LANES - TP), (0, 0)))
    o_s = _attn_sample(page_table, q_s, k_new, cache_ckv, jnp.swapaxes(cache_krope, 1, 2),
                       t_valid=TS, scale=scale)
    o_s = o_s.reshape(DB, n_bheads, TS, kv_lora).transpose(0, 2, 1, 3)
    o_s = jnp.pad(o_s, ((0, 0), (0, TP - TS), (0, 0), (0, 0))).reshape(n_s, n_bheads * kv_lora)
    o_lat = lax.dynamic_update_slice(o_lat, o_s.astype(BF16), (n_p, 0))

    w_uvh = jnp.transpose(w_uv, (1, 0, 2)).astype(BF16)
    w_rt = jnp.pad(w_router[0], ((0, 0), (0, LANES - n_experts))).astype(BF16)
    x3, xn_m, ids, wts = _attn_out(o_lat, x2, w_uvh, w_o_b[0].astype(BF16), g_ffn_m, w_rt,
                                   n_experts=n_experts)

    top_k = 2
    n2 = n * top_k
    assert n2 % MOE_TILE == 0
    order, pos1, pos2, starts, ends = _route(ids[:, 0], ids[:, 1], n_experts)
    row_token = order // top_k
    n_tiles = n2 // MOE_TILE
    parts = math.gcd(n_tiles, MOE_PARTS)
    part_tiles = n_tiles // parts
    weights = (wg_m.reshape(n_experts, D, f_m), wu_m.reshape(n_experts, D, f_m),
               wd_m.reshape(n_experts, f_m, D))
    pairs, n_pairs = _tile_expert_pairs(starts, ends, n_tiles, MOE_TILE)
    valid = jnp.arange(pairs[0].shape[0]) < n_pairs
    ys = None
    for p in range(parts):
        t0 = p * part_tiles
        before = jnp.sum(valid & (pairs[0] < t0))
        upto = jnp.sum(valid & (pairs[0] < t0 + part_tiles))
        window = jnp.stack([before, upto - before]).astype(jnp.int32)
        rows = row_token[t0 * MOE_TILE:(t0 + part_tiles) * MOE_TILE]
        ys = _moe(pairs, window, xn_m[rows], *weights, ys, t0, n2)
    y_p, y_s = _final(x3, ys[pos1], ys[pos2], wts, g_final.reshape(1, D), n_p)

    prompt = lambda a: a.reshape(B, T, a.shape[1])
    sample = lambda a: a.reshape(DB, TP, a.shape[1])[:, :TS]
    return (prompt(y_p), sample(y_s), c_p[None], n_pr[None], m_p[None], prompt(ckv_p), prompt(kr_p),
            c_s[None], n_sm[None], m_s[None], sample(ckv_s), sample(kr_s))
```

```python
import functools
import math

import jax
import jax.numpy as jnp
from jax import lax
from jax.experimental import pallas as pl
from jax.experimental.pallas import tpu as pltpu

F32 = jnp.float32
BF16 = jnp.bfloat16

EPS = 1e-6
GATE_CAP = 15.0
ROPE_THETA = 10000.0
NEG = -1e30
LANES = 128
VMEM_LIMIT_BYTES = 56 * 2**20

SAMPLE_PAD = 8
TOKEN_TILE = 512
ATTN_TILE = 512
ATTN_KEY_TILE = 256
MLSTM_CHUNK = 512
MLSTM_ROW_BLOCK = 256
SHORT_SEQS_PER_STEP = 1
FFN_CHUNK = 1408
MOE_TILE = 512
MOE_CHUNK = 512
MOE_PARTS = 4
PAGES_PER_STEP = 64


def _cparams(*sem, flags=None):
    return pltpu.CompilerParams(dimension_semantics=sem, vmem_limit_bytes=VMEM_LIMIT_BYTES,
                                flags=flags)


def _dot(a, b):
    return jnp.dot(a, b, preferred_element_type=F32)


def _dot_nt(a, b):
    return lax.dot_general(a, b, (((1,), (1,)), ((), ())), preferred_element_type=F32)


def _rms(x, g):
    return x * lax.rsqrt(jnp.mean(x * x, axis=-1, keepdims=True) + EPS) * g


def _sigmoid(x):
    return 1.0 / (1.0 + jnp.exp(-x))


def _split3(x):
    hi = x.astype(BF16)
    r1 = x - hi.astype(F32)
    mid = r1.astype(BF16)
    lo = (r1 - mid.astype(F32)).astype(BF16)
    return hi, mid, lo


def _split_specs(tm, w, npb):
    return [pl.BlockSpec((tm, w), lambda i: (jnp.minimum(i, npb - 1), 0)),
            pl.BlockSpec((tm, w), lambda i: (jnp.maximum(i - npb, 0), 0))]


def _load_split(npb, p_ref, s_ref):
    return jnp.where(pl.program_id(0) < npb, p_ref[...], s_ref[...])


def _store_split(npb, p_ref, s_ref, val):
    i = pl.program_id(0)

    @pl.when(i < npb)
    def _():
        p_ref[...] = val

    @pl.when(i >= npb)
    def _():
        s_ref[...] = val


def _full(shape):
    nd = len(shape)
    return pl.BlockSpec(shape, lambda *_: (0,) * nd)


def _in_proj_kernel(xp_ref, xs_ref, g_ref, wq_ref, wk_ref, wv_ref, wo_ref, wg_ref, b_ref,
                    q_ref, k_ref, v_ref, o_ref, gate_ref, *, n_heads, q_scale, npb):
    xn = _rms(_load_split(npb, xp_ref, xs_ref), g_ref[...]).astype(BF16)
    q_ref[...] = _dot(xn, wq_ref[...]) * q_scale
    k_ref[...] = _dot(xn, wk_ref[...])
    v_ref[...] = _dot(xn, wv_ref[...])
    o_ref[...] = _dot(xn, wo_ref[...])
    g = _dot(xn, wg_ref[...]) + b_ref[...]
    g = GATE_CAP * jnp.tanh(g / GATE_CAP)
    logf = jnp.minimum(g, 0.0) - jnp.log1p(jnp.exp(-jnp.abs(g)))
    lane = lax.broadcasted_iota(jnp.int32, g.shape, 1)
    gate_ref[...] = jnp.where(lane < n_heads, g, jnp.where(lane < 2 * n_heads, logf, 0.0))


def _in_proj(xp, xs, g, wq, wk, wv, wo, wg, b, *, n_heads, q_scale):
    d = xp.shape[1]
    n = xp.shape[0] + xs.shape[0]
    tm = TOKEN_TILE
    npb = xp.shape[0] // tm
    row = lambda w: pl.BlockSpec((tm, w), lambda i: (i, 0))
    outs = [(wq.shape[1], F32), (wk.shape[1], F32), (wv.shape[1], F32), (wo.shape[1], F32), (LANES, F32)]
    return pl.pallas_call(
        functools.partial(_in_proj_kernel, n_heads=n_heads, q_scale=q_scale, npb=npb),
        grid=(n // tm,),
        in_specs=_split_specs(tm, d, npb) + [
            _full(g.shape), _full(wq.shape), _full(wk.shape), _full(wv.shape),
            _full(wo.shape), _full(wg.shape), _full(b.shape)],
        out_specs=[row(w) for w, _ in outs],
        out_shape=[jax.ShapeDtypeStruct((n, w), dt) for w, dt in outs],
        compiler_params=_cparams("parallel"),
        name="in_proj",
    )(xp, xs, g, wq, wk, wv, wo, wg, b)


def _mlstm_kernel(*refs, L, H, DK, DV, nb, t_valid, has_state, mm_dtype):
    if has_state:
        (q_ref, k_ref, v_ref, o_ref, gate_ref, gh_ref, c0_ref, n0_ref, m0_ref, _,
         hg_ref, cout_ref, nout_ref, mout_ref, caug_ref, m_scr) = refs
    else:
        (q_ref, k_ref, v_ref, o_ref, gate_ref, gh_ref,
         hg_ref, cout_ref, nout_ref, mout_ref, caug_ref, m_scr) = refs
    c = pl.program_id(1)
    last = pl.num_programs(1) - 1

    rk = lax.broadcasted_iota(jnp.int32, (DK, DK), 0)
    ck = lax.broadcasted_iota(jnp.int32, (DK, DK), 1)
    eye_k = rk == ck

    @pl.when(c == 0)
    def _():
        if has_state:
            for sh in range(nb * H):
                s, h = divmod(sh, H)
                caug_ref[sh, :, :DV] = c0_ref[s, h]
                n_row = n0_ref[s, h:h + 1, :]
                n_col = jnp.sum(jnp.where(eye_k, jnp.broadcast_to(n_row, (DK, DK)), 0.0),
                                axis=1, keepdims=True)
                caug_ref[sh, :, DV:] = jnp.broadcast_to(n_col, (DK, DV))
                m_scr[sh:sh + 1, :] = jnp.broadcast_to(m0_ref[s, :, h:h + 1], (1, LANES))
        else:
            caug_ref[...] = jnp.zeros_like(caug_ref)
            m_scr[...] = jnp.zeros_like(m_scr)

    ri = lax.broadcasted_iota(jnp.int32, (L, L), 0)
    ci = lax.broadcasted_iota(jnp.int32, (L, L), 1)
    causal = ci <= ri
    eye = ci == ri
    ones_v = jnp.ones((L, DV), mm_dtype)
    use_mxu_cumsum = L % LANES == 0
    RB = MLSTM_ROW_BLOCK if L % MLSTM_ROW_BLOCK == 0 else L

    def head(s, h, gates, cums):
        rows = slice(s * L, (s + 1) * L)
        sh = s * H + h
        if use_mxu_cumsum:
            gates_t, cum, cum_t = cums
            a_col = cum[:, H + h:H + h + 1]
            b_row = gates_t[h:h + 1, :] - cum_t[H + h:H + h + 1, :]
        else:
            li_col = gates[:, h:h + 1]
            lf_col = gates[:, H + h:H + h + 1]
            lf_b = jnp.broadcast_to(lf_col, (L, L))
            lf_row = jnp.sum(jnp.where(eye, lf_b, 0.0), axis=0, keepdims=True)
            a_col = jnp.sum(jnp.where(causal, jnp.broadcast_to(lf_row, (L, L)), 0.0),
                            axis=1, keepdims=True)
            a_row = jnp.sum(jnp.where(ci >= ri, lf_b, 0.0), axis=0, keepdims=True)
            li_row = jnp.sum(jnp.where(eye, jnp.broadcast_to(li_col, (L, L)), 0.0),
                             axis=0, keepdims=True)
            b_row = li_row - a_row
        m_prev = m_scr[sh:sh + 1, 0:1]
        kh = k_ref[rows, h * DK:(h + 1) * DK].astype(mm_dtype)
        vaug = jnp.concatenate([v_ref[rows, h * DV:(h + 1) * DV].astype(mm_dtype), ones_v], axis=1)
        caug = caug_ref[sh]
        caug_mm = caug.astype(mm_dtype)

        for r in range(L // RB):
            rb = slice(r * RB, (r + 1) * RB)
            rows_r = slice(s * L + r * RB, s * L + (r + 1) * RB)
            a_r = a_col[rb]
            d = jnp.where(causal[rb], a_r + b_row, NEG)
            inter = a_r + m_prev
            m_t = jnp.maximum(inter, jnp.max(d, axis=1, keepdims=True))
            qh = q_ref[rows_r, h * DK:(h + 1) * DK].astype(mm_dtype)
            w = jnp.exp(d - m_t) * _dot_nt(qh, kh)
            e_inter = jnp.exp(inter - m_t)
            num = _dot(w.astype(mm_dtype), vaug) + e_inter * _dot(qh, caug_mm)
            hh = num[:, :DV] / jnp.maximum(jnp.abs(num[:, DV:]), jnp.exp(-m_t))
            hn = _rms(hh, gh_ref[h:h + 1, :])
            hg_ref[rows_r, h * DV:(h + 1) * DV] = (
                _sigmoid(o_ref[rows_r, h * DV:(h + 1) * DV]) * hn)

        m_new = m_t[RB - 1:RB, :]
        a_last = a_col[L - 1:L, :]
        e_end = jnp.exp(a_last + b_row - m_new)
        e_carry = jnp.exp(a_last + m_prev - m_new)
        k_t = _dot_nt(eye_k.astype(mm_dtype), kh)
        caug_new = e_carry * caug + _dot((k_t * e_end).astype(mm_dtype), vaug)
        caug_ref[sh] = caug_new
        m_scr[sh:sh + 1, :] = jnp.broadcast_to(m_new, (1, LANES))

        @pl.when(c == last)
        def _():
            cout_ref[s, h] = caug_new[:, :DV]
            nout_ref[s, h:h + 1, :] = jnp.sum(jnp.where(eye_k, caug_new[:, DV:DV + DK], 0.0),
                                              axis=0, keepdims=True)
            mout_ref[s, :, h:h + 1] = m_new

    for s in range(nb):
        gates = gate_ref[s * L:(s + 1) * L, :]
        if t_valid < L:
            t_id = lax.broadcasted_iota(jnp.int32, gates.shape, 0)
            lane = lax.broadcasted_iota(jnp.int32, gates.shape, 1)
            gates = jnp.where(t_id < t_valid, gates, jnp.where(lane < H, NEG, 0.0))
        cums = None
        if use_mxu_cumsum:
            gates_t = gates.T
            cum = sum(_dot(causal.astype(BF16), p) for p in _split3(gates))
            cum_t = sum(_dot(p, (ri <= ci).astype(BF16)) for p in _split3(gates_t))
            cums = (gates_t, cum, cum_t)
        for h in range(H):
            head(s, h, gates, cums)


def _mlstm(q, k, v, o, gates, g_head, *, row0, B, T, L, t_valid, state=None, hg_prev=None):
    n = q.shape[0]
    H, DV = g_head.shape
    DK = q.shape[1] // H
    nc = T // L
    nb = SHORT_SEQS_PER_STEP if nc == 1 else 1
    assert B % nb == 0 and row0 % (nb * L) == 0
    blk0 = row0 // (nb * L)
    row = lambda w: pl.BlockSpec((nb * L, w), lambda b, c: (blk0 + b * nc + c, 0))
    c_spec = pl.BlockSpec((nb, H, DK, DV), lambda b, c: (b, 0, 0, 0))
    n_spec = pl.BlockSpec((nb, H, DK), lambda b, c: (b, 0, 0))
    m_spec = pl.BlockSpec((nb, 1, H), lambda b, c: (b, 0, 0))
    in_specs = [row(H * DK), row(H * DK), row(H * DV), row(H * DV), row(LANES), _full(g_head.shape)]
    args = [q, k, v, o, gates, g_head]
    aliases = {}
    if state is not None:
        c0, n0, m0 = state
        in_specs += [c_spec, n_spec, m_spec, pl.BlockSpec(memory_space=pl.ANY)]
        args += [c0, n0, m0.reshape(B, 1, H), hg_prev]
        aliases = {len(args) - 1: 0}
    out_shape = [jax.ShapeDtypeStruct((n, H * DV), F32),
                 jax.ShapeDtypeStruct((B, H, DK, DV), F32),
                 jax.ShapeDtypeStruct((B, H, DK), F32),
                 jax.ShapeDtypeStruct((B, 1, H), F32)]
    out_specs = [row(H * DV), c_spec, n_spec, m_spec]
    kern = functools.partial(_mlstm_kernel, L=L, H=H, DK=DK, DV=DV, nb=nb, t_valid=t_valid,
                             has_state=state is not None,
                             mm_dtype=BF16 if L % 16 == 0 else F32)
    hg, c_out, n_out, m_out = pl.pallas_call(
        kern, grid=(B // nb, nc), in_specs=in_specs, out_specs=out_specs, out_shape=out_shape,
        scratch_shapes=[pltpu.VMEM((nb * H, DK, 2 * DV), F32), pltpu.VMEM((nb * H, LANES), F32)],
        input_output_aliases=aliases,
        compiler_params=_cparams("parallel", "arbitrary"),
        name="mlstm_sample" if state is not None else "mlstm_prompt",
    )(*args)
    return hg, c_out, n_out, m_out.reshape(B, H)


def _ffn_kernel(xp_ref, xs_ref, hg_ref, wout_ref, g_ref, wg_ref, wu_ref, wd_ref, out_ref, acc_ref,
                *, npb):
    x1 = _load_split(npb, xp_ref, xs_ref) + _dot(hg_ref[...].astype(BF16), wout_ref[...])
    xn = _rms(x1, g_ref[...]).astype(BF16)
    acc_ref[...] = jnp.zeros_like(acc_ref)

    def body(c, carry):
        gate = _dot(xn, wg_ref[c])
        up = _dot(xn, wu_ref[c])
        hmid = (gate * _sigmoid(gate) * up).astype(BF16)
        acc_ref[...] += _dot(hmid, wd_ref[c])
        return carry

    lax.fori_loop(0, wg_ref.shape[0], body, 0)
    out_ref[...] = x1 + acc_ref[...]


def _ffn(xp, xs, hg, wout, g, wg, wu, wd):
    n, d = hg.shape
    tm = TOKEN_TILE
    npb = xp.shape[0] // tm
    row = pl.BlockSpec((tm, d), lambda i: (i, 0))
    return pl.pallas_call(
        functools.partial(_ffn_kernel, npb=npb), grid=(n // tm,),
        in_specs=_split_specs(tm, d, npb) + [
            row, _full(wout.shape), _full(g.shape), _full(wg.shape), _full(wu.shape),
            _full(wd.shape)],
        out_specs=row, out_shape=jax.ShapeDtypeStruct((n, d), F32),
        scratch_shapes=[pltpu.VMEM((tm, d), F32)],
        compiler_params=_cparams("parallel"),
        name="outproj_ffn",
    )(xp, xs, hg, wout, g, wg, wu, wd)


def _latq_kernel(x_ref, cos_ref, sin_ref, gkv_ref, wdkv_ref, gckv_ref, gnb_ref, wdq_ref, gq_ref,
                 wn_ref, wr_ref, wrr_ref, wuk_ref,
                 ckvp_ref, ckvs_ref, krp_ref, krs_ref, kcat_ref, kvt_ref, q_ref,
                 *, n_heads, kv_lora, rope_dim, nope_dim, npb):
    x = x_ref[...]
    xs = x * lax.rsqrt(jnp.mean(x * x, axis=-1, keepdims=True) + EPS)
    cos = cos_ref[...]
    sin = sin_ref[...]
    lane = lax.broadcasted_iota(jnp.int32, cos.shape, 1)
    lo = lane < rope_dim

    lat = _dot((xs * gkv_ref[...]).astype(BF16), wdkv_ref[...])
    ckv = _rms(lat[:, :kv_lora], gckv_ref[...])
    _store_split(npb, ckvp_ref, ckvs_ref, ckv)
    t = lat[:, kv_lora:] * jnp.where(lo, cos, sin)
    kr2 = t + pltpu.roll(t, rope_dim, axis=1)
    _store_split(npb, krp_ref, krs_ref, kr2[:, :rope_dim])
    kcat_ref[...] = jnp.concatenate([ckv, kr2], axis=1).astype(BF16)
    for r in range(kvt_ref.shape[0]):
        kvt_ref[r] = ckv[r * ATTN_KEY_TILE:(r + 1) * ATTN_KEY_TILE].T.astype(BF16)

    cq = _dot((xs * gnb_ref[...]).astype(BF16), wdq_ref[...])
    cqn = _rms(cq, gq_ref[...]).astype(BF16)
    qn = _dot(cqn, wn_ref[...]).astype(BF16)
    reps = n_heads * rope_dim // LANES
    cos_h = jnp.concatenate([cos] * reps, axis=1)
    sin_h = jnp.concatenate([sin] * reps, axis=1)
    qr = _dot(cqn, wr_ref[...]) * cos_h + _dot(cqn, wrr_ref[...]) * sin_h
    for h in range(n_heads):
        ql = _dot(qn[:, h * nope_dim:(h + 1) * nope_dim], wuk_ref[h])
        pair = qr[:, (h // 2) * LANES:(h // 2 + 1) * LANES]
        slot = jnp.where(lo if h % 2 == 0 else jnp.logical_not(lo), pair, 0.0)
        q_ref[0, h] = jnp.concatenate([ql, slot], axis=1).astype(BF16)


def _latq(x, cos_tab, sin_tab, n_prompt_blocks, blocks_per_seq, weights, *, n_heads, kv_lora,
          rope_dim, nope_dim):
    n, d = x.shape
    tm = ATTN_TILE
    kw = kv_lora + 2 * rope_dim

    def tab_map(i):
        return (jnp.where(i < n_prompt_blocks, i % blocks_per_seq, blocks_per_seq), 0)

    tab = pl.BlockSpec((tm, LANES), tab_map)
    row = lambda w: pl.BlockSpec((tm, w), lambda i: (i, 0))
    npb = n_prompt_blocks
    n_p, n_s = npb * tm, n - npb * tm
    return pl.pallas_call(
        functools.partial(_latq_kernel, n_heads=n_heads, kv_lora=kv_lora, rope_dim=rope_dim,
                          nope_dim=nope_dim, npb=npb),
        grid=(n // tm,),
        in_specs=[row(d), tab, tab] + [_full(w.shape) for w in weights],
        out_specs=_split_specs(tm, kv_lora, npb) + _split_specs(tm, rope_dim, npb) + [
            row(kw),
            pl.BlockSpec((tm // ATTN_KEY_TILE, kv_lora, ATTN_KEY_TILE), lambda i: (i, 0, 0)),
            pl.BlockSpec((1, n_heads, tm, kw), lambda i: (i, 0, 0, 0))],
        out_shape=[jax.ShapeDtypeStruct((n_p, kv_lora), F32),
                   jax.ShapeDtypeStruct((n_s, kv_lora), F32),
                   jax.ShapeDtypeStruct((n_p, rope_dim), F32),
                   jax.ShapeDtypeStruct((n_s, rope_dim), F32),
                   jax.ShapeDtypeStruct((n, kw), BF16),
                   jax.ShapeDtypeStruct((n // ATTN_KEY_TILE, kv_lora, ATTN_KEY_TILE), BF16),
                   jax.ShapeDtypeStruct((n // tm, n_heads, tm, kw), BF16)],
        compiler_params=_cparams("arbitrary"),
        name="latent_q",
    )(x, cos_tab, sin_tab, *weights)


def _attn_prompt_kernel(q_ref, k_ref, kt_ref, *refs, tq, tk, kv_lora, scale, n_cast):
    cast_in = refs[:n_cast]
    o_ref = refs[n_cast]
    cast_out = refs[n_cast + 1:2 * n_cast + 1]
    m_scr, l_scr, acc_scr, sa_scr, sb_scr = refs[2 * n_cast + 1:]
    for src, dst in zip(cast_in, cast_out):
        dst[...] = src[...].astype(dst.dtype)

    qi = pl.program_id(1)
    n_heads = q_ref.shape[1]
    m_scr[...] = jnp.full_like(m_scr, NEG)
    l_scr[...] = jnp.zeros_like(l_scr)
    acc_scr[...] = jnp.zeros_like(acc_scr)

    def scores(j, s_ref):
        kc = k_ref[pl.ds(pl.multiple_of(j * tk, tk), tk), :]
        for h in range(n_heads):
            s_ref[h] = _dot_nt(kc, q_ref[0, h])

    def consume(j, s_all, diag_offset):
        kt = kt_ref[j]
        if diag_offset is not None:
            key = lax.broadcasted_iota(jnp.int32, (tk, tq), 0) + diag_offset
            qry = lax.broadcasted_iota(jnp.int32, (tk, tq), 1)
            keep = key <= qry
        for h in range(n_heads):
            st = s_all[h] * scale
            if diag_offset is not None:
                st = jnp.where(keep, st, NEG)
            m_prev = m_scr[h]
            m_new = jnp.maximum(m_prev, jnp.max(st, axis=0, keepdims=True))
            p = jnp.exp(st - m_new)
            alpha = jnp.exp(m_prev - m_new)
            l_scr[h] = alpha * l_scr[h] + jnp.sum(p, axis=0, keepdims=True)
            acc_scr[h] = alpha * acc_scr[h] + _dot(kt, p.astype(BF16))
            m_scr[h] = m_new

    assert tq == 2 * tk

    def body(i, carry):
        j = 2 * i
        scores(j + 1, sb_scr)
        consume(j, sa_scr, None)
        scores(j + 2, sa_scr)
        consume(j + 1, sb_scr, None)
        return carry

    scores(0, sa_scr)
    lax.fori_loop(0, qi, body, 0)
    n_full = 2 * qi
    scores(n_full + 1, sb_scr)
    consume(n_full, sa_scr, 0)
    consume(n_full + 1, sb_scr, tk)
    for h in range(n_heads):
        o_ref[:, h * kv_lora:(h + 1) * kv_lora] = (acc_scr[h] / l_scr[h]).T.astype(BF16)


def _attn_prompt(q, kcat, kvt, n_rows, to_cast, *, B, T, kv_lora, scale):
    tq, tk = ATTN_TILE, ATTN_KEY_TILE
    _, n_heads, _, kw = q.shape
    nq = T // tq
    steps = B * nq
    cast_specs = []
    for w in to_cast:
        rows = w.shape[0] // steps
        assert rows * steps == w.shape[0] and rows % 16 == 0
        cast_specs.append(pl.BlockSpec((rows, w.shape[1]), lambda b, i: (b * nq + i, 0)))
    out = pl.pallas_call(
        functools.partial(_attn_prompt_kernel, tq=tq, tk=tk, kv_lora=kv_lora, scale=scale,
                          n_cast=len(to_cast)),
        grid=(B, nq),
        in_specs=[pl.BlockSpec((1, n_heads, tq, kw), lambda b, i: (b * nq + i, 0, 0, 0)),
                  pl.BlockSpec((T, kw), lambda b, i: (b, 0)),
                  pl.BlockSpec((T // tk, kv_lora, tk), lambda b, i: (b, 0, 0))] + cast_specs,
        out_specs=[pl.BlockSpec((tq, n_heads * kv_lora), lambda b, i: (b * nq + i, 0))] + cast_specs,
        out_shape=[jax.ShapeDtypeStruct((n_rows, n_heads * kv_lora), BF16)]
        + [jax.ShapeDtypeStruct(w.shape, BF16) for w in to_cast],
        scratch_shapes=[pltpu.VMEM((n_heads, 1, tq), F32), pltpu.VMEM((n_heads, 1, tq), F32),
                        pltpu.VMEM((n_heads, kv_lora, tq), F32),
                        pltpu.VMEM((n_heads, tk, tq), F32), pltpu.VMEM((n_heads, tk, tq), F32)],
        compiler_params=_cparams("parallel", "arbitrary"),
        name="attn_prompt",
    )(q, kcat, kvt, *to_cast)
    return out[0], out[1:]


def _attn_sample_kernel(pt_ref, q_ref, knew_ref, *refs, n_pages, t_valid, kv_lora, rope_dim, scale,
                        single_step):
    ck_refs = refs[:n_pages]
    kr_refs = refs[n_pages:2 * n_pages]
    o_ref, m_scr, l_scr, acc_scr = refs[2 * n_pages:]
    g = pl.program_id(1)
    q = q_ref[0]
    ql = q[:, :kv_lora]
    qr = q[:, kv_lora:]

    def new_token_scores():
        kn = knew_ref[0]
        ckn = kn[:, :kv_lora]
        sn = (_dot_nt(ql, ckn) + _dot_nt(qr, kn[:, kv_lora:kv_lora + rope_dim])) * scale
        t = lax.broadcasted_iota(jnp.int32, sn.shape, 0) % t_valid
        j = lax.broadcasted_iota(jnp.int32, sn.shape, 1)
        return jnp.where(j <= t, sn, NEG), ckn

    if single_step:
        cks = [r[0].astype(BF16) for r in ck_refs]
        sn, ckn = new_token_scores()
        s = jnp.concatenate(
            [(_dot_nt(ql, ck) + _dot(qr, kr[0].astype(BF16))) * scale
             for ck, kr in zip(cks, kr_refs)] + [sn], axis=1)
        p = jnp.exp(s - jnp.max(s, axis=1, keepdims=True))
        denom = jnp.sum(p, axis=1, keepdims=True)
        p = p.astype(BF16)
        values = cks + [ckn]
        page = cks[0].shape[0]
        pv = _dot(p[:, :page], values[0])
        for i in range(1, len(values)):
            pv += _dot(p[:, i * page:(i + 1) * page], values[i])
        o_ref[0] = pv / denom
        return

    @pl.when(g == 0)
    def _():
        m_scr[...] = jnp.full_like(m_scr, NEG)
        l_scr[...] = jnp.zeros_like(l_scr)
        acc_scr[...] = jnp.zeros_like(acc_scr)

    def update(s, values):
        m_prev = m_scr[...]
        m_new = jnp.maximum(m_prev, jnp.max(s, axis=1, keepdims=True))
        p = jnp.exp(s - m_new)
        alpha = jnp.exp(m_prev - m_new)
        l_scr[...] = alpha * l_scr[...] + jnp.sum(p, axis=1, keepdims=True)
        p = p.astype(BF16)
        pv = _dot(p[:, :values[0].shape[0]], values[0])
        for i in range(1, len(values)):
            rows = values[i].shape[0]
            pv += _dot(p[:, i * rows:(i + 1) * rows], values[i])
        acc_scr[...] = alpha * acc_scr[...] + pv
        m_scr[...] = m_new

    cks = [r[0].astype(BF16) for r in ck_refs]
    s = jnp.concatenate(
        [_dot_nt(ql, ck) + _dot(qr, kr[0].astype(BF16)) for ck, kr in zip(cks, kr_refs)],
        axis=1) * scale
    update(s, cks)

    @pl.when(g == pl.num_programs(1) - 1)
    def _():
        sn, ckn = new_token_scores()
        update(sn, [ckn])
        o_ref[0] = acc_scr[...] / l_scr[...]


def _attn_sample(page_table, q, knew, cache_ckv, cache_krope_t, *, t_valid, scale):
    nb, n_pages = page_table.shape
    _, rows, qw = q.shape
    _, page, kv_lora = cache_ckv.shape
    rope_dim = cache_krope_t.shape[1]
    pg = min(PAGES_PER_STEP, n_pages)
    assert n_pages % pg == 0
    ck_specs = [pl.BlockSpec((1, page, kv_lora),
                             functools.partial(lambda b, g, pt, i: (pt[b, g * pg + i], 0, 0), i=i))
                for i in range(pg)]
    kr_specs = [pl.BlockSpec((1, rope_dim, page),
                             functools.partial(lambda b, g, pt, i: (pt[b, g * pg + i], 0, 0), i=i))
                for i in range(pg)]
    grid_spec = pltpu.PrefetchScalarGridSpec(
        num_scalar_prefetch=1, grid=(nb, n_pages // pg),
        in_specs=[pl.BlockSpec((1, rows, qw), lambda b, g, pt: (b, 0, 0)),
                  pl.BlockSpec((1,) + knew.shape[1:], lambda b, g, pt: (b, 0, 0))]
        + ck_specs + kr_specs,
        out_specs=pl.BlockSpec((1, rows, kv_lora), lambda b, g, pt: (b, 0, 0)),
        scratch_shapes=[pltpu.VMEM((rows, 1), F32), pltpu.VMEM((rows, 1), F32),
                        pltpu.VMEM((rows, kv_lora), F32)])
    return pl.pallas_call(
        functools.partial(_attn_sample_kernel, n_pages=pg, t_valid=t_valid, kv_lora=kv_lora,
                          rope_dim=rope_dim, scale=scale, single_step=pg == n_pages),
        grid_spec=grid_spec,
        out_shape=jax.ShapeDtypeStruct((nb, rows, kv_lora), F32),
        compiler_params=_cparams("parallel", "arbitrary"),
        name="attn_sample",
    )(page_table, q, knew, *([cache_ckv] * pg), *([cache_krope_t] * pg))


def _attn_out_kernel(ol_ref, x_ref, wuv_ref, wo_ref, g_ref, wr_ref,
                     x3_ref, xn_ref, ids_ref, wts_ref, *, n_experts):
    n_heads, kv_lora, _ = wuv_ref.shape
    o = jnp.concatenate(
        [_dot(ol_ref[:, h * kv_lora:(h + 1) * kv_lora], wuv_ref[h]) for h in range(n_heads)],
        axis=1).astype(BF16)
    x3 = x_ref[...] + _dot(o, wo_ref[...])
    x3_ref[...] = x3
    xn = _rms(x3, g_ref[...]).astype(BF16)
    xn_ref[...] = xn
    logits = _dot(xn, wr_ref[...])
    lane = lax.broadcasted_iota(jnp.int32, logits.shape, 1)
    lane_f = lane.astype(F32)
    lg = jnp.where(lane < n_experts, logits, -jnp.inf)
    v1 = jnp.max(lg, axis=1, keepdims=True)
    i1 = jnp.min(jnp.where(lg == v1, lane_f, float(LANES)), axis=1, keepdims=True)
    lg2 = jnp.where(lane_f == i1, -jnp.inf, lg)
    v2 = jnp.max(lg2, axis=1, keepdims=True)
    i2 = jnp.min(jnp.where(lg2 == v2, lane_f, float(LANES)), axis=1, keepdims=True)
    e = jnp.exp(v2 - v1)
    w1 = 1.0 / (1.0 + e)
    w2 = e / (1.0 + e)
    ids_ref[...] = jnp.where(lane == 0, i1, jnp.where(lane == 1, i2, 0.0)).astype(jnp.int32)
    wts_ref[...] = jnp.where(lane == 0, w1, jnp.where(lane == 1, w2, 0.0))


def _attn_out(o_lat, x, wuv, wo, g, wr, *, n_experts):
    n, d = x.shape
    tm = TOKEN_TILE
    row = lambda w: pl.BlockSpec((tm, w), lambda i: (i, 0))
    return pl.pallas_call(
        functools.partial(_attn_out_kernel, n_experts=n_experts),
        grid=(n // tm,),
        in_specs=[row(o_lat.shape[1]), row(d), _full(wuv.shape), _full(wo.shape), _full(g.shape),
                  _full(wr.shape)],
        out_specs=[row(d), row(d), row(LANES), row(LANES)],
        out_shape=[jax.ShapeDtypeStruct((n, d), F32), jax.ShapeDtypeStruct((n, d), BF16),
                   jax.ShapeDtypeStruct((n, LANES), jnp.int32),
                   jax.ShapeDtypeStruct((n, LANES), F32)],
        compiler_params=_cparams("parallel"),
        name="attn_out_router",
    )(o_lat, x, wuv, wo, g, wr)


def _moe_kernel(pt_ref, pe_ref, lo_ref, hi_ref, first_ref, win_ref, xs_ref, wg_ref, wu_ref, wd_ref,
                *refs, chunk):
    out_ref = refs[-1]
    step = pl.program_id(0)
    i = win_ref[0] + step

    @pl.when(step < win_ref[1])
    def _():
        xs = xs_ref[...]
        acc = jnp.zeros(out_ref.shape, F32)
        for c in range(wg_ref.shape[2] // chunk):
            sl = slice(c * chunk, (c + 1) * chunk)
            gate = _dot(xs, wg_ref[0, :, sl])
            up = _dot(xs, wu_ref[0, :, sl])
            acc += _dot((gate * _sigmoid(gate) * up).astype(BF16), wd_ref[0, sl, :])
        res = acc.astype(out_ref.dtype)
        row = lax.broadcasted_iota(jnp.int32, out_ref.shape, 0)
        lo, hi = lo_ref[i], hi_ref[i]

        def keep_rows(base):
            return jnp.where(row >= lo, jnp.where(row < hi, res, base), base)

        @pl.when(first_ref[i] == 1)
        def _():
            out_ref[...] = keep_rows(jnp.zeros_like(res))

        @pl.when(first_ref[i] == 0)
        def _():
            out_ref[...] = keep_rows(out_ref[...])


def _moe(pairs, window, xs, wg, wu, wd, out_prev, t0, n_rows):
    d = xs.shape[1]
    tm = MOE_TILE
    n_experts = wg.shape[0]
    pair = lambda s, win: win[0] + jnp.minimum(s, win[1] - 1)
    expert = lambda s, pt, pe, lo, hi, first, win: (pe[pair(s, win)], 0, 0)
    w_specs = [pl.BlockSpec((1,) + w.shape[1:], expert) for w in (wg, wu, wd)]
    args = [xs, wg, wu, wd]
    in_specs = [pl.BlockSpec((tm, d), lambda s, pt, pe, lo, hi, first, win:
                             (pt[pair(s, win)] - t0, 0))] + w_specs
    aliases = {}
    if out_prev is not None:
        in_specs.append(pl.BlockSpec(memory_space=pl.ANY))
        args.append(out_prev)
        aliases = {len(pairs) + 1 + len(args) - 1: 0}
    grid_spec = pltpu.PrefetchScalarGridSpec(
        num_scalar_prefetch=len(pairs) + 1, grid=(xs.shape[0] // tm + n_experts - 1,),
        in_specs=in_specs,
        out_specs=pl.BlockSpec((tm, d), lambda s, pt, pe, lo, hi, first, win:
                               (pt[pair(s, win)], 0)))
    return pl.pallas_call(
        functools.partial(_moe_kernel, chunk=MOE_CHUNK),
        grid_spec=grid_spec,
        out_shape=jax.ShapeDtypeStruct((n_rows, d), BF16),
        input_output_aliases=aliases,
        compiler_params=_cparams("arbitrary"),
        name="moe_experts",
    )(*pairs, window, *args)


def _final_kernel(x_ref, y1_ref, y2_ref, wts_ref, g_ref, outp_ref, outs_ref, *, npb):
    w = wts_ref[...]
    x4 = x_ref[...] + (w[:, 0:1] * y1_ref[...].astype(F32) + w[:, 1:2] * y2_ref[...].astype(F32))
    _store_split(npb, outp_ref, outs_ref, _rms(x4, g_ref[...]))


def _final(x, y1, y2, wts, g, n_p):
    n, d = x.shape
    tm = TOKEN_TILE
    npb = n_p // tm
    row = lambda w: pl.BlockSpec((tm, w), lambda i: (i, 0))
    return pl.pallas_call(
        functools.partial(_final_kernel, npb=npb), grid=(n // tm,),
        in_specs=[row(d), row(d), row(d), row(LANES), _full(g.shape)],
        out_specs=_split_specs(tm, d, npb),
        out_shape=[jax.ShapeDtypeStruct((n_p, d), F32), jax.ShapeDtypeStruct((n - n_p, d), F32)],
        compiler_params=_cparams("arbitrary"),
        name="combine_final",
    )(x, y1, y2, wts, g)


def _rot_cols(w, half):
    return jnp.concatenate([-w[..., half:], w[..., :half]], axis=-1)


def _route(e1, e2, n_experts):
    n = e1.shape[0]
    ex = jnp.arange(n_experts, dtype=jnp.int32)[:, None]
    h1, h2 = e1[None, :] == ex, e2[None, :] == ex
    m = h1.astype(jnp.int32) + h2.astype(jnp.int32)
    counts = jnp.sum(m, axis=1)
    ends = jnp.cumsum(counts)
    starts = ends - counts
    row = starts[:, None] + jnp.cumsum(m, axis=1) - m
    pos1 = jnp.sum(jnp.where(h1, row, 0), axis=0)
    pos2 = jnp.sum(jnp.where(h2, row, 0), axis=0)
    shift = (2 * n - 1).bit_length()
    a1 = 2 * jnp.arange(n, dtype=jnp.int32)
    keys = jnp.concatenate([(e1 << shift) | a1, (e2 << shift) | (a1 + 1)])
    order = jnp.sort(keys) & ((1 << shift) - 1)
    return order, pos1, pos2, starts, ends


def _tile_expert_pairs(starts, ends, n_tiles, tile):
    n_experts = starts.shape[0]
    max_pairs = n_tiles + n_experts - 1
    base = jnp.arange(n_tiles, dtype=jnp.int32)[:, None] * tile
    lo = jnp.maximum(starts[None, :] - base, 0).reshape(-1)
    hi = jnp.minimum(ends[None, :] - base, tile).reshape(-1)
    hit = hi > lo
    n = jnp.sum(hit).astype(jnp.int32)
    idx = jnp.nonzero(hit, size=max_pairs, fill_value=0)[0].astype(jnp.int32)
    idx = jnp.where(jnp.arange(max_pairs) < n, idx, idx[n - 1])
    p_tile = idx // n_experts
    first = jnp.concatenate([jnp.ones((1,), jnp.int32),
                             (p_tile[1:] != p_tile[:-1]).astype(jnp.int32)])
    return (p_tile, idx % n_experts, lo[idx], hi[idx], first), n


def kernel(x_prompt, x_sample, state_C, state_n, state_m, cache_ckv, cache_krope, page_table, g_norm_a, w_in_a, b_gate_a, g_head_a, w_out_a, g_kv, w_dkv, g_ckv, w_uk, w_uv, g_norm_b, w_dq, g_q, w_uq, w_o_b, g_ffn_d, w_gate_d, w_up_d, w_down_d, g_ffn_m, w_router, w_gate_m, w_up_m, w_down_m, g_final):
    B, T, D = x_prompt.shape
    DB, TS, _ = x_sample.shape
    H, DV = g_head_a.shape[1:]
    DK = state_C.shape[3]
    kv_lora, n_bheads, nope_dim = w_uk.shape
    rope_dim = cache_krope.shape[2]
    page = cache_ckv.shape[1]
    past_len = page_table.shape[1] * page
    n_experts = w_router.shape[2]
    assert state_C.shape[0] == 1 and g_norm_b.shape[0] == 1 and g_ffn_d.shape[0] == 1
    assert TS <= SAMPLE_PAD and 2 * rope_dim == LANES and 2 * H <= LANES
    TP = SAMPLE_PAD
    n_p, n_s = B * T, DB * TP
    n = n_p + n_s
    assert n_p % TOKEN_TILE == 0 and n_s % TOKEN_TILE == 0 and T % ATTN_TILE == 0

    x_p = x_prompt.reshape(n_p, D)
    x_s = jnp.pad(x_sample, ((0, 0), (0, TP - TS), (0, 0))).reshape(n_s, D)

    w_in = w_in_a[0].astype(BF16)
    hq, hv = H * DK, H * DV
    wq, wk, wv, wo = (w_in[:, :hq], w_in[:, hq:2 * hq], w_in[:, 2 * hq:2 * hq + hv],
                      w_in[:, 2 * hq + hv:2 * hq + 2 * hv])
    wgate = jnp.pad(w_in[:, 2 * hq + 2 * hv:], ((0, 0), (0, LANES - 2 * H)))
    bgate = jnp.pad(b_gate_a[0], (0, LANES - 2 * H)).reshape(1, LANES)
    q, k, v, o, gates = _in_proj(x_p, x_s, g_norm_a, wq, wk, wv, wo, wgate, bgate,
                                 n_heads=H, q_scale=DK ** -0.5)
    hg, c_p, n_pr, m_p = _mlstm(q, k, v, o, gates, g_head_a[0], row0=0, B=B, T=T,
                                L=MLSTM_CHUNK, t_valid=MLSTM_CHUNK)
    hg, c_s, n_sm, m_s = _mlstm(q, k, v, o, gates, g_head_a[0], row0=n_p, B=DB, T=TP, L=TP,
                                t_valid=TS, state=(state_C[0], state_n[0], state_m[0]), hg_prev=hg)

    f = w_gate_d.shape[2]
    f_pad = -(-f // FFN_CHUNK) * FFN_CHUNK
    nch = f_pad // FFN_CHUNK
    col_chunks = lambda w: jnp.pad(w.astype(BF16), ((0, 0), (0, f_pad - f))).reshape(
        D, nch, FFN_CHUNK).transpose(1, 0, 2)
    wd_d = jnp.pad(w_down_d[0].astype(BF16), ((0, f_pad - f), (0, 0))).reshape(nch, FFN_CHUNK, D)
    x2 = _ffn(x_p, x_s, hg, w_out_a[0].astype(BF16), g_ffn_d, col_chunks(w_gate_d[0]),
              col_chunks(w_up_d[0]), wd_d)

    half = rope_dim // 2
    inv = ROPE_THETA ** (-jnp.arange(half, dtype=F32) / half)

    def tables(pos):
        ang = pos.astype(F32)[:, None] * inv[None, :]
        return (jnp.tile(jnp.cos(ang), (1, LANES // half)), jnp.tile(jnp.sin(ang), (1, LANES // half)))

    cos_p, sin_p = tables(jnp.arange(T, dtype=jnp.int32))
    cos_s, sin_s = tables(past_len + jnp.arange(TP, dtype=jnp.int32))
    reps = ATTN_TILE // TP
    cos_tab = jnp.concatenate([cos_p, jnp.tile(cos_s, (reps, 1))], axis=0)
    sin_tab = jnp.concatenate([sin_p, jnp.tile(sin_s, (reps, 1))], axis=0)

    w_kr = w_dkv[:, kv_lora:]
    wdkv = jnp.concatenate([w_dkv[:, :kv_lora], w_kr, _rot_cols(w_kr, half)], axis=1).astype(BF16)
    wuq = w_uq[0].reshape(-1, n_bheads, nope_dim + rope_dim)
    w_nope = wuq[:, :, :nope_dim].reshape(-1, n_bheads * nope_dim).astype(BF16)
    w_rope = wuq[:, :, nope_dim:]
    w_r = w_rope.reshape(-1, n_bheads * rope_dim).astype(BF16)
    w_rr = _rot_cols(w_rope, half).reshape(-1, n_bheads * rope_dim).astype(BF16)
    w_ukt = jnp.transpose(w_uk, (1, 2, 0)).astype(BF16)
    lat_weights = [g_kv.reshape(1, D), wdkv, g_ckv.reshape(1, kv_lora), g_norm_b,
                   w_dq[0].astype(BF16), g_q, w_nope, w_r, w_rr, w_ukt]
    ckv_p, ckv_s, kr_p, kr_s, kcat, kvt, qcat = _latq(
        x2, cos_tab, sin_tab, n_p // ATTN_TILE, T // ATTN_TILE, lat_weights, n_heads=n_bheads,
        kv_lora=kv_lora, rope_dim=rope_dim, nope_dim=nope_dim)

    scale = (nope_dim + rope_dim) ** -0.5
    f_m = w_gate_m.shape[3]
    o_lat, (wg_m, wu_m, wd_m) = _attn_prompt(
        qcat, kcat, kvt, n,
        [w_gate_m[0].reshape(n_experts * D, f_m), w_up_m[0].reshape(n_experts * D, f_m),
         w_down_m[0].reshape(n_experts * f_m, D)],
        B=B, T=T, kv_lora=kv_lora, scale=scale)

    kw = kv_lora + 2 * rope_dim
    q_s = qcat[n_p // ATTN_TILE:].reshape(-1, n_bheads, ATTN_TILE // TP, TP, kw)[:, :, :, :TS]
    q_s = q_s.transpose(0, 2, 1, 3, 4).reshape(DB, n_bheads * TS, kw)
    q_s = jnp.concatenate([q_s[..., :kv_lora],
                           q_s[..., kv_lora:kv_lora + rope_dim] + q_s[..., kv_lora + rope_dim:]], axis=-1)
    k_new = jnp.pad(kcat[n_p:].reshape(DB, TP, kw), ((0, 0), (0, LANES - TP), (0, 0)))
    o_s = _attn_sample(page_table, q_s, k_new, cache_ckv, jnp.swapaxes(cache_krope, 1, 2),
                       t_valid=TS, scale=scale)
    o_s = o_s.reshape(DB, n_bheads, TS, kv_lora).transpose(0, 2, 1, 3)
    o_s = jnp.pad(o_s, ((0, 0), (0, TP - TS), (0, 0), (0, 0))).reshape(n_s, n_bheads * kv_lora)
    o_lat = lax.dynamic_update_slice(o_lat, o_s.astype(BF16), (n_p, 0))

    w_uvh = jnp.transpose(w_uv, (1, 0, 2)).astype(BF16)
    w_rt = jnp.pad(w_router[0], ((0, 0), (0, LANES - n_experts))).astype(BF16)
    x3, xn_m, ids, wts = _attn_out(o_lat, x2, w_uvh, w_o_b[0].astype(BF16), g_ffn_m, w_rt,
                                   n_experts=n_experts)

    top_k = 2
    n2 = n * top_k
    assert n2 % MOE_TILE == 0
    order, pos1, pos2, starts, ends = _route(ids[:, 0], ids[:, 1], n_experts)
    row_token = order // top_k
    n_tiles = n2 // MOE_TILE
    parts = math.gcd(n_tiles, MOE_PARTS)
    part_tiles = n_tiles // parts
    weights = (wg_m.reshape(n_experts, D, f_m), wu_m.reshape(n_experts, D, f_m),
               wd_m.reshape(n_experts, f_m, D))
    pairs, n_pairs = _tile_expert_pairs(starts, ends, n_tiles, MOE_TILE)
    valid = jnp.arange(pairs[0].shape[0]) < n_pairs
    ys = None
    for p in range(parts):
        t0 = p * part_tiles
        before = jnp.sum(valid & (pairs[0] < t0))
        upto = jnp.sum(valid & (pairs[0] < t0 + part_tiles))
        window = jnp.stack([before, upto - before]).astype(jnp.int32)
        rows = row_token[t0 * MOE_TILE:(t0 + part_tiles) * MOE_TILE]
        ys = _moe(pairs, window, xn_m[rows], *weights, ys, t0, n2)
    y_p, y_s = _final(x3, ys[pos1], ys[pos2], wts, g_final.reshape(1, D), n_p)

    prompt = lambda a: a.reshape(B, T, a.shape[1])
    sample = lambda a: a.reshape(DB, TP, a.shape[1])[:, :TS]
    return (prompt(y_p), sample(y_s), c_p[None], n_pr[None], m_p[None], prompt(ckv_p), prompt(kr_p),
            c_s[None], n_sm[None], m_s[None], sample(ckv_s), sample(kr_s))
```
